```python
import math
import jax, jax.numpy as jnp
from jax import lax
import numpy as np

D_MODEL = 2048
BATCH = 1
SEQ = 8192
DEPTH = 4

GRID_W = 64
CTX_LEN = 256
CHUNK = 128
EPS = 1e-6

SSM_D = D_MODEL
SSM_HEADDIM = 64
SSM_HEADS = SSM_D // SSM_HEADDIM
SSM_GROUPS = 4
SSM_HPG = SSM_HEADS // SSM_GROUPS
SSM_STATE = 128
SSM_CONV = 5
XBC_D = SSM_D + 2 * SSM_GROUPS * SSM_STATE

RET_HEADS = 8
RET_DK = 256
RET_DV = 256
RET_QK = RET_HEADS * RET_DK
RET_D = RET_HEADS * RET_DV
ROPE_FREQS = RET_DK // 4
ROPE_BASE = 10000.0

N_GROUPS_E = 4
EXPERTS_PER_GROUP = 8
N_EXPERTS = N_GROUPS_E * EXPERTS_PER_GROUP
TOP_K_IN_GROUP = 2
D_EXPERT = 512
MOE_BLOCK = 128

IN_SIZES = (SSM_D, XBC_D, SSM_HEADS, RET_QK, RET_QK, RET_D, RET_D, 2 * D_MODEL)
IN_D = sum(IN_SIZES)
IN_SPLIT = tuple(sum(IN_SIZES[:i + 1]) for i in range(len(IN_SIZES) - 1))

kernel_name = 'hybrid_ssd_retention_hmoe_prefix_dit'


def rmsnorm(u, w):
    uf = u.astype(jnp.float32)
    y = uf * lax.rsqrt(jnp.mean(uf * uf, axis=-1, keepdims=True) + EPS)
    return (y * w.astype(jnp.float32)).astype(u.dtype)


def modulate(u, w, shift, scale):
    return rmsnorm(u, w) * (1 + scale) + shift


def flip(u):
    return jnp.flip(u, axis=1)


def depthwise_conv(u, w, b):
    ch = u.shape[-1]
    out = lax.conv_general_dilated(u, w[:, None, :].astype(u.dtype), window_strides=(1,),
                                   padding=[(SSM_CONV // 2, SSM_CONV // 2)],
                                   dimension_numbers=('NWC', 'WIO', 'NWC'), feature_group_count=ch)
    return out + b.astype(u.dtype)


def rope_rotate(u, ang):
    u1, u2 = jnp.split(u, 2, axis=-1)
    cos = jnp.cos(ang)[None, :, None, :]
    sin = jnp.sin(ang)[None, :, None, :]
    return jnp.concatenate([u1 * cos - u2 * sin, u1 * sin + u2 * cos], axis=-1)


def rope2d(u, ang_row, ang_col):
    ur, uc = jnp.split(u, 2, axis=-1)
    return jnp.concatenate([rope_rotate(ur, ang_row), rope_rotate(uc, ang_col)], axis=-1)


def ssd_chunked(xs, dt, A, Bm, Cm, h0):
    b, L = xs.shape[:2]
    nc = L // CHUNK
    G, Hg, P, N = SSM_GROUPS, SSM_HPG, SSM_HEADDIM, SSM_STATE
    a = jnp.moveaxis((dt * A).reshape(b, nc, CHUNK, G, Hg), 2, -1)
    a_cum = jnp.cumsum(a, axis=-1)
    xdt = (xs * dt[..., None]).reshape(b, nc, CHUNK, G, Hg, P)
    Bc = Bm.reshape(b, nc, CHUNK, G, N)
    Cc = Cm.reshape(b, nc, CHUNK, G, N)
    lower = jnp.tril(jnp.ones((CHUNK, CHUNK), dtype=bool))
    seg = a_cum[..., :, None] - a_cum[..., None, :]
    Lmat = jnp.where(lower, jnp.exp(jnp.where(lower, seg, 0.0)), 0.0)
    CB = jnp.einsum('bcign,bcjgn->bcgij', Cc, Bc)
    y_diag = jnp.einsum('bcghij,bcjghp->bcighp', CB[:, :, :, None] * Lmat, xdt)
    decay_to_end = jnp.moveaxis(jnp.exp(a_cum[..., -1:] - a_cum), -1, 2)
    states = jnp.einsum('bcjgn,bcjghp->bcghpn', Bc, xdt * decay_to_end[..., None])
    chunk_decay = jnp.exp(a_cum[..., -1])

    def step(h, inp):
        s, d = inp
        return h * d[..., None, None] + s, h

    h_final, h_prev = lax.scan(step, h0, (jnp.moveaxis(states, 1, 0), jnp.moveaxis(chunk_decay, 1, 0)))
    h_prev = jnp.moveaxis(h_prev, 0, 1)
    decay_from_start = jnp.moveaxis(jnp.exp(a_cum), -1, 2)[..., None]
    y_off = jnp.einsum('bcign,bcghpn->bcighp', Cc, h_prev) * decay_from_start
    return (y_diag + y_off).reshape(b, L, G, Hg, P), h_final


def ssd_direction(xs, dt_raw, Bm, Cm, A_log, dt_bias, h0):
    b, L = xs.shape[:2]
    dt = jax.nn.softplus(dt_raw.astype(jnp.float32) + dt_bias.astype(jnp.float32))
    dt = dt.reshape(b, L, SSM_GROUPS, SSM_HPG)
    A = -jnp.exp(A_log.astype(jnp.float32)).reshape(SSM_GROUPS, SSM_HPG)
    return ssd_chunked(xs.astype(jnp.float32), dt, A, Bm.astype(jnp.float32), Cm.astype(jnp.float32), h0)


def retention_chunked(q, k, v, log_gamma, R0):
    b, L, H, dk = q.shape
    dv = v.shape[-1]
    nc = L // CHUNK
    qc = q.reshape(b, nc, CHUNK, H, dk)
    kc = k.reshape(b, nc, CHUNK, H, dk)
    vc = v.reshape(b, nc, CHUNK, H, dv)
    idx = jnp.arange(CHUNK, dtype=jnp.float32)
    diff = idx[:, None] - idx[None, :]
    Dmat = jnp.where(diff >= 0, jnp.exp(jnp.maximum(diff, 0.0)[None] * log_gamma[:, None, None]), 0.0)
    scores = jnp.einsum('bcihd,bcjhd->bchij', qc, kc) * Dmat
    y_intra = jnp.einsum('bchij,bcjhv->bcihv', scores, vc)
    k_decay = jnp.exp((CHUNK - 1 - idx)[:, None] * log_gamma[None, :])
    kv = jnp.einsum('bcjhd,bcjhv->bchdv', kc * k_decay[:, :, None], vc)
    chunk_decay = jnp.exp(CHUNK * log_gamma)

    def step(R, kv_c):
        return R * chunk_decay[None, :, None, None] + kv_c, R

    R_final, R_prev = lax.scan(step, R0, jnp.moveaxis(kv, 1, 0))
    R_prev = jnp.moveaxis(R_prev, 0, 1)
    q_decay = jnp.exp((idx + 1)[:, None] * log_gamma[None, :])
    y_inter = jnp.einsum('bcihd,bchdv->bcihv', qc * q_decay[:, :, None], R_prev)
    return (y_intra + y_inter).reshape(b, L, H, dv), R_final


def ret_direction(q, k, v, decay_logit, R0):
    log_gamma = jax.nn.log_sigmoid(decay_logit.astype(jnp.float32))
    return retention_chunked(q.astype(jnp.float32), k.astype(jnp.float32), v.astype(jnp.float32), log_gamma, R0)


def bidir(scan_fn, seqs, params_fwd, params_bwd, s0_fwd, s0_bwd):
    y_f, s_f = scan_fn(*seqs, *params_fwd, s0_fwd)
    y_b, s_b = scan_fn(*[flip(u) for u in seqs], *params_bwd, s0_bwd)
    return y_f + flip(y_b), s_f, s_b


def mixer_inputs(xn, w_in, conv_w, conv_b, ang):
    b, L, _ = xn.shape
    z, xbc, dt_raw, q, k, v, g, gates = jnp.split(xn @ w_in, IN_SPLIT, axis=-1)
    xbc = jax.nn.silu(depthwise_conv(xbc, conv_w, conv_b))
    xs, Bm, Cm = jnp.split(xbc, [SSM_D, SSM_D + SSM_GROUPS * SSM_STATE], axis=-1)
    xs = xs.reshape(b, L, SSM_GROUPS, SSM_HPG, SSM_HEADDIM)
    Bm = Bm.reshape(b, L, SSM_GROUPS, SSM_STATE)
    Cm = Cm.reshape(b, L, SSM_GROUPS, SSM_STATE)
    q = q.reshape(b, L, RET_HEADS, RET_DK)
    k = k.reshape(b, L, RET_HEADS, RET_DK) * (RET_DK ** -0.5)
    v = v.reshape(b, L, RET_HEADS, RET_DV)
    if ang is not None:
        q = rope2d(q, ang[0], ang[1])
        k = rope2d(k, ang[0], ang[1])
    return z, xs, Bm, Cm, dt_raw, q, k, v, g, gates


def mixer_output(z, xs, g, gates, y_ssd, y_ret, D_skip, ssm_norm_w, ret_norm_w, w_ssd_proj, w_ret_proj, w_out):
    b, L = z.shape[:2]
    dt_out = z.dtype
    y = y_ssd + D_skip.astype(jnp.float32).reshape(SSM_GROUPS, SSM_HPG, 1) * xs.astype(jnp.float32)
    y = rmsnorm(y.reshape(b, L, SSM_D) * jax.nn.silu(z.astype(jnp.float32)), ssm_norm_w)
    r = y_ret * lax.rsqrt(jnp.mean(y_ret * y_ret, axis=-1, keepdims=True) + EPS)
    r = r.reshape(b, L, RET_D) * ret_norm_w.astype(jnp.float32) * jax.nn.silu(g.astype(jnp.float32))
    gate_ssd, gate_ret = jnp.split(jax.nn.sigmoid(gates.astype(jnp.float32)), 2, axis=-1)
    merged = gate_ssd * (y.astype(dt_out) @ w_ssd_proj) + gate_ret * (r.astype(dt_out) @ w_ret_proj)
    return merged.astype(dt_out) @ w_out


def hier_moe(xn, w_rg, b_rg, w_re, b_re, w_gate, w_up, w_down):
    b, L, D = xn.shape
    t = xn.reshape(b * L, D)
    p_group = jax.nn.softmax((t @ w_rg + b_rg).astype(jnp.float32), axis=-1)
    pg_top, g_idx = lax.top_k(p_group, 1)
    logits_e = (t @ w_re + b_re).astype(jnp.float32).reshape(b * L, N_GROUPS_E, EXPERTS_PER_GROUP)
    logits_in = jnp.einsum('tge,tg->te', logits_e, jax.nn.one_hot(g_idx[:, 0], N_GROUPS_E, dtype=jnp.float32))
    p_in = jax.nn.softmax(logits_in, axis=-1)
    w_top, e_top = lax.top_k(p_in, TOP_K_IN_GROUP)
    w_top = pg_top * w_top / jnp.sum(w_top, axis=-1, keepdims=True)
    expert_idx = g_idx * EXPERTS_PER_GROUP + e_top
    combine = jnp.sum(jax.nn.one_hot(expert_idx, N_EXPERTS, dtype=jnp.float32) * w_top[..., None], axis=1)

    def block(args):
        tk, ck = args
        h = jax.nn.silu(jnp.einsum('td,edf->tef', tk, w_gate)) * jnp.einsum('td,edf->tef', tk, w_up)
        return jnp.einsum('tef,efd->td', h * ck[..., None], w_down)

    out = lax.map(block, (t.reshape(-1, MOE_BLOCK, D), combine.reshape(-1, MOE_BLOCK, N_EXPERTS)))
    return out.reshape(b, L, D).astype(xn.dtype)


def setup_inputs(seed: int = 0) -> dict:
    key = jax.random.key(seed)
    ks = jax.random.split(key, 32)
    f32 = jnp.float32

    def nrm(k, shape, scale):
        return jax.random.normal(k, shape, f32) * scale

    def gain(k, shape):
        return 1.0 + 0.01 * jax.random.normal(k, shape, f32)

    dt_init = jnp.exp(jax.random.uniform(ks[10], (DEPTH, 2, SSM_HEADS), f32, math.log(1e-3), math.log(1e-1)))
    heads = jnp.arange(RET_HEADS, dtype=f32)
    ret_base = jnp.log(jnp.exp2(5.0 + heads) - 1.0)
    return {
        'x': nrm(ks[0], (BATCH, SEQ, D_MODEL), 1.0),
        'c': nrm(ks[1], (BATCH, D_MODEL), 1.0),
        'ctx': nrm(ks[2], (BATCH, CTX_LEN, D_MODEL), 1.0),
        'c_ctx': nrm(ks[3], (D_MODEL,), 1.0),
        'w_mod': nrm(ks[4], (DEPTH, D_MODEL, 6 * D_MODEL), 0.5 * D_MODEL ** -0.5),
        'b_mod': nrm(ks[5], (DEPTH, 6 * D_MODEL), 0.01),
        'norm1_w': gain(ks[6], (DEPTH, D_MODEL)),
        'w_in': nrm(ks[7], (DEPTH, D_MODEL, IN_D), D_MODEL ** -0.5),
        'conv_w': nrm(ks[8], (DEPTH, SSM_CONV, XBC_D), SSM_CONV ** -0.5),
        'conv_b': nrm(ks[9], (DEPTH, XBC_D), 0.01),
        'ssm_A_log': jnp.log(jax.random.uniform(ks[11], (DEPTH, 2, SSM_HEADS), f32, 1.0, 16.0)),
        'ssm_dt_bias': dt_init + jnp.log(-jnp.expm1(-dt_init)),
        'ssm_D': gain(ks[12], (DEPTH, SSM_HEADS)),
        'ssm_norm_w': gain(ks[13], (DEPTH, SSM_D)),
        'ret_decay_logit': ret_base + nrm(ks[14], (DEPTH, 2, RET_HEADS), 0.05),
        'ret_norm_w': gain(ks[15], (DEPTH, RET_D)),
        'w_ssd_proj': nrm(ks[16], (DEPTH, SSM_D, D_MODEL), SSM_D ** -0.5),
        'w_ret_proj': nrm(ks[17], (DEPTH, RET_D, D_MODEL), RET_D ** -0.5),
        'w_out': nrm(ks[18], (DEPTH, D_MODEL, D_MODEL), D_MODEL ** -0.5),
        'norm2_w': gain(ks[19], (DEPTH, D_MODEL)),
        'w_route_group': nrm(ks[20], (DEPTH, D_MODEL, N_GROUPS_E), D_MODEL ** -0.5),
        'b_route_group': nrm(ks[21], (DEPTH, N_GROUPS_E), 0.01),
        'w_route_expert': nrm(ks[22], (DEPTH, D_MODEL, N_EXPERTS), D_MODEL ** -0.5),
        'b_route_expert': nrm(ks[23], (DEPTH, N_EXPERTS), 0.01),
        'w_exp_gate': nrm(ks[24], (DEPTH, N_EXPERTS, D_MODEL, D_EXPERT), D_MODEL ** -0.5),
        'w_exp_up': nrm(ks[25], (DEPTH, N_EXPERTS, D_MODEL, D_EXPERT), D_MODEL ** -0.5),
        'w_exp_down': nrm(ks[26], (DEPTH, N_EXPERTS, D_EXPERT, D_MODEL), D_EXPERT ** -0.5),
        'final_norm_w': gain(ks[27], (D_MODEL,)),
    }


def reference(x, c, ctx, c_ctx, w_mod, b_mod, norm1_w, w_in, conv_w, conv_b, ssm_A_log, ssm_dt_bias,
              ssm_D, ssm_norm_w, ret_decay_logit, ret_norm_w, w_ssd_proj, w_ret_proj, w_out, norm2_w,
              w_route_group, b_route_group, w_route_expert, b_route_expert, w_exp_gate, w_exp_up,
              w_exp_down, final_norm_w):
    b, T, _ = x.shape
    ROWS = T // GRID_W
    row = jnp.broadcast_to(jnp.arange(ROWS, dtype=jnp.float32)[:, None], (ROWS, GRID_W)).reshape(T)
    col = jnp.broadcast_to(jnp.arange(GRID_W, dtype=jnp.float32)[None, :], (ROWS, GRID_W)).reshape(T)
    inv_freq = ROPE_BASE ** (-jnp.arange(ROPE_FREQS, dtype=jnp.float32) / ROPE_FREQS)
    ang = (row[:, None] * inv_freq[None, :], col[:, None] * inv_freq[None, :])
    h0 = jnp.zeros((b, SSM_GROUPS, SSM_HPG, SSM_HEADDIM, SSM_STATE), jnp.float32)
    r0 = jnp.zeros((b, RET_HEADS, RET_DK, RET_DV), jnp.float32)

    for l in range(DEPTH):
        mod_x = (jax.nn.silu(c) @ w_mod[l] + b_mod[l])[:, None, :]
        mod_c = (jax.nn.silu(c_ctx) @ w_mod[l] + b_mod[l])[None, None, :]
        sh1, sc1, g1, sh2, sc2, g2 = jnp.split(mod_x, 6, axis=-1)
        csh1, csc1, cg1, csh2, csc2, cg2 = jnp.split(mod_c, 6, axis=-1)

        zc, xsc, Bc, Cc, dtc, qc, kc, vc, gc, gatc = mixer_inputs(
            modulate(ctx, norm1_w[l], csh1, csc1), w_in[l], conv_w[l], conv_b[l], None)
        zx, xsx, Bx, Cx, dtx, qx, kx, vx, gx, gatx = mixer_inputs(
            modulate(x, norm1_w[l], sh1, sc1), w_in[l], conv_w[l], conv_b[l], ang)

        ssd_f = (ssm_A_log[l, 0], ssm_dt_bias[l, 0])
        ssd_b = (ssm_A_log[l, 1], ssm_dt_bias[l, 1])
        ys_c, hs_f, hs_b = bidir(ssd_direction, (xsc, dtc, Bc, Cc), ssd_f, ssd_b, h0, h0)
        ys_x, _, _ = bidir(ssd_direction, (xsx, dtx, Bx, Cx), ssd_f, ssd_b, hs_f, hs_b)
        ret_f = (ret_decay_logit[l, 0],)
        ret_b = (ret_decay_logit[l, 1],)
        yr_c, rs_f, rs_b = bidir(ret_direction, (qc, kc, vc), ret_f, ret_b, r0, r0)
        yr_x, _, _ = bidir(ret_direction, (qx, kx, vx), ret_f, ret_b, rs_f, rs_b)

        x = x + g1 * mixer_output(zx, xsx, gx, gatx, ys_x, yr_x, ssm_D[l], ssm_norm_w[l], ret_norm_w[l],
                                  w_ssd_proj[l], w_ret_proj[l], w_out[l])
        x = x + g2 * hier_moe(modulate(x, norm2_w[l], sh2, sc2), w_route_group[l], b_route_group[l],
                              w_route_expert[l], b_route_expert[l], w_exp_gate[l], w_exp_up[l], w_exp_down[l])
        if l < DEPTH - 1:
            ctx = ctx + cg1 * mixer_output(zc, xsc, gc, gatc, ys_c, yr_c, ssm_D[l], ssm_norm_w[l], ret_norm_w[l],
                                           w_ssd_proj[l], w_ret_proj[l], w_out[l])
            ctx = ctx + cg2 * hier_moe(modulate(ctx, norm2_w[l], csh2, csc2), w_route_group[l], b_route_group[l],
                                       w_route_expert[l], b_route_expert[l], w_exp_gate[l], w_exp_up[l], w_exp_down[l])

    return rmsnorm(x, final_norm_w)
```

```python
import functools
import math

import jax
import jax.numpy as jnp
from jax import lax
from jax.experimental import pallas as pl
from jax.experimental.pallas import tpu as pltpu

F32 = jnp.float32
BF16 = jnp.bfloat16
HIGHEST = lax.Precision.HIGHEST

D_MODEL = 2048
GRID_W = 64
CHUNK = 128
EPS = 1e-6

SSM_HEADDIM = 64
SSM_HEADS = 32
SSM_GROUPS = 4
SSM_HPG = 8
SSM_STATE = 128
SSM_CONV = 5
GROUP_W = SSM_HPG * SSM_HEADDIM
BC_W = SSM_GROUPS * SSM_STATE
XBC_D = D_MODEL + 2 * BC_W

RET_HEADS = 8
RET_DK = 256
ROPE_FREQS = 64
ROPE_BASE = 10000.0

N_GROUPS_E = 4
EXPERTS_PER_GROUP = 8
N_EXPERTS = 32
D_EXPERT = 512

LANES = 128
MAIN_W = 7 * D_MODEL + XBC_D
XBC_COL0 = 7 * D_MODEL

ROW_BLOCK = 768
COL_BLOCK = 1024
MERGE_ROW_BLOCK = 384
MERGE_COL_BLOCK = 512
TOK_BLOCK = 256
EXPERT_TILE = 256
NEG_BIG = -1e30
VMEM_LIMIT = 56 * 1024 * 1024


def _params(sem, vmem=VMEM_LIMIT):
    return pltpu.CompilerParams(dimension_semantics=sem, vmem_limit_bytes=vmem)


def _silu(v):
    return v * jax.nn.sigmoid(v)


def _softplus(v):
    return jnp.maximum(v, 0.0) + jnp.log1p(jnp.exp(-jnp.abs(v)))


def _modulated_norm(x, nw, sh_ref, sc_ref, row0, n_ctx):
    ms = jnp.mean(x * x, axis=-1, keepdims=True)
    y = x * lax.rsqrt(ms + EPS) * nw
    row = row0 + lax.broadcasted_iota(jnp.int32, (x.shape[0], 1), 0)
    is_ctx = row < n_ctx
    sc = jnp.where(is_ctx, sc_ref[1:2, :], sc_ref[0:1, :])
    sh = jnp.where(is_ctx, sh_ref[1:2, :], sh_ref[0:1, :])
    return y * (1.0 + sc) + sh


def _mod_kernel(a_ref, w_ref, b_ref, o_ref):
    a = a_ref[...]
    o_ref[...] = jnp.dot(_silu(a), w_ref[...], precision=HIGHEST,
                         preferred_element_type=F32) + b_ref[...]


def _modulation(cvec, w_mod, b_mod):
    depth, d, n = w_mod.shape
    bn = 1024
    return pl.pallas_call(
        _mod_kernel,
        grid=(depth, n // bn),
        in_specs=[pl.BlockSpec((8, d), lambda l, j: (0, 0)),
                  pl.BlockSpec((None, d, bn), lambda l, j: (l, 0, j)),
                  pl.BlockSpec((None, 1, bn), lambda l, j: (l, 0, j))],
        out_specs=pl.BlockSpec((None, 8, bn), lambda l, j: (l, 0, j)),
        out_shape=jax.ShapeDtypeStruct((depth, 8, n), F32),
        compiler_params=_params(("arbitrary", "arbitrary")),
        name="modulation",
    )(cvec, w_mod, b_mod.reshape(depth, 1, n))


def _inproj_kernel(h_ref, sh_ref, sc_ref, nw_ref, w_ref, wdt_ref, cos_ref, sin_ref,
                   o_ref, dt_ref, xn_s, *, n_ctx, bm, bn):
    i = pl.program_id(0)
    j = pl.program_id(1)

    @pl.when(j == 0)
    def _():
        xn = _modulated_norm(h_ref[...], nw_ref[...], sh_ref, sc_ref, i * bm, n_ctx)
        xn_s[...] = xn.astype(BF16)
        dt_ref[...] = jnp.dot(xn, wdt_ref[...], precision=HIGHEST, preferred_element_type=F32)

    acc = jnp.dot(xn_s[...], w_ref[...], preferred_element_type=F32)
    q_lo = D_MODEL // bn
    k_lo = 2 * D_MODEL // bn
    k_hi = 3 * D_MODEL // bn
    is_rope = (j >= q_lo) & (j < k_hi)

    @pl.when(is_rope)
    def _():
        scale = jnp.where(j >= k_lo, RET_DK ** -0.5, 1.0).astype(F32)
        for gi in range(bn // LANES):
            u = acc[:, gi * LANES:(gi + 1) * LANES]
            half = gi % 2
            c = cos_ref[:, half * LANES:(half + 1) * LANES]
            s = sin_ref[:, half * LANES:(half + 1) * LANES]
            r = (u * c + pltpu.roll(u, LANES // 2, 1) * s) * scale
            o_ref[:, gi * LANES:(gi + 1) * LANES] = r.astype(BF16)

    @pl.when(jnp.logical_not(is_rope))
    def _():
        o_ref[...] = acc.astype(BF16)


def _inproj(h, mod, norm_w, w_main, w_dt, cos_t, sin_t, l, n_ctx):
    t, d = h.shape
    bm, bn = ROW_BLOCK, COL_BLOCK
    nj = MAIN_W // bn
    kern = functools.partial(_inproj_kernel, n_ctx=n_ctx, bm=bm, bn=bn)
    return pl.pallas_call(
        kern,
        grid=(t // bm, nj),
        in_specs=[pl.BlockSpec((bm, d), lambda i, j: (i, 0)),
                  pl.BlockSpec((None, 8, d), lambda i, j: (l, 0, 0)),
                  pl.BlockSpec((None, 8, d), lambda i, j: (l, 0, 1)),
                  pl.BlockSpec((None, 1, d), lambda i, j: (l, 0, 0)),
                  pl.BlockSpec((None, d, bn), lambda i, j: (l, 0, j)),
                  pl.BlockSpec((None, d, LANES), lambda i, j: (l, 0, 0)),
                  pl.BlockSpec((bm, 2 * LANES), lambda i, j: (i, 0)),
                  pl.BlockSpec((bm, 2 * LANES), lambda i, j: (i, 0))],
        out_specs=[pl.BlockSpec((bm, bn), lambda i, j: (i, j)),
                   pl.BlockSpec((bm, LANES), lambda i, j: (i, 0))],
        out_shape=[jax.ShapeDtypeStruct((t, MAIN_W), BF16),
                   jax.ShapeDtypeStruct((t, LANES), F32)],
        scratch_shapes=[pltpu.VMEM((bm, d), BF16)],
        compiler_params=_params(("arbitrary", "arbitrary")),
        name="inproj",
    )(h, mod, mod, norm_w, w_main, w_dt, cos_t, sin_t)


def _conv_kernel(x_ref, w_ref, b_ref, o_ref, pad_s, *, n_ctx, n_lat, rows):
    halo = 8
    zeros = jnp.zeros((halo, LANES), F32)
    c0 = halo
    x0 = 2 * halo + n_ctx
    pad_s[0:halo, :] = zeros
    pad_s[c0:c0 + n_ctx, :] = x_ref[0:n_ctx, :].astype(F32)
    pad_s[c0 + n_ctx:x0, :] = zeros
    pad_s[x0:x0 + n_lat, :] = x_ref[n_ctx:n_ctx + n_lat, :].astype(F32)
    pad_s[x0 + n_lat:x0 + n_lat + halo, :] = zeros
    w = w_ref[...]
    b = b_ref[...]

    def segment(src0, dst0, n):
        def body(s, carry):
            r0 = pl.multiple_of(s * rows, rows)
            acc = jnp.broadcast_to(b, (rows, LANES))
            for kk in range(SSM_CONV):
                acc = acc + w[kk:kk + 1, :] * pad_s[pl.ds(src0 + r0 + kk - SSM_CONV // 2, rows), :]
            o_ref[pl.ds(dst0 + r0, rows), :] = _silu(acc).astype(BF16)
            return carry
        lax.fori_loop(0, n // rows, body, 0)

    segment(c0, 0, n_ctx)
    segment(x0, n_ctx, n_lat)


def _conv(p_main, conv_w, conv_b, l, n_ctx):
    t = p_main.shape[0]
    n_lat = t - n_ctx
    col0 = XBC_COL0 // LANES
    kern = functools.partial(_conv_kernel, n_ctx=n_ctx, n_lat=n_lat, rows=256)
    return pl.pallas_call(
        kern,
        grid=(XBC_D // LANES,),
        in_specs=[pl.BlockSpec((t, LANES), lambda c: (0, col0 + c)),
                  pl.BlockSpec((None, SSM_CONV, LANES), lambda c: (l, 0, c)),
                  pl.BlockSpec((None, 1, LANES), lambda c: (l, 0, c))],
        out_specs=pl.BlockSpec((t, LANES), lambda c: (0, c)),
        out_shape=jax.ShapeDtypeStruct((t, XBC_D), BF16),
        scratch_shapes=[pltpu.VMEM((t + 24, LANES), F32)],
        compiler_params=_params(("arbitrary",)),
        name="conv_silu",
    )(p_main, conv_w, conv_b)


def _chunk_index(s, nc_ctx, nc, rev):
    if not rev:
        return s
    return jnp.where(s < nc_ctx, nc_ctx - 1 - s, nc - 1 - (s - nc_ctx))


def _split_dot(v, e):
    hi = v.astype(BF16)
    lo = (v - hi.astype(F32)).astype(BF16)
    return (jnp.dot(hi, e, preferred_element_type=F32) + jnp.dot(lo, e, preferred_element_type=F32))


def _ssd_kernel(*refs, rev, has_prev):
    if has_prev:
        xs_ref, b_ref, c_ref, dt_ref, bias_ref, alog_ref, e_ref, prev_ref, o_ref, h_s = refs
    else:
        xs_ref, b_ref, c_ref, dt_ref, bias_ref, alog_ref, e_ref, o_ref, h_s = refs
        prev_ref = None
    q = CHUNK

    @pl.when(pl.program_id(0) == 0)
    def _():
        h_s[...] = jnp.zeros_like(h_s)

    ii = lax.broadcasted_iota(jnp.int32, (q, q), 0)
    jj = lax.broadcasted_iota(jnp.int32, (q, q), 1)
    seen = (jj >= ii) if rev else (jj <= ii)
    tri = jnp.where(seen, 1.0, 0.0).astype(F32)

    dt = _softplus(dt_ref[...] + bias_ref[...])
    a = dt * (-jnp.exp(alog_ref[...]))
    cum = jnp.dot(tri, a, precision=HIGHEST, preferred_element_type=F32)
    cum_t = cum.T
    dt_t = dt.T
    total = cum[0:1, :] if rev else cum[q - 1:q, :]
    dfs = jnp.exp(cum)
    w_end = dt * jnp.exp(total - cum)
    cdec = jnp.broadcast_to(jnp.exp(total), (8, LANES))
    e = e_ref[...]
    expanded = _split_dot(jnp.concatenate([w_end, dfs, cdec], axis=0), e)
    w_end_x = expanded[0:q]
    dfs_x = expanded[q:2 * q]
    cdec_x = expanded[2 * q:2 * q + 1]

    xs = xs_ref[...]
    xw = (xs.astype(F32) * w_end_x).astype(BF16)

    for g in range(SSM_GROUPS):
        cg = c_ref[:, g * SSM_STATE:(g + 1) * SSM_STATE]
        bg = b_ref[:, g * SSM_STATE:(g + 1) * SSM_STATE]
        gs = slice(g * GROUP_W, (g + 1) * GROUP_W)
        cb = lax.dot_general(cg, bg, (((1,), (1,)), ((), ())), preferred_element_type=F32)
        h_prev = h_s[:, gs]
        y_off = jnp.dot(cg, h_prev.astype(BF16), preferred_element_type=F32) * dfs_x[:, gs]
        new = lax.dot_general(bg, xw[:, gs], (((0,), (0,)), ((), ())), preferred_element_type=F32)
        pieces = []
        for hh in range(SSM_HPG):
            hd = g * SSM_HPG + hh
            seg = cum[:, hd:hd + 1] - cum_t[hd:hd + 1, :]
            m = jnp.exp(jnp.where(seen, seg, NEG_BIG)) * cb * dt_t[hd:hd + 1, :]
            pieces.append(jnp.dot(m.astype(BF16), xs[:, hd * SSM_HEADDIM:(hd + 1) * SSM_HEADDIM],
                                  preferred_element_type=F32))
        y = jnp.concatenate(pieces, axis=1) + y_off
        if prev_ref is not None:
            y = y + prev_ref[:, gs]
        o_ref[:, gs] = y
        h_s[:, gs] = h_prev * cdec_x[:, gs] + new


def _ssd_scan(xbc_act, dt_raw, dt_bias, a_log, e_mat, prev, n_ctx, rev):
    t = xbc_act.shape[0]
    nc, nc_ctx = t // CHUNK, n_ctx // CHUNK
    cidx = functools.partial(_chunk_index, nc_ctx=nc_ctx, nc=nc, rev=rev)
    in_specs = [pl.BlockSpec((CHUNK, D_MODEL), lambda s: (cidx(s), 0)),
                pl.BlockSpec((CHUNK, BC_W), lambda s: (cidx(s), D_MODEL // BC_W)),
                pl.BlockSpec((CHUNK, BC_W), lambda s: (cidx(s), D_MODEL // BC_W + 1)),
                pl.BlockSpec((CHUNK, LANES), lambda s: (cidx(s), 0)),
                pl.BlockSpec((1, LANES), lambda s: (0, 0)),
                pl.BlockSpec((1, LANES), lambda s: (0, 0)),
                pl.BlockSpec((LANES, D_MODEL), lambda s: (0, 0))]
    args = [xbc_act, xbc_act, xbc_act, dt_raw, dt_bias, a_log, e_mat]
    if prev is not None:
        in_specs.append(pl.BlockSpec((CHUNK, D_MODEL), lambda s: (cidx(s), 0)))
        args.append(prev)
    return pl.pallas_call(
        functools.partial(_ssd_kernel, rev=rev, has_prev=prev is not None),
        grid=(nc,),
        in_specs=in_specs,
        out_specs=pl.BlockSpec((CHUNK, D_MODEL), lambda s: (cidx(s), 0)),
        out_shape=jax.ShapeDtypeStruct((t, D_MODEL), F32),
        scratch_shapes=[pltpu.VMEM((SSM_STATE, D_MODEL), F32)],
        compiler_params=_params(("arbitrary",)),
        name="ssd_bwd" if rev else "ssd_fwd",
    )(*args)


def _ret_kernel(*refs, rev, has_prev):
    if has_prev:
        q_ref, k_ref, v_ref, dl_ref, prev_ref, o_ref, r_s, dm_s, kd_s, qd_s = refs
    else:
        q_ref, k_ref, v_ref, dl_ref, o_ref, r_s, dm_s, kd_s, qd_s = refs
        prev_ref = None
    qn = CHUNK
    lg_all = -_softplus(-dl_ref[...])

    @pl.when(pl.program_id(0) == 0)
    def _():
        r_s[...] = jnp.zeros_like(r_s)
        ii = lax.broadcasted_iota(jnp.int32, (qn, qn), 0)
        jj = lax.broadcasted_iota(jnp.int32, (qn, qn), 1)
        diff = (jj - ii) if rev else (ii - jj)
        dpos = jnp.maximum(diff, 0).astype(F32)
        k_exp = (ii if rev else (qn - 1 - ii)).astype(F32)
        q_exp = ((qn - ii) if rev else (ii + 1)).astype(F32)
        for hd in range(RET_HEADS):
            lg = lg_all[:, hd:hd + 1]
            dm_s[hd] = jnp.where(diff >= 0, jnp.exp(dpos * lg), 0.0)
            kd_s[hd] = jnp.exp(k_exp * lg)
            qd_s[hd] = jnp.exp(q_exp * lg)

    for hd in range(RET_HEADS):
        hs = slice(hd * RET_DK, (hd + 1) * RET_DK)
        qh = q_ref[:, hs]
        kh = k_ref[:, hs]
        vh = v_ref[:, hs]
        sc = lax.dot_general(qh, kh, (((1,), (1,)), ((), ())), preferred_element_type=F32)
        y = jnp.dot((sc * dm_s[hd]).astype(BF16), vh, preferred_element_type=F32)
        r_prev = r_s[hd]
        qd = qd_s[hd]
        y_int = jnp.dot(qh, r_prev.astype(BF16), preferred_element_type=F32)
        y = y + y_int * jnp.concatenate([qd, qd], axis=1)
        kd = kd_s[hd]
        kdec = (kh.astype(F32) * jnp.concatenate([kd, kd], axis=1)).astype(BF16)
        kv = lax.dot_general(kdec, vh, (((0,), (0,)), ((), ())), preferred_element_type=F32)
        cdec = jnp.exp(float(qn) * lg_all[:, hd:hd + 1])
        r_s[hd] = r_prev * cdec + kv
        if prev_ref is not None:
            y = y + prev_ref[:, hs]
        o_ref[:, hs] = y


def _ret_scan(p_main, decay_logit, prev, n_ctx, rev):
    t = p_main.shape[0]
    nc, nc_ctx = t // CHUNK, n_ctx // CHUNK
    cidx = functools.partial(_chunk_index, nc_ctx=nc_ctx, nc=nc, rev=rev)
    in_specs = [pl.BlockSpec((CHUNK, D_MODEL), lambda s: (cidx(s), 1)),
                pl.BlockSpec((CHUNK, D_MODEL), lambda s: (cidx(s), 2)),
                pl.BlockSpec((CHUNK, D_MODEL), lambda s: (cidx(s), 3)),
                pl.BlockSpec((1, LANES), lambda s: (0, 0))]
    args = [p_main, p_main, p_main, decay_logit]
    if prev is not None:
        in_specs.append(pl.BlockSpec((CHUNK, D_MODEL), lambda s: (cidx(s), 0)))
        args.append(prev)
    return pl.pallas_call(
        functools.partial(_ret_kernel, rev=rev, has_prev=prev is not None),
        grid=(nc,),
        in_specs=in_specs,
        out_specs=pl.BlockSpec((CHUNK, D_MODEL), lambda s: (cidx(s), 0)),
        out_shape=jax.ShapeDtypeStruct((t, D_MODEL), F32),
        scratch_shapes=[pltpu.VMEM((RET_HEADS, RET_DK, RET_DK), F32),
                        pltpu.VMEM((RET_HEADS, CHUNK, CHUNK), F32),
                        pltpu.VMEM((RET_HEADS, CHUNK, CHUNK), F32),
                        pltpu.VMEM((RET_HEADS, CHUNK, CHUNK), F32)],
        compiler_params=_params(("arbitrary",)),
        name="ret_bwd" if rev else "ret_fwd",
    )(*args)


def _merge_kernel(ys_ref, xs_ref, z_ref, yr_ref, g_ref, gs_ref, gr_ref, dsk_ref, snw_ref, rnw_ref,
                  ws_ref, wr_ref, o_ref, a_s, b_s):
    @pl.when(pl.program_id(1) == 0)
    def _():
        y = ys_ref[...] + dsk_ref[...] * xs_ref[...].astype(F32)
        y = y * _silu(z_ref[...].astype(F32))
        ms = jnp.mean(y * y, axis=-1, keepdims=True)
        a_s[...] = (y * lax.rsqrt(ms + EPS) * snw_ref[...]).astype(BF16)
        gate = _silu(g_ref[...].astype(F32)) * rnw_ref[...]
        for hd in range(RET_HEADS):
            hs = slice(hd * RET_DK, (hd + 1) * RET_DK)
            r = yr_ref[:, hs]
            ms = jnp.mean(r * r, axis=-1, keepdims=True)
            b_s[:, hs] = (r * lax.rsqrt(ms + EPS) * gate[:, hs]).astype(BF16)

    p1 = jnp.dot(a_s[...], ws_ref[...], preferred_element_type=F32)
    p2 = jnp.dot(b_s[...], wr_ref[...], preferred_element_type=F32)
    merged = (jax.nn.sigmoid(gs_ref[...].astype(F32)) * p1
              + jax.nn.sigmoid(gr_ref[...].astype(F32)) * p2)
    o_ref[...] = merged.astype(BF16)


def _merge(y_ssd, xbc_act, p_main, y_ret, d_skip, ssm_norm_w, ret_norm_w, w_ssd, w_ret, l):
    t, d = y_ssd.shape
    bm, bn = MERGE_ROW_BLOCK, MERGE_COL_BLOCK
    nb = d // bn
    return pl.pallas_call(
        _merge_kernel,
        grid=(t // bm, nb),
        in_specs=[pl.BlockSpec((bm, d), lambda i, j: (i, 0)),
                  pl.BlockSpec((bm, d), lambda i, j: (i, 0)),
                  pl.BlockSpec((bm, d), lambda i, j: (i, 0)),
                  pl.BlockSpec((bm, d), lambda i, j: (i, 0)),
                  pl.BlockSpec((bm, d), lambda i, j: (i, 4)),
                  pl.BlockSpec((bm, bn), lambda i, j: (i, 5 * nb + j)),
                  pl.BlockSpec((bm, bn), lambda i, j: (i, 6 * nb + j)),
                  pl.BlockSpec((None, 1, d), lambda i, j: (l, 0, 0)),
                  pl.BlockSpec((None, 1, d), lambda i, j: (l, 0, 0)),
                  pl.BlockSpec((None, 1, d), lambda i, j: (l, 0, 0)),
                  pl.BlockSpec((None, d, bn), lambda i, j: (l, 0, j)),
                  pl.BlockSpec((None, d, bn), lambda i, j: (l, 0, j))],
        out_specs=pl.BlockSpec((bm, bn), lambda i, j: (i, j)),
        out_shape=jax.ShapeDtypeStruct((t, d), BF16),
        scratch_shapes=[pltpu.VMEM((bm, d), BF16), pltpu.VMEM((bm, d), BF16)],
        compiler_params=_params(("arbitrary", "arbitrary")),
        name="merge",
    )(y_ssd, xbc_act, p_main, y_ret, p_main, p_main, p_main, d_skip, ssm_norm_w, ret_norm_w,
      w_ssd, w_ret)


def _outproj_kernel(m_ref, w_ref, h_ref, g_ref, o_ref, *, n_ctx, bm):
    i = pl.program_id(0)
    row = i * bm + lax.broadcasted_iota(jnp.int32, (bm, 1), 0)
    gate = jnp.where(row < n_ctx, g_ref[1:2, :], g_ref[0:1, :])
    o_ref[...] = h_ref[...] + gate * jnp.dot(m_ref[...], w_ref[...], preferred_element_type=F32)


def _outproj(merged, w_out, h, mod, l, n_ctx):
    t, d = h.shape
    bm, bn = ROW_BLOCK, COL_BLOCK
    nb = d // bn
    return pl.pallas_call(
        functools.partial(_outproj_kernel, n_ctx=n_ctx, bm=bm),
        grid=(t // bm, nb),
        in_specs=[pl.BlockSpec((bm, d), lambda i, j: (i, 0)),
                  pl.BlockSpec((None, d, bn), lambda i, j: (l, 0, j)),
                  pl.BlockSpec((bm, bn), lambda i, j: (i, j)),
                  pl.BlockSpec((None, 8, bn), lambda i, j: (l, 0, 2 * nb + j))],
        out_specs=pl.BlockSpec((bm, bn), lambda i, j: (i, j)),
        out_shape=jax.ShapeDtypeStruct((t, d), F32),
        compiler_params=_params(("arbitrary", "arbitrary")),
        name="outproj",
    )(merged, w_out, h, mod)


def _router_kernel(h_ref, sh_ref, sc_ref, nw_ref, wr_ref, br_ref, xn_ref, meta_ref, cnt_ref,
                   carry_s, *, n_ctx, bm):
    i = pl.program_id(0)

    @pl.when(i == 0)
    def _():
        carry_s[...] = jnp.zeros_like(carry_s)

    xn = _modulated_norm(h_ref[...], nw_ref[...], sh_ref, sc_ref, i * bm, n_ctx)
    xn_ref[...] = xn
    logits = jnp.dot(xn, wr_ref[...], precision=HIGHEST, preferred_element_type=F32) + br_ref[...]
    lane = lax.broadcasted_iota(jnp.int32, (bm, LANES), 1)
    far = jnp.int32(4 * LANES)

    glog = jnp.where(lane < N_GROUPS_E, logits, NEG_BIG)
    gmax = jnp.max(glog, axis=1, keepdims=True)
    gidx = jnp.min(jnp.where(glog == gmax, lane, far), axis=1, keepdims=True)
    p_top = 1.0 / jnp.sum(jnp.exp(glog - gmax), axis=1, keepdims=True)

    in_group = ((lane >= N_GROUPS_E) & (lane < N_GROUPS_E + N_EXPERTS)
                & (((lane - N_GROUPS_E) >> 3) == gidx))
    elog = jnp.where(in_group, logits, NEG_BIG)
    m1 = jnp.max(elog, axis=1, keepdims=True)
    i1 = jnp.min(jnp.where(elog == m1, lane, far), axis=1, keepdims=True)
    elog2 = jnp.where(lane == i1, NEG_BIG, elog)
    m2 = jnp.max(elog2, axis=1, keepdims=True)
    i2 = jnp.min(jnp.where(elog2 == m2, lane, far), axis=1, keepdims=True)
    e2 = jnp.exp(m2 - m1)
    w1 = p_top / (1.0 + e2)
    w2 = p_top * e2 / (1.0 + e2)
    ex1 = i1 - N_GROUPS_E
    ex2 = i2 - N_GROUPS_E

    onehot = jnp.where((lane == ex1) | (lane == ex2), 1.0, 0.0).astype(F32)
    ri = lax.broadcasted_iota(jnp.int32, (bm, bm), 0)
    ci = lax.broadcasted_iota(jnp.int32, (bm, bm), 1)
    earlier = jnp.where(ci < ri, 1.0, 0.0).astype(BF16)
    before = jnp.dot(earlier, onehot.astype(BF16), preferred_element_type=F32) + carry_s[0:1, :]
    r1 = jnp.sum(jnp.where(lane == ex1, before, 0.0), axis=1, keepdims=True)
    r2 = jnp.sum(jnp.where(lane == ex2, before, 0.0), axis=1, keepdims=True)
    carry = carry_s[...] + jnp.sum(onehot, axis=0, keepdims=True)
    carry_s[...] = carry
    cnt_ref[...] = carry

    meta = jnp.where(lane == 0, ex1.astype(F32), 0.0)
    meta = jnp.where(lane == 1, ex2.astype(F32), meta)
    meta = jnp.where(lane == 2, w1, meta)
    meta = jnp.where(lane == 3, w2, meta)
    meta = jnp.where(lane == 4, r1, meta)
    meta = jnp.where(lane == 5, r2, meta)
    meta_ref[...] = meta


def _router(h, mod, norm_w, w_route, b_route, l, n_ctx):
    t, d = h.shape
    bm = TOK_BLOCK
    return pl.pallas_call(
        functools.partial(_router_kernel, n_ctx=n_ctx, bm=bm),
        grid=(t // bm,),
        in_specs=[pl.BlockSpec((bm, d), lambda i: (i, 0)),
                  pl.BlockSpec((None, 8, d), lambda i: (l, 0, 3)),
                  pl.BlockSpec((None, 8, d), lambda i: (l, 0, 4)),
                  pl.BlockSpec((None, 1, d), lambda i: (l, 0, 0)),
                  pl.BlockSpec((None, d, LANES), lambda i: (l, 0, 0)),
                  pl.BlockSpec((None, 1, LANES), lambda i: (l, 0, 0))],
        out_specs=[pl.BlockSpec((bm, d), lambda i: (i, 0)),
                   pl.BlockSpec((bm, LANES), lambda i: (i, 0)),
                   pl.BlockSpec((8, LANES), lambda i: (0, 0))],
        out_shape=[jax.ShapeDtypeStruct((t, d), F32),
                   jax.ShapeDtypeStruct((t, LANES), F32),
                   jax.ShapeDtypeStruct((8, LANES), F32)],
        scratch_shapes=[pltpu.VMEM((8, LANES), F32)],
        compiler_params=_params(("arbitrary",)),
        name="router",
    )(h, mod, mod, norm_w, w_route, b_route)


def _row_copy(src, src_row, dst, dst_row, sem):
    return pltpu.make_async_copy(src.at[pl.ds(src_row, 1), :], dst.at[pl.ds(dst_row, 1), :], sem)


def _dispatch_kernel(pos_ref, xn_ref, init_ref, o_ref, sem, *, bm):
    del init_ref
    base = pl.program_id(0) * bm

    def issue(r, carry):
        for k in range(2):
            _row_copy(xn_ref, r, o_ref, pos_ref[(base + r) * 2 + k], sem).start()
        return carry

    def drain(r, carry):
        for k in range(2):
            _row_copy(xn_ref, 0, o_ref, 0, sem).wait()
        return carry

    lax.fori_loop(0, bm, issue, 0)
    lax.fori_loop(0, bm, drain, 0)


def _dispatch(pos, xn, n_rows):
    t, d = xn.shape
    bm = TOK_BLOCK
    init = jnp.zeros((n_rows, d), F32)
    return pl.pallas_call(
        functools.partial(_dispatch_kernel, bm=bm),
        grid_spec=pltpu.PrefetchScalarGridSpec(
            num_scalar_prefetch=1,
            grid=(t // bm,),
            in_specs=[pl.BlockSpec((bm, d), lambda i, pos: (i, 0)),
                      pl.BlockSpec(memory_space=pl.ANY)],
            out_specs=pl.BlockSpec(memory_space=pl.ANY),
            scratch_shapes=[pltpu.SemaphoreType.DMA(())]),
        out_shape=jax.ShapeDtypeStruct((n_rows, d), F32),
        input_output_aliases={2: 0},
        compiler_params=_params(("arbitrary",)),
        name="dispatch",
    )(pos, xn, init)


def _expert_kernel(te_ref, xi_ref, oi_ref, first_ref, valid_ref, x_ref, wg_ref, wu_ref, wd_ref,
                   o_ref, wg_s, wu_s, wd_s):
    i = pl.program_id(0)

    @pl.when(first_ref[i] == 1)
    def _():
        wg_s[...] = wg_ref[...].astype(BF16)
        wu_s[...] = wu_ref[...].astype(BF16)
        wd_s[...] = wd_ref[...].astype(BF16)

    @pl.when(valid_ref[i] == 1)
    def _():
        x = x_ref[...].astype(BF16)
        gate = jnp.dot(x, wg_s[...], preferred_element_type=F32)
        up = jnp.dot(x, wu_s[...], preferred_element_type=F32)
        mid = (_silu(gate) * up).astype(BF16)
        o_ref[...] = jnp.dot(mid, wd_s[...], preferred_element_type=F32)

    @pl.when(valid_ref[i] == 0)
    def _():
        o_ref[...] = jnp.zeros_like(o_ref)


def _experts(tile_meta, x_sorted, w_gate, w_up, w_down, l, n_tiles):
    d = x_sorted.shape[1]
    tm = EXPERT_TILE
    return pl.pallas_call(
        _expert_kernel,
        grid_spec=pltpu.PrefetchScalarGridSpec(
            num_scalar_prefetch=5,
            grid=(n_tiles,),
            in_specs=[pl.BlockSpec((tm, d), lambda i, te, xi, oi, fi, va: (xi[i], 0)),
                      pl.BlockSpec((None, None, d, D_EXPERT),
                                   lambda i, te, xi, oi, fi, va: (l, te[i], 0, 0)),
                      pl.BlockSpec((None, None, d, D_EXPERT),
                                   lambda i, te, xi, oi, fi, va: (l, te[i], 0, 0)),
                      pl.BlockSpec((None, None, D_EXPERT, d),
                                   lambda i, te, xi, oi, fi, va: (l, te[i], 0, 0))],
            out_specs=pl.BlockSpec((tm, d), lambda i, te, xi, oi, fi, va: (oi[i], 0)),
            scratch_shapes=[pltpu.VMEM((d, D_EXPERT), BF16), pltpu.VMEM((d, D_EXPERT), BF16),
                            pltpu.VMEM((D_EXPERT, d), BF16)]),
        out_shape=jax.ShapeDtypeStruct((n_tiles * tm, d), F32),
        compiler_params=_params(("arbitrary",)),
        name="experts",
    )(*tile_meta, x_sorted, w_gate, w_up, w_down)


def _combine_kernel(pos_ref, h_ref, meta_ref, g_ref, ys_ref, o_ref, buf_s, sem, *, n_ctx, bm):
    i = pl.program_id(0)
    base = i * bm

    def issue(r, carry):
        for k in range(2):
            _row_copy(ys_ref, pos_ref[(base + r) * 2 + k], buf_s.at[k], r, sem).start()
        return carry

    def drain(r, carry):
        for k in range(2):
            _row_copy(ys_ref, 0, buf_s.at[k], 0, sem).wait()
        return carry

    lax.fori_loop(0, bm, issue, 0)
    lax.fori_loop(0, bm, drain, 0)
    meta = meta_ref[...]
    moe = meta[:, 2:3] * buf_s[0] + meta[:, 3:4] * buf_s[1]
    row = base + lax.broadcasted_iota(jnp.int32, (bm, 1), 0)
    gate = jnp.where(row < n_ctx, g_ref[1:2, :], g_ref[0:1, :])
    o_ref[...] = h_ref[...] + gate * moe


def _combine(pos, h, meta, mod, y_sorted, l, n_ctx):
    t, d = h.shape
    bm = TOK_BLOCK
    return pl.pallas_call(
        functools.partial(_combine_kernel, n_ctx=n_ctx, bm=bm),
        grid_spec=pltpu.PrefetchScalarGridSpec(
            num_scalar_prefetch=1,
            grid=(t // bm,),
            in_specs=[pl.BlockSpec((bm, d), lambda i, pos: (i, 0)),
                      pl.BlockSpec((bm, LANES), lambda i, pos: (i, 0)),
                      pl.BlockSpec((None, 8, d), lambda i, pos: (l, 0, 5)),
                      pl.BlockSpec(memory_space=pl.ANY)],
            out_specs=pl.BlockSpec((bm, d), lambda i, pos: (i, 0)),
            scratch_shapes=[pltpu.VMEM((2, bm, d), F32), pltpu.SemaphoreType.DMA(())]),
        out_shape=jax.ShapeDtypeStruct((t, d), F32),
        compiler_params=_params(("arbitrary",)),
        name="combine",
    )(pos, h, meta, mod, y_sorted)


def _tile_plan(meta, counts, n_tiles):
    tm = EXPERT_TILE
    cnt = counts[0, :N_EXPERTS].astype(jnp.int32)
    tiles_e = (cnt + tm - 1) // tm
    tile_end = jnp.cumsum(tiles_e)
    offset = (tile_end - tiles_e) * tm
    expert = meta[:, 0:2].astype(jnp.int32)
    rank = meta[:, 4:6].astype(jnp.int32)
    pos = (offset[expert] + rank).reshape(-1)
    n_used = tile_end[-1]
    tid = jnp.arange(n_tiles, dtype=jnp.int32)
    valid = tid < n_used
    tsrc = jnp.minimum(tid, n_used - 1)
    te = jnp.minimum(jnp.searchsorted(tile_end, tsrc, side="right"), N_EXPERTS - 1).astype(jnp.int32)
    first = (valid & ((tid == 0) | (te != jnp.roll(te, 1)))).astype(jnp.int32)
    oi = tid
    return pos, (te, tsrc.astype(jnp.int32), oi, first, valid.astype(jnp.int32))


def _final_kernel(h_ref, w_ref, o_ref):
    x = h_ref[...]
    ms = jnp.mean(x * x, axis=-1, keepdims=True)
    o_ref[...] = x * lax.rsqrt(ms + EPS) * w_ref[...]


def _final_norm(h, w, n_ctx):
    t, d = h.shape
    bm = TOK_BLOCK
    skip = n_ctx // bm
    return pl.pallas_call(
        _final_kernel,
        grid=((t - n_ctx) // bm,),
        in_specs=[pl.BlockSpec((bm, d), lambda i: (i + skip, 0)),
                  pl.BlockSpec((1, d), lambda i: (0, 0))],
        out_specs=pl.BlockSpec((bm, d), lambda i: (i, 0)),
        out_shape=jax.ShapeDtypeStruct((t - n_ctx, d), F32),
        compiler_params=_params(("arbitrary",)),
        name="final_norm",
    )(h, w)


def _pad_lanes(v):
    return jnp.pad(v, [(0, 0)] * (v.ndim - 1) + [(0, LANES - v.shape[-1])])


def _rope_tables(n_ctx, n_lat):
    rows = n_lat // GRID_W
    row = jnp.broadcast_to(jnp.arange(rows, dtype=F32)[:, None], (rows, GRID_W)).reshape(n_lat)
    col = jnp.broadcast_to(jnp.arange(GRID_W, dtype=F32)[None, :], (rows, GRID_W)).reshape(n_lat)
    inv_freq = ROPE_BASE ** (-jnp.arange(ROPE_FREQS, dtype=F32) / ROPE_FREQS)
    ang_r = row[:, None] * inv_freq[None, :]
    ang_c = col[:, None] * inv_freq[None, :]
    cos_l = jnp.concatenate([jnp.cos(ang_r)] * 2 + [jnp.cos(ang_c)] * 2, axis=1)
    sin_l = jnp.concatenate([-jnp.sin(ang_r), jnp.sin(ang_r), -jnp.sin(ang_c), jnp.sin(ang_c)], axis=1)
    cos_t = jnp.concatenate([jnp.ones((n_ctx, 2 * LANES), F32), cos_l], axis=0)
    sin_t = jnp.concatenate([jnp.zeros((n_ctx, 2 * LANES), F32), sin_l], axis=0)
    return cos_t, sin_t


def kernel(x, c, ctx, c_ctx, w_mod, b_mod, norm1_w, w_in, conv_w, conv_b, ssm_A_log, ssm_dt_bias,
           ssm_D, ssm_norm_w, ret_decay_logit, ret_norm_w, w_ssd_proj, w_ret_proj, w_out, norm2_w,
           w_route_group, b_route_group, w_route_expert, b_route_expert, w_exp_gate, w_exp_up,
           w_exp_down, final_norm_w):
    batch, n_lat, d = x.shape
    n_ctx = ctx.shape[1]
    depth = w_mod.shape[0]
    assert batch == 1 and d == D_MODEL
    t = n_ctx + n_lat
    assert t % ROW_BLOCK == 0 and n_ctx % TOK_BLOCK == 0 and n_lat % TOK_BLOCK == 0

    z_w, xbc_w, dt_w, q_w, k_w, v_w, g_w, gates_w = jnp.split(
        w_in, [2048, 5120, 5152, 7200, 9248, 11296, 13344], axis=-1)
    w_main = jnp.concatenate([z_w, q_w, k_w, v_w, g_w, gates_w, xbc_w], axis=-1).astype(BF16)
    w_dt = _pad_lanes(dt_w)
    w_ssd_b = w_ssd_proj.astype(BF16)
    w_ret_b = w_ret_proj.astype(BF16)
    w_out_b = w_out.astype(BF16)
    w_route = _pad_lanes(jnp.concatenate([w_route_group, w_route_expert], axis=-1))
    b_route = _pad_lanes(jnp.concatenate([b_route_group, b_route_expert], axis=-1))[:, None, :]
    d_skip = jnp.repeat(ssm_D, SSM_HEADDIM, axis=-1)[:, None, :]
    dt_bias = _pad_lanes(ssm_dt_bias)[:, :, None, :]
    a_log = _pad_lanes(ssm_A_log)[:, :, None, :]
    ret_dl = _pad_lanes(ret_decay_logit)[:, :, None, :]
    head_of_col = jnp.arange(D_MODEL, dtype=jnp.int32) // SSM_HEADDIM
    e_mat = (jnp.arange(LANES, dtype=jnp.int32)[:, None] == head_of_col[None, :]).astype(BF16)
    cos_t, sin_t = _rope_tables(n_ctx, n_lat)

    cvec = jnp.concatenate([c, c_ctx[None, :], jnp.zeros((6, d), F32)], axis=0)
    mod = _modulation(cvec, w_mod, b_mod)

    n_tiles = (2 * t) // EXPERT_TILE + N_EXPERTS
    h = jnp.concatenate([ctx[0], x[0]], axis=0)
    for l in range(depth):
        p_main, dt_raw = _inproj(h, mod, norm1_w[:, None, :], w_main, w_dt, cos_t, sin_t, l, n_ctx)
        xbc_act = _conv(p_main, conv_w, conv_b[:, None, :], l, n_ctx)
        y_ssd = _ssd_scan(xbc_act, dt_raw, dt_bias[l, 0], a_log[l, 0], e_mat, None, n_ctx, False)
        y_ssd = _ssd_scan(xbc_act, dt_raw, dt_bias[l, 1], a_log[l, 1], e_mat, y_ssd, n_ctx, True)
        y_ret = _ret_scan(p_main, ret_dl[l, 0], None, n_ctx, False)
        y_ret = _ret_scan(p_main, ret_dl[l, 1], y_ret, n_ctx, True)
        merged = _merge(y_ssd, xbc_act, p_main, y_ret, d_skip, ssm_norm_w[:, None, :],
                        ret_norm_w[:, None, :], w_ssd_b, w_ret_b, l)
        h = _outproj(merged, w_out_b, h, mod, l, n_ctx)
        xn, meta, counts = _router(h, mod, norm2_w[:, None, :], w_route, b_route, l, n_ctx)
        pos, tile_meta = _tile_plan(meta, counts, n_tiles)
        x_sorted = _dispatch(pos, xn, n_tiles * EXPERT_TILE)
        y_sorted = _experts(tile_meta, x_sorted, w_exp_gate, w_exp_up, w_exp_down, l, n_tiles)
        h = _combine(pos, h, meta, mod, y_sorted, l, n_ctx)

    return _final_norm(h, final_norm_w[None, :], n_ctx)[None]
```

```python
import functools

import jax
import jax.numpy as jnp
from jax import lax
from jax.experimental import pallas as pl
from jax.experimental.pallas import tpu as pltpu

F32 = jnp.float32
BF16 = jnp.bfloat16
HIGHEST = lax.Precision.HIGHEST

D_MODEL = 2048
GRID_W = 64
CHUNK = 128
EPS = 1e-6

SSM_HEADDIM = 64
SSM_HEADS = 32
SSM_GROUPS = 4
SSM_HPG = 8
SSM_STATE = 128
SSM_CONV = 5
GROUP_W = SSM_HPG * SSM_HEADDIM
BC_W = SSM_GROUPS * SSM_STATE
XBC_D = D_MODEL + 2 * BC_W

RET_HEADS = 8
RET_DK = 256
ROPE_FREQS = 64
ROPE_BASE = 10000.0

N_GROUPS_E = 4
EXPERTS_PER_GROUP = 8
N_EXPERTS = 32
D_EXPERT = 512

LANES = 128
MAIN_W = 7 * D_MODEL + XBC_D
XBC_COL0 = 7 * D_MODEL
DT_W = SSM_HEADS
DT_COL0 = D_MODEL + XBC_D

INPROJ_ROW_BLOCKS = (1056, 768)
COL_BLOCK = 1024
OUT_ROW_BLOCK = 384
MERGE_ROW_BLOCK = 256
TOK_BLOCK = 256
EXPERT_TILE = 256
NEG_BIG = -1e30
VMEM_LIMIT = 56 * 1024 * 1024


def _params(sem, vmem=VMEM_LIMIT):
    return pltpu.CompilerParams(dimension_semantics=sem, vmem_limit_bytes=vmem)


def _silu(v):
    return v * jax.nn.sigmoid(v)


def _softplus(v):
    return jnp.maximum(v, 0.0) + jnp.log1p(jnp.exp(-jnp.abs(v)))


def _split_bf16(v):
    hi = v.astype(BF16)
    return hi, (v - hi.astype(F32)).astype(BF16)


def _dot3(x, w):
    xh, xl = _split_bf16(x)
    wh, wl = _split_bf16(w)
    return (jnp.dot(xh, wh, preferred_element_type=F32) + jnp.dot(xl, wh, preferred_element_type=F32)
            + jnp.dot(xh, wl, preferred_element_type=F32))


def _modulated_norm(x, nw, sh_ref, sc_ref, row0, n_ctx):
    ms = jnp.mean(x * x, axis=-1, keepdims=True)
    y = x * lax.rsqrt(ms + EPS) * nw
    row = row0 + lax.broadcasted_iota(jnp.int32, (x.shape[0], 1), 0)
    is_ctx = row < n_ctx
    sc = jnp.where(is_ctx, sc_ref[1:2, :], sc_ref[0:1, :])
    sh = jnp.where(is_ctx, sh_ref[1:2, :], sh_ref[0:1, :])
    return y * (1.0 + sc) + sh


def _mod_kernel(a_ref, w_ref, b_ref, o_ref):
    a = a_ref[...]
    o_ref[...] = jnp.dot(_silu(a), w_ref[...], precision=HIGHEST,
                         preferred_element_type=F32) + b_ref[...]


def _modulation(cvec, w_mod, b_mod):
    depth, d, n = w_mod.shape
    bn = 1024
    return pl.pallas_call(
        _mod_kernel,
        grid=(depth, n // bn),
        in_specs=[pl.BlockSpec((8, d), lambda l, j: (0, 0)),
                  pl.BlockSpec((None, d, bn), lambda l, j: (l, 0, j)),
                  pl.BlockSpec((None, 1, bn), lambda l, j: (l, 0, j))],
        out_specs=pl.BlockSpec((None, 8, bn), lambda l, j: (l, 0, j)),
        out_shape=jax.ShapeDtypeStruct((depth, 8, n), F32),
        compiler_params=_params(("arbitrary", "arbitrary")),
        name="modulation",
    )(cvec, w_mod, b_mod.reshape(depth, 1, n))


def _dot3_nt(x, wt):
    nt = (((1,), (1,)), ((), ()))
    xh, xl = _split_bf16(x)
    wh, wl = _split_bf16(wt)
    return (lax.dot_general(xh, wh, nt, preferred_element_type=F32)
            + lax.dot_general(xl, wh, nt, preferred_element_type=F32)
            + lax.dot_general(xh, wl, nt, preferred_element_type=F32))


def _wprep_kernel(w_hbm, o_ref, dt_ref, buf_s, dtb_s, sem, dsem, *, bn, n_aligned, nj):
    l = pl.program_id(0)
    n = pl.program_id(1)
    step = l * nj + n
    total = pl.num_programs(0) * nj

    def fetch(ll, nn, slot):
        row0 = pl.multiple_of(nn * bn + jnp.where(nn >= n_aligned, DT_W, 0), 8)
        return pltpu.make_async_copy(w_hbm.at[ll, pl.ds(row0, bn), :], buf_s.at[slot], sem.at[slot])

    @pl.when(step == 0)
    def _():
        fetch(l, n, 0).start()

    @pl.when(step + 1 < total)
    def _():
        wrap = n + 1 == nj
        fetch(jnp.where(wrap, l + 1, l), jnp.where(wrap, 0, n + 1), (step + 1) % 2).start()

    fetch(l, n, step % 2).wait()
    o_ref[...] = buf_s[step % 2].astype(BF16)

    @pl.when(n == 0)
    def _():
        cp = pltpu.make_async_copy(w_hbm.at[l, pl.ds(DT_COL0, DT_W), :], dtb_s, dsem)
        cp.start()
        cp.wait()
        dt_ref[0:DT_W, :] = dtb_s[...]
        dt_ref[DT_W:, :] = jnp.zeros((LANES - DT_W, dt_ref.shape[1]), F32)


def _wprep(w_in_t):
    depth, _, d = w_in_t.shape
    bn = COL_BLOCK
    nj = MAIN_W // bn
    return pl.pallas_call(
        functools.partial(_wprep_kernel, bn=bn, n_aligned=DT_COL0 // bn, nj=nj),
        grid=(depth, nj),
        in_specs=[pl.BlockSpec(memory_space=pl.ANY)],
        out_specs=[pl.BlockSpec((None, bn, d), lambda l, n: (l, n, 0)),
                   pl.BlockSpec((None, LANES, d), lambda l, n: (l, 0, 0))],
        out_shape=[jax.ShapeDtypeStruct((depth, MAIN_W, d), BF16),
                   jax.ShapeDtypeStruct((depth, LANES, d), F32)],
        scratch_shapes=[pltpu.VMEM((2, bn, d), F32), pltpu.VMEM((DT_W, d), F32),
                        pltpu.SemaphoreType.DMA((2,)), pltpu.SemaphoreType.DMA(())],
        compiler_params=_params(("arbitrary", "arbitrary")),
        name="wprep",
    )(w_in_t)


def _inproj_kernel(h_ref, sh_ref, sc_ref, nw_ref, w_ref, wdt_ref, cos_ref, sin_ref,
                   o_ref, dt_ref, xn_s, *, n_ctx, bm, bn):
    i = pl.program_id(0)
    j = pl.program_id(1)

    @pl.when(j == 0)
    def _():
        xn = _modulated_norm(h_ref[...], nw_ref[...], sh_ref, sc_ref, i * bm, n_ctx)
        xn_s[...] = xn.astype(BF16)
        dt_ref[...] = _dot3_nt(xn, wdt_ref[...])

    acc = lax.dot_general(xn_s[...], w_ref[...], (((1,), (1,)), ((), ())),
                          preferred_element_type=F32)
    q_lo = DT_COL0 // bn
    k_lo = (DT_COL0 + D_MODEL) // bn
    k_hi = (DT_COL0 + 2 * D_MODEL) // bn
    is_rope = (j >= q_lo) & (j < k_hi)

    @pl.when(is_rope)
    def _():
        scale = jnp.where(j >= k_lo, RET_DK ** -0.5, 1.0).astype(F32)
        for gi in range(bn // LANES):
            u = acc[:, gi * LANES:(gi + 1) * LANES]
            half = gi % 2
            c = cos_ref[:, half * LANES:(half + 1) * LANES]
            s = sin_ref[:, half * LANES:(half + 1) * LANES]
            r = (u * c + pltpu.roll(u, LANES // 2, 1) * s) * scale
            o_ref[:, gi * LANES:(gi + 1) * LANES] = r.astype(BF16)

    @pl.when(jnp.logical_not(is_rope))
    def _():
        o_ref[...] = acc.astype(BF16)


def _inproj(h, mod, norm_w, w_nat, w_dt, cos_t, sin_t, l, n_ctx):
    t, d = h.shape
    bm = next(b for b in INPROJ_ROW_BLOCKS if t % b == 0)
    bn = COL_BLOCK
    nj = MAIN_W // bn
    z_blocks = D_MODEL // bn
    xbc_blocks = XBC_D // bn

    def out_block(j):
        return jnp.where(j < z_blocks, j,
                         jnp.where(j < z_blocks + xbc_blocks, j + (nj - z_blocks - xbc_blocks),
                                   j - xbc_blocks))

    once = pl.Buffered(1)
    kern = functools.partial(_inproj_kernel, n_ctx=n_ctx, bm=bm, bn=bn)
    return pl.pallas_call(
        kern,
        grid=(t // bm, nj),
        in_specs=[pl.BlockSpec((bm, d), lambda i, j: (i, 0), pipeline_mode=once),
                  pl.BlockSpec((None, 8, d), lambda i, j: (l, 0, 0)),
                  pl.BlockSpec((None, 8, d), lambda i, j: (l, 0, 1)),
                  pl.BlockSpec((None, 1, d), lambda i, j: (l, 0, 0)),
                  pl.BlockSpec((None, bn, d), lambda i, j: (l, j, 0)),
                  pl.BlockSpec((None, LANES, d), lambda i, j: (l, 0, 0)),
                  pl.BlockSpec((bm, 2 * LANES), lambda i, j: (i, 0), pipeline_mode=once),
                  pl.BlockSpec((bm, 2 * LANES), lambda i, j: (i, 0), pipeline_mode=once)],
        out_specs=[pl.BlockSpec((bm, bn), lambda i, j: (i, out_block(j))),
                   pl.BlockSpec((bm, LANES), lambda i, j: (i, 0))],
        out_shape=[jax.ShapeDtypeStruct((t, MAIN_W), BF16),
                   jax.ShapeDtypeStruct((t, LANES), F32)],
        scratch_shapes=[pltpu.VMEM((bm, d), BF16)],
        compiler_params=_params(("arbitrary", "arbitrary")),
        name="inproj",
    )(h, mod, mod, norm_w, w_nat, w_dt, cos_t, sin_t)


def _conv_kernel(x_ref, w_ref, b_ref, o_ref, pad_s, *, n_ctx, n_lat, rows):
    halo = 8
    zeros = jnp.zeros((halo, LANES), F32)
    c0 = halo
    x0 = 2 * halo + n_ctx
    pad_s[0:halo, :] = zeros
    pad_s[c0:c0 + n_ctx, :] = x_ref[0:n_ctx, :].astype(F32)
    pad_s[c0 + n_ctx:x0, :] = zeros
    pad_s[x0:x0 + n_lat, :] = x_ref[n_ctx:n_ctx + n_lat, :].astype(F32)
    pad_s[x0 + n_lat:x0 + n_lat + halo, :] = zeros
    w = w_ref[...]
    b = b_ref[...]

    def segment(src0, dst0, n):
        def body(s, carry):
            r0 = pl.multiple_of(s * rows, rows)
            acc = jnp.broadcast_to(b, (rows, LANES))
            for kk in range(SSM_CONV):
                acc = acc + w[kk:kk + 1, :] * pad_s[pl.ds(src0 + r0 + kk - SSM_CONV // 2, rows), :]
            o_ref[pl.ds(dst0 + r0, rows), :] = _silu(acc).astype(BF16)
            return carry
        lax.fori_loop(0, n // rows, body, 0)

    segment(c0, 0, n_ctx)
    segment(x0, n_ctx, n_lat)


def _conv(p_main, conv_w, conv_b, l, n_ctx):
    t = p_main.shape[0]
    n_lat = t - n_ctx
    col0 = XBC_COL0 // LANES
    kern = functools.partial(_conv_kernel, n_ctx=n_ctx, n_lat=n_lat, rows=256)
    return pl.pallas_call(
        kern,
        grid=(XBC_D // LANES,),
        in_specs=[pl.BlockSpec((t, LANES), lambda c: (0, col0 + c)),
                  pl.BlockSpec((None, SSM_CONV, LANES), lambda c: (l, 0, c)),
                  pl.BlockSpec((None, 1, LANES), lambda c: (l, 0, c))],
        out_specs=pl.BlockSpec((t, LANES), lambda c: (0, c)),
        out_shape=jax.ShapeDtypeStruct((t, XBC_D), BF16),
        scratch_shapes=[pltpu.VMEM((t + 24, LANES), F32)],
        compiler_params=_params(("arbitrary",)),
        name="conv_silu",
    )(p_main, conv_w, conv_b)


def _chunk_index(s, nc_ctx, nc, rev):
    if not rev:
        return s
    return jnp.where(s < nc_ctx, nc_ctx - 1 - s, nc - 1 - (s - nc_ctx))


def _split_dot(v, e):
    hi, lo = _split_bf16(v)
    return (jnp.dot(hi, e, preferred_element_type=F32) + jnp.dot(lo, e, preferred_element_type=F32))


def _ssd_kernel(*refs, rev, has_prev):
    if has_prev:
        xs_ref, b_ref, c_ref, dt_ref, bias_ref, alog_ref, e_ref, prev_ref, o_ref, h_s = refs
    else:
        xs_ref, b_ref, c_ref, dt_ref, bias_ref, alog_ref, e_ref, o_ref, h_s = refs
        prev_ref = None
    q = CHUNK

    @pl.when(pl.program_id(0) == 0)
    def _():
        h_s[...] = jnp.zeros_like(h_s)

    ii = lax.broadcasted_iota(jnp.int32, (q, q), 0)
    jj = lax.broadcasted_iota(jnp.int32, (q, q), 1)
    seen = (jj >= ii) if rev else (jj <= ii)
    tri = jnp.where(seen, 1.0, 0.0).astype(F32)

    dt = _softplus(dt_ref[...] + bias_ref[...])
    a = dt * (-jnp.exp(alog_ref[...]))
    cum = jnp.dot(tri, a, precision=HIGHEST, preferred_element_type=F32)
    cum_t = cum.T
    dt_t = dt.T
    total = cum[0:1, :] if rev else cum[q - 1:q, :]
    dfs = jnp.exp(cum)
    w_end = dt * jnp.exp(total - cum)
    cdec = jnp.broadcast_to(jnp.exp(total), (8, LANES))
    e = e_ref[...]
    expanded = _split_dot(jnp.concatenate([w_end, dfs, cdec], axis=0), e)
    w_end_x = expanded[0:q]
    dfs_x = expanded[q:2 * q]
    cdec_x = expanded[2 * q:2 * q + 1]

    xs = xs_ref[...]
    xw = (xs.astype(F32) * w_end_x).astype(BF16)

    for g in range(SSM_GROUPS):
        cg = c_ref[:, g * SSM_STATE:(g + 1) * SSM_STATE]
        bg = b_ref[:, g * SSM_STATE:(g + 1) * SSM_STATE]
        gs = slice(g * GROUP_W, (g + 1) * GROUP_W)
        cb = lax.dot_general(cg, bg, (((1,), (1,)), ((), ())), preferred_element_type=F32)
        h_prev = h_s[:, gs]
        y_off = jnp.dot(cg, h_prev.astype(BF16), preferred_element_type=F32) * dfs_x[:, gs]
        new = lax.dot_general(bg, xw[:, gs], (((0,), (0,)), ((), ())), preferred_element_type=F32)
        pieces = []
        for hh in range(SSM_HPG):
            hd = g * SSM_HPG + hh
            seg = cum[:, hd:hd + 1] - cum_t[hd:hd + 1, :]
            m = jnp.exp(jnp.where(seen, seg, NEG_BIG)) * cb * dt_t[hd:hd + 1, :]
            pieces.append(jnp.dot(m.astype(BF16), xs[:, hd * SSM_HEADDIM:(hd + 1) * SSM_HEADDIM],
                                  preferred_element_type=F32))
        y = jnp.concatenate(pieces, axis=1) + y_off
        if prev_ref is not None:
            y = y + prev_ref[:, gs]
        o_ref[:, gs] = y.astype(o_ref.dtype)
        h_s[:, gs] = h_prev * cdec_x[:, gs] + new


def _ssd_scan(xbc_act, dt_raw, dt_bias, a_log, e_mat, prev, n_ctx, rev):
    t = xbc_act.shape[0]
    nc, nc_ctx = t // CHUNK, n_ctx // CHUNK
    cidx = functools.partial(_chunk_index, nc_ctx=nc_ctx, nc=nc, rev=rev)
    in_specs = [pl.BlockSpec((CHUNK, D_MODEL), lambda s: (cidx(s), 0)),
                pl.BlockSpec((CHUNK, BC_W), lambda s: (cidx(s), D_MODEL // BC_W)),
                pl.BlockSpec((CHUNK, BC_W), lambda s: (cidx(s), D_MODEL // BC_W + 1)),
                pl.BlockSpec((CHUNK, LANES), lambda s: (cidx(s), 0)),
                pl.BlockSpec((1, LANES), lambda s: (0, 0)),
                pl.BlockSpec((1, LANES), lambda s: (0, 0)),
                pl.BlockSpec((LANES, D_MODEL), lambda s: (0, 0))]
    args = [xbc_act, xbc_act, xbc_act, dt_raw, dt_bias, a_log, e_mat]
    if prev is not None:
        in_specs.append(pl.BlockSpec((CHUNK, D_MODEL), lambda s: (cidx(s), 0)))
        args.append(prev)
    out_dtype = F32 if prev is None else BF16
    return pl.pallas_call(
        functools.partial(_ssd_kernel, rev=rev, has_prev=prev is not None),
        grid=(nc,),
        in_specs=in_specs,
        out_specs=pl.BlockSpec((CHUNK, D_MODEL), lambda s: (cidx(s), 0)),
        out_shape=jax.ShapeDtypeStruct((t, D_MODEL), out_dtype),
        scratch_shapes=[pltpu.VMEM((SSM_STATE, D_MODEL), F32)],
        compiler_params=_params(("arbitrary",)),
        name="ssd_bwd" if rev else "ssd_fwd",
    )(*args)


def _ret_kernel(*refs, rev, has_prev):
    if has_prev:
        q_ref, k_ref, v_ref, dl_ref, prev_ref, o_ref, r_s, dm_s, kd_s, qd_s = refs
    else:
        q_ref, k_ref, v_ref, dl_ref, o_ref, r_s, dm_s, kd_s, qd_s = refs
        prev_ref = None
    qn = CHUNK
    lg_all = -_softplus(-dl_ref[...])

    @pl.when(pl.program_id(0) == 0)
    def _():
        r_s[...] = jnp.zeros_like(r_s)
        ii = lax.broadcasted_iota(jnp.int32, (qn, qn), 0)
        jj = lax.broadcasted_iota(jnp.int32, (qn, qn), 1)
        diff = (jj - ii) if rev else (ii - jj)
        dpos = jnp.maximum(diff, 0).astype(F32)
        k_exp = (ii if rev else (qn - 1 - ii)).astype(F32)
        q_exp = ((qn - ii) if rev else (ii + 1)).astype(F32)
        for hd in range(RET_HEADS):
            lg = lg_all[:, hd:hd + 1]
            dm_s[hd] = jnp.where(diff >= 0, jnp.exp(dpos * lg), 0.0)
            kd_s[hd] = jnp.exp(k_exp * lg)
            qd_s[hd] = jnp.exp(q_exp * lg)

    for hd in range(RET_HEADS):
        hs = slice(hd * RET_DK, (hd + 1) * RET_DK)
        qh = q_ref[:, hs]
        kh = k_ref[:, hs]
        vh = v_ref[:, hs]
        sc = lax.dot_general(qh, kh, (((1,), (1,)), ((), ())), preferred_element_type=F32)
        y = jnp.dot((sc * dm_s[hd]).astype(BF16), vh, preferred_element_type=F32)
        r_prev = r_s[hd]
        qd = qd_s[hd]
        y_int = jnp.dot(qh, r_prev.astype(BF16), preferred_element_type=F32)
        y = y + y_int * jnp.concatenate([qd, qd], axis=1)
        kd = kd_s[hd]
        kdec = (kh.astype(F32) * jnp.concatenate([kd, kd], axis=1)).astype(BF16)
        kv = lax.dot_general(kdec, vh, (((0,), (0,)), ((), ())), preferred_element_type=F32)
        cdec = jnp.exp(float(qn) * lg_all[:, hd:hd + 1])
        r_s[hd] = r_prev * cdec + kv
        if prev_ref is not None:
            y = y + prev_ref[:, hs]
        o_ref[:, hs] = y.astype(o_ref.dtype)


def _ret_scan(p_main, decay_logit, prev, n_ctx, rev):
    t = p_main.shape[0]
    nc, nc_ctx = t // CHUNK, n_ctx // CHUNK
    cidx = functools.partial(_chunk_index, nc_ctx=nc_ctx, nc=nc, rev=rev)
    in_specs = [pl.BlockSpec((CHUNK, D_MODEL), lambda s: (cidx(s), 1)),
                pl.BlockSpec((CHUNK, D_MODEL), lambda s: (cidx(s), 2)),
                pl.BlockSpec((CHUNK, D_MODEL), lambda s: (cidx(s), 3)),
                pl.BlockSpec((1, LANES), lambda s: (0, 0))]
    args = [p_main, p_main, p_main, decay_logit]
    if prev is not None:
        in_specs.append(pl.BlockSpec((CHUNK, D_MODEL), lambda s: (cidx(s), 0)))
        args.append(prev)
    out_dtype = F32 if prev is None else BF16
    return pl.pallas_call(
        functools.partial(_ret_kernel, rev=rev, has_prev=prev is not None),
        grid=(nc,),
        in_specs=in_specs,
        out_specs=pl.BlockSpec((CHUNK, D_MODEL), lambda s: (cidx(s), 0)),
        out_shape=jax.ShapeDtypeStruct((t, D_MODEL), out_dtype),
        scratch_shapes=[pltpu.VMEM((RET_HEADS, RET_DK, RET_DK), F32),
                        pltpu.VMEM((RET_HEADS, CHUNK, CHUNK), F32),
                        pltpu.VMEM((RET_HEADS, CHUNK, CHUNK), F32),
                        pltpu.VMEM((RET_HEADS, CHUNK, CHUNK), F32)],
        compiler_params=_params(("arbitrary",)),
        name="ret_bwd" if rev else "ret_fwd",
    )(*args)


def _merge_kernel(ys_ref, xs_ref, z_ref, yr_ref, g_ref, gs_ref, gr_ref, dsk_ref, snw_ref, rnw_ref,
                  ws_ref, wr_ref, o_ref):
    y = ys_ref[...].astype(F32) + dsk_ref[...] * xs_ref[...].astype(F32)
    y = y * _silu(z_ref[...].astype(F32))
    ms = jnp.mean(y * y, axis=-1, keepdims=True)
    a = (y * lax.rsqrt(ms + EPS) * snw_ref[...]).astype(BF16)
    merged = jax.nn.sigmoid(gs_ref[...].astype(F32)) * jnp.dot(a, ws_ref[...],
                                                               preferred_element_type=F32)
    gate = _silu(g_ref[...].astype(F32)) * rnw_ref[...]
    parts = []
    for hd in range(RET_HEADS):
        hs = slice(hd * RET_DK, (hd + 1) * RET_DK)
        r = yr_ref[:, hs].astype(F32)
        ms = jnp.mean(r * r, axis=-1, keepdims=True)
        parts.append((r * lax.rsqrt(ms + EPS) * gate[:, hs]).astype(BF16))
    b = jnp.concatenate(parts, axis=1)
    merged = merged + jax.nn.sigmoid(gr_ref[...].astype(F32)) * jnp.dot(
        b, wr_ref[...], preferred_element_type=F32)
    o_ref[...] = merged.astype(BF16)


def _merge(y_ssd, xbc_act, p_main, y_ret, d_skip, ssm_norm_w, ret_norm_w, w_ssd, w_ret, l):
    t, d = y_ssd.shape
    bm = MERGE_ROW_BLOCK
    once = pl.Buffered(1)
    return pl.pallas_call(
        _merge_kernel,
        grid=(t // bm,),
        in_specs=[pl.BlockSpec((bm, d), lambda i: (i, 0)),
                  pl.BlockSpec((bm, d), lambda i: (i, 0)),
                  pl.BlockSpec((bm, d), lambda i: (i, 0)),
                  pl.BlockSpec((bm, d), lambda i: (i, 0)),
                  pl.BlockSpec((bm, d), lambda i: (i, 4)),
                  pl.BlockSpec((bm, d), lambda i: (i, 5)),
                  pl.BlockSpec((bm, d), lambda i: (i, 6)),
                  pl.BlockSpec((None, 1, d), lambda i: (l, 0, 0)),
                  pl.BlockSpec((None, 1, d), lambda i: (l, 0, 0)),
                  pl.BlockSpec((None, 1, d), lambda i: (l, 0, 0)),
                  pl.BlockSpec((None, d, d), lambda i: (l, 0, 0), pipeline_mode=once),
                  pl.BlockSpec((None, d, d), lambda i: (l, 0, 0), pipeline_mode=once)],
        out_specs=pl.BlockSpec((bm, d), lambda i: (i, 0)),
        out_shape=jax.ShapeDtypeStruct((t, d), BF16),
        compiler_params=_params(("arbitrary",)),
        name="merge",
    )(y_ssd, xbc_act, p_main, y_ret, p_main, p_main, p_main, d_skip, ssm_norm_w, ret_norm_w,
      w_ssd, w_ret)


def _outproj_kernel(m_ref, w_ref, h_ref, g_ref, o_ref, *, n_ctx, bm):
    i = pl.program_id(0)
    row = i * bm + lax.broadcasted_iota(jnp.int32, (bm, 1), 0)
    gate = jnp.where(row < n_ctx, g_ref[1:2, :], g_ref[0:1, :])
    o_ref[...] = h_ref[...] + gate * jnp.dot(m_ref[...], w_ref[...], preferred_element_type=F32)


def _outproj(merged, w_out, h, mod, l, n_ctx):
    t, d = h.shape
    bm = OUT_ROW_BLOCK
    return pl.pallas_call(
        functools.partial(_outproj_kernel, n_ctx=n_ctx, bm=bm),
        grid=(t // bm,),
        in_specs=[pl.BlockSpec((bm, d), lambda i: (i, 0)),
                  pl.BlockSpec((None, d, d), lambda i: (l, 0, 0), pipeline_mode=pl.Buffered(1)),
                  pl.BlockSpec((bm, d), lambda i: (i, 0)),
                  pl.BlockSpec((None, 8, d), lambda i: (l, 0, 2))],
        out_specs=pl.BlockSpec((bm, d), lambda i: (i, 0)),
        out_shape=jax.ShapeDtypeStruct((t, d), F32),
        compiler_params=_params(("arbitrary",)),
        name="outproj",
    )(merged, w_out, h, mod)


def _router_kernel(h_ref, sh_ref, sc_ref, nw_ref, wr_ref, br_ref, xn_ref, meta_ref, cnt_ref,
                   carry_s, *, n_ctx, bm):
    i = pl.program_id(0)

    @pl.when(i == 0)
    def _():
        carry_s[...] = jnp.zeros_like(carry_s)

    xn = _modulated_norm(h_ref[...], nw_ref[...], sh_ref, sc_ref, i * bm, n_ctx)
    xn_ref[...] = xn
    logits = _dot3(xn, wr_ref[...]) + br_ref[...]
    lane = lax.broadcasted_iota(jnp.int32, (bm, LANES), 1)
    far = jnp.int32(4 * LANES)

    glog = jnp.where(lane < N_GROUPS_E, logits, NEG_BIG)
    gmax = jnp.max(glog, axis=1, keepdims=True)
    gidx = jnp.min(jnp.where(glog == gmax, lane, far), axis=1, keepdims=True)
    p_top = 1.0 / jnp.sum(jnp.exp(glog - gmax), axis=1, keepdims=True)

    in_group = ((lane >= N_GROUPS_E) & (lane < N_GROUPS_E + N_EXPERTS)
                & (((lane - N_GROUPS_E) >> 3) == gidx))
    elog = jnp.where(in_group, logits, NEG_BIG)
    m1 = jnp.max(elog, axis=1, keepdims=True)
    i1 = jnp.min(jnp.where(elog == m1, lane, far), axis=1, keepdims=True)
    elog2 = jnp.where(lane == i1, NEG_BIG, elog)
    m2 = jnp.max(elog2, axis=1, keepdims=True)
    i2 = jnp.min(jnp.where(elog2 == m2, lane, far), axis=1, keepdims=True)
    e2 = jnp.exp(m2 - m1)
    w1 = p_top / (1.0 + e2)
    w2 = p_top * e2 / (1.0 + e2)
    ex1 = i1 - N_GROUPS_E
    ex2 = i2 - N_GROUPS_E

    onehot = jnp.where((lane == ex1) | (lane == ex2), 1.0, 0.0).astype(F32)
    ri = lax.broadcasted_iota(jnp.int32, (bm, bm), 0)
    ci = lax.broadcasted_iota(jnp.int32, (bm, bm), 1)
    earlier = jnp.where(ci < ri, 1.0, 0.0).astype(BF16)
    before = jnp.dot(earlier, onehot.astype(BF16), preferred_element_type=F32) + carry_s[0:1, :]
    r1 = jnp.sum(jnp.where(lane == ex1, before, 0.0), axis=1, keepdims=True)
    r2 = jnp.sum(jnp.where(lane == ex2, before, 0.0), axis=1, keepdims=True)
    carry = carry_s[...] + jnp.sum(onehot, axis=0, keepdims=True)
    carry_s[...] = carry
    cnt_ref[...] = carry

    meta = jnp.where(lane == 0, ex1.astype(F32), 0.0)
    meta = jnp.where(lane == 1, ex2.astype(F32), meta)
    meta = jnp.where(lane == 2, w1, meta)
    meta = jnp.where(lane == 3, w2, meta)
    meta = jnp.where(lane == 4, r1, meta)
    meta = jnp.where(lane == 5, r2, meta)
    meta_ref[...] = meta


def _router(h, mod, norm_w, w_route, b_route, l, n_ctx):
    t, d = h.shape
    bm = TOK_BLOCK
    return pl.pallas_call(
        functools.partial(_router_kernel, n_ctx=n_ctx, bm=bm),
        grid=(t // bm,),
        in_specs=[pl.BlockSpec((bm, d), lambda i: (i, 0)),
                  pl.BlockSpec((None, 8, d), lambda i: (l, 0, 3)),
                  pl.BlockSpec((None, 8, d), lambda i: (l, 0, 4)),
                  pl.BlockSpec((None, 1, d), lambda i: (l, 0, 0)),
                  pl.BlockSpec((None, d, LANES), lambda i: (l, 0, 0)),
                  pl.BlockSpec((None, 1, LANES), lambda i: (l, 0, 0))],
        out_specs=[pl.BlockSpec((bm, d), lambda i: (i, 0)),
                   pl.BlockSpec((bm, LANES), lambda i: (i, 0)),
                   pl.BlockSpec((8, LANES), lambda i: (0, 0))],
        out_shape=[jax.ShapeDtypeStruct((t, d), F32),
                   jax.ShapeDtypeStruct((t, LANES), F32),
                   jax.ShapeDtypeStruct((8, LANES), F32)],
        scratch_shapes=[pltpu.VMEM((8, LANES), F32)],
        compiler_params=_params(("arbitrary",)),
        name="router",
    )(h, mod, mod, norm_w, w_route, b_route)


def _row_copy(src, src_row, dst, dst_row, sem):
    return pltpu.make_async_copy(src.at[pl.ds(src_row, 1), :], dst.at[pl.ds(dst_row, 1), :], sem)


def _dispatch_kernel(pos_ref, xn_ref, init_ref, o_ref, sem, *, bm):
    del init_ref
    base = pl.program_id(0) * bm

    def issue(r, carry):
        for k in range(2):
            _row_copy(xn_ref, r, o_ref, pos_ref[(base + r) * 2 + k], sem).start()
        return carry

    lax.fori_loop(0, bm, issue, 0, unroll=4)
    for k in range(2):
        pltpu.make_async_copy(xn_ref, o_ref.at[pl.ds(0, bm), :], sem).wait()


def _dispatch(pos, xn, init):
    t, d = xn.shape
    n_rows = init.shape[0]
    bm = TOK_BLOCK
    return pl.pallas_call(
        functools.partial(_dispatch_kernel, bm=bm),
        grid_spec=pltpu.PrefetchScalarGridSpec(
            num_scalar_prefetch=1,
            grid=(t // bm,),
            in_specs=[pl.BlockSpec((bm, d), lambda i, pos: (i, 0)),
                      pl.BlockSpec(memory_space=pl.ANY)],
            out_specs=pl.BlockSpec(memory_space=pl.ANY),
            scratch_shapes=[pltpu.SemaphoreType.DMA(())]),
        out_shape=jax.ShapeDtypeStruct((n_rows, d), F32),
        input_output_aliases={2: 0},
        compiler_params=_params(("arbitrary",)),
        name="dispatch",
    )(pos, xn, init)


def _expert_kernel(te_ref, xi_ref, oi_ref, first_ref, valid_ref, x_ref, wg_ref, wu_ref, wd_ref,
                   o_ref, wg_s, wu_s, wd_s):
    i = pl.program_id(0)

    @pl.when(first_ref[i] == 1)
    def _():
        wg_s[...] = wg_ref[...].astype(BF16)
        wu_s[...] = wu_ref[...].astype(BF16)
        wd_s[...] = wd_ref[...].astype(BF16)

    @pl.when(valid_ref[i] == 1)
    def _():
        x = x_ref[...].astype(BF16)
        gate = jnp.dot(x, wg_s[...], preferred_element_type=F32)
        up = jnp.dot(x, wu_s[...], preferred_element_type=F32)
        mid = (_silu(gate) * up).astype(BF16)
        o_ref[...] = jnp.dot(mid, wd_s[...], preferred_element_type=F32)

    @pl.when(valid_ref[i] == 0)
    def _():
        o_ref[...] = jnp.zeros_like(o_ref)


def _experts(tile_meta, x_sorted, w_gate, w_up, w_down, l, n_tiles):
    d = x_sorted.shape[1]
    tm = EXPERT_TILE
    return pl.pallas_call(
        _expert_kernel,
        grid_spec=pltpu.PrefetchScalarGridSpec(
            num_scalar_prefetch=5,
            grid=(n_tiles,),
            in_specs=[pl.BlockSpec((tm, d), lambda i, te, xi, oi, fi, va: (xi[i], 0)),
                      pl.BlockSpec((None, None, d, D_EXPERT),
                                   lambda i, te, xi, oi, fi, va: (l, te[i], 0, 0)),
                      pl.BlockSpec((None, None, d, D_EXPERT),
                                   lambda i, te, xi, oi, fi, va: (l, te[i], 0, 0)),
                      pl.BlockSpec((None, None, D_EXPERT, d),
                                   lambda i, te, xi, oi, fi, va: (l, te[i], 0, 0))],
            out_specs=pl.BlockSpec((tm, d), lambda i, te, xi, oi, fi, va: (oi[i], 0)),
            scratch_shapes=[pltpu.VMEM((d, D_EXPERT), BF16), pltpu.VMEM((d, D_EXPERT), BF16),
                            pltpu.VMEM((D_EXPERT, d), BF16)]),
        out_shape=jax.ShapeDtypeStruct((n_tiles * tm, d), F32),
        compiler_params=_params(("arbitrary",)),
        name="experts",
    )(*tile_meta, x_sorted, w_gate, w_up, w_down)


def _combine_kernel(pos_ref, h_ref, meta_ref, g_ref, ys_ref, o_ref, buf_s, sem, *, n_ctx, bm):
    i = pl.program_id(0)
    n = pl.num_programs(0)

    def issue(blk, slot):
        base = blk * bm

        def body(r, carry):
            for k in range(2):
                _row_copy(ys_ref, pos_ref[(base + r) * 2 + k], buf_s.at[slot, k], r,
                          sem.at[slot]).start()
            return carry

        lax.fori_loop(0, bm, body, 0, unroll=4)

    @pl.when(i == 0)
    def _():
        issue(0, 0)

    @pl.when(i + 1 < n)
    def _():
        issue(i + 1, (i + 1) % 2)

    slot = i % 2
    for k in range(2):
        pltpu.make_async_copy(ys_ref.at[pl.ds(0, bm), :], buf_s.at[slot, k], sem.at[slot]).wait()
    meta = meta_ref[...]
    moe = meta[:, 2:3] * buf_s[slot, 0] + meta[:, 3:4] * buf_s[slot, 1]
    row = i * bm + lax.broadcasted_iota(jnp.int32, (bm, 1), 0)
    gate = jnp.where(row < n_ctx, g_ref[1:2, :], g_ref[0:1, :])
    o_ref[...] = h_ref[...] + gate * moe


def _combine(pos, h, meta, mod, y_sorted, l, n_ctx):
    t, d = h.shape
    bm = TOK_BLOCK
    return pl.pallas_call(
        functools.partial(_combine_kernel, n_ctx=n_ctx, bm=bm),
        grid_spec=pltpu.PrefetchScalarGridSpec(
            num_scalar_prefetch=1,
            grid=(t // bm,),
            in_specs=[pl.BlockSpec((bm, d), lambda i, pos: (i, 0)),
                      pl.BlockSpec((bm, LANES), lambda i, pos: (i, 0)),
                      pl.BlockSpec((None, 8, d), lambda i, pos: (l, 0, 5)),
                      pl.BlockSpec(memory_space=pl.ANY)],
            out_specs=pl.BlockSpec((bm, d), lambda i, pos: (i, 0)),
            scratch_shapes=[pltpu.VMEM((2, 2, bm, d), F32), pltpu.SemaphoreType.DMA((2,))]),
        out_shape=jax.ShapeDtypeStruct((t, d), F32),
        compiler_params=_params(("arbitrary",)),
        name="combine",
    )(pos, h, meta, mod, y_sorted)


def _tile_plan(meta, counts, n_tiles):
    tm = EXPERT_TILE
    cnt = counts[0, :N_EXPERTS].astype(jnp.int32)
    tiles_e = (cnt + tm - 1) // tm
    tile_end = jnp.cumsum(tiles_e)
    offset = (tile_end - tiles_e) * tm
    expert = meta[:, 0:2].astype(jnp.int32)
    rank = meta[:, 4:6].astype(jnp.int32)
    pos = (offset[expert] + rank).reshape(-1)
    n_used = tile_end[-1]
    tid = jnp.arange(n_tiles, dtype=jnp.int32)
    valid = tid < n_used
    tsrc = jnp.minimum(tid, n_used - 1)
    te = jnp.minimum(jnp.searchsorted(tile_end, tsrc, side="right"), N_EXPERTS - 1).astype(jnp.int32)
    first = (valid & ((tid == 0) | (te != jnp.roll(te, 1)))).astype(jnp.int32)
    oi = tid
    return pos, (te, tsrc.astype(jnp.int32), oi, first, valid.astype(jnp.int32))


def _final_kernel(h_ref, w_ref, o_ref):
    x = h_ref[...]
    ms = jnp.mean(x * x, axis=-1, keepdims=True)
    o_ref[...] = x * lax.rsqrt(ms + EPS) * w_ref[...]


def _final_norm(h, w, n_ctx):
    t, d = h.shape
    bm = TOK_BLOCK
    skip = n_ctx // bm
    return pl.pallas_call(
        _final_kernel,
        grid=((t - n_ctx) // bm,),
        in_specs=[pl.BlockSpec((bm, d), lambda i: (i + skip, 0)),
                  pl.BlockSpec((1, d), lambda i: (0, 0))],
        out_specs=pl.BlockSpec((bm, d), lambda i: (i, 0)),
        out_shape=jax.ShapeDtypeStruct((t - n_ctx, d), F32),
        compiler_params=_params(("arbitrary",)),
        name="final_norm",
    )(h, w)


def _pad_lanes(v):
    return jnp.pad(v, [(0, 0)] * (v.ndim - 1) + [(0, LANES - v.shape[-1])])


def _rope_tables(n_ctx, n_lat):
    rows = n_lat // GRID_W
    row = jnp.broadcast_to(jnp.arange(rows, dtype=F32)[:, None], (rows, GRID_W)).reshape(n_lat)
    col = jnp.broadcast_to(jnp.arange(GRID_W, dtype=F32)[None, :], (rows, GRID_W)).reshape(n_lat)
    inv_freq = ROPE_BASE ** (-jnp.arange(ROPE_FREQS, dtype=F32) / ROPE_FREQS)
    ang_r = row[:, None] * inv_freq[None, :]
    ang_c = col[:, None] * inv_freq[None, :]
    cos_l = jnp.concatenate([jnp.cos(ang_r)] * 2 + [jnp.cos(ang_c)] * 2, axis=1)
    sin_l = jnp.concatenate([-jnp.sin(ang_r), jnp.sin(ang_r), -jnp.sin(ang_c), jnp.sin(ang_c)], axis=1)
    cos_t = jnp.concatenate([jnp.ones((n_ctx, 2 * LANES), F32), cos_l], axis=0)
    sin_t = jnp.concatenate([jnp.zeros((n_ctx, 2 * LANES), F32), sin_l], axis=0)
    return cos_t, sin_t


def kernel(x, c, ctx, c_ctx, w_mod, b_mod, norm1_w, w_in, conv_w, conv_b, ssm_A_log, ssm_dt_bias,
           ssm_D, ssm_norm_w, ret_decay_logit, ret_norm_w, w_ssd_proj, w_ret_proj, w_out, norm2_w,
           w_route_group, b_route_group, w_route_expert, b_route_expert, w_exp_gate, w_exp_up,
           w_exp_down, final_norm_w):
    batch, n_lat, d = x.shape
    n_ctx = ctx.shape[1]
    depth = w_mod.shape[0]
    assert batch == 1 and d == D_MODEL and w_in.shape[-1] == MAIN_W + DT_W
    t = n_ctx + n_lat
    assert t % OUT_ROW_BLOCK == 0 and n_ctx % TOK_BLOCK == 0 and n_lat % TOK_BLOCK == 0

    w_nat, w_dt = _wprep(jnp.swapaxes(w_in, 1, 2))
    w_ssd_b = w_ssd_proj.astype(BF16)
    w_ret_b = w_ret_proj.astype(BF16)
    w_out_b = w_out.astype(BF16)
    w_route = _pad_lanes(jnp.concatenate([w_route_group, w_route_expert], axis=-1))
    b_route = _pad_lanes(jnp.concatenate([b_route_group, b_route_expert], axis=-1))[:, None, :]
    d_skip = jnp.repeat(ssm_D, SSM_HEADDIM, axis=-1)[:, None, :]
    dt_bias = _pad_lanes(ssm_dt_bias)[:, :, None, :]
    a_log = _pad_lanes(ssm_A_log)[:, :, None, :]
    ret_dl = _pad_lanes(ret_decay_logit)[:, :, None, :]
    head_of_col = jnp.arange(D_MODEL, dtype=jnp.int32) // SSM_HEADDIM
    e_mat = (jnp.arange(LANES, dtype=jnp.int32)[:, None] == head_of_col[None, :]).astype(BF16)
    cos_t, sin_t = _rope_tables(n_ctx, n_lat)

    cvec = jnp.concatenate([c, c_ctx[None, :], jnp.zeros((6, d), F32)], axis=0)
    mod = _modulation(cvec, w_mod, b_mod)

    n_tiles = (2 * t) // EXPERT_TILE + N_EXPERTS
    h = jnp.concatenate([ctx[0], x[0]], axis=0)
    x_sorted = jnp.zeros((n_tiles * EXPERT_TILE, d), F32)
    for l in range(depth):
        p_main, dt_raw = _inproj(h, mod, norm1_w[:, None, :], w_nat, w_dt, cos_t, sin_t, l, n_ctx)
        xbc_act = _conv(p_main, conv_w, conv_b[:, None, :], l, n_ctx)
        y_ssd = _ssd_scan(xbc_act, dt_raw, dt_bias[l, 0], a_log[l, 0], e_mat, None, n_ctx, False)
        y_ssd = _ssd_scan(xbc_act, dt_raw, dt_bias[l, 1], a_log[l, 1], e_mat, y_ssd, n_ctx, True)
        y_ret = _ret_scan(p_main, ret_dl[l, 0], None, n_ctx, False)
        y_ret = _ret_scan(p_main, ret_dl[l, 1], y_ret, n_ctx, True)
        merged = _merge(y_ssd, xbc_act, p_main, y_ret, d_skip, ssm_norm_w[:, None, :],
                        ret_norm_w[:, None, :], w_ssd_b, w_ret_b, l)
        h = _outproj(merged, w_out_b, h, mod, l, n_ctx)
        xn, meta, counts = _router(h, mod, norm2_w[:, None, :], w_route, b_route, l, n_ctx)
        pos, tile_meta = _tile_plan(meta, counts, n_tiles)
        x_sorted = _dispatch(pos, xn, x_sorted)
        y_sorted = _experts(tile_meta, x_sorted, w_exp_gate, w_exp_up, w_exp_down, l, n_tiles)
        h = _combine(pos, h, meta, mod, y_sorted, l, n_ctx)

    return _final_norm(h, final_norm_w[None, :], n_ctx)[None]
```

```python
import functools

import jax
import jax.numpy as jnp
from jax import lax
from jax.experimental import pallas as pl
from jax.experimental.pallas import tpu as pltpu

F32 = jnp.float32
BF16 = jnp.bfloat16
HIGHEST = lax.Precision.HIGHEST

D_MODEL = 2048
GRID_W = 64
CHUNK = 128
EPS = 1e-6

SSM_HEADDIM = 64
SSM_HEADS = 32
SSM_GROUPS = 4
SSM_HPG = 8
SSM_STATE = 128
SSM_CONV = 5
GROUP_W = SSM_HPG * SSM_HEADDIM
BC_W = SSM_GROUPS * SSM_STATE
XBC_D = D_MODEL + 2 * BC_W

RET_HEADS = 8
RET_DK = 256
ROPE_FREQS = 64
ROPE_BASE = 10000.0

N_GROUPS_E = 4
EXPERTS_PER_GROUP = 8
N_EXPERTS = 32
D_EXPERT = 512

LANES = 128
MAIN_W = 7 * D_MODEL + XBC_D
XBC_COL0 = 7 * D_MODEL
DT_W = SSM_HEADS
DT_COL0 = D_MODEL + XBC_D

INPROJ_ROW_BLOCKS = (1056, 768)
COL_BLOCK = 1024
OUT_ROW_BLOCK = 384
MERGE_ROW_BLOCK = 256
MERGE_SUB_ROWS = 256
TOK_BLOCK = 256
EXPERT_TILE = 256
NEG_BIG = -1e30
VMEM_LIMIT = 56 * 1024 * 1024


def _params(sem, vmem=VMEM_LIMIT):
    return pltpu.CompilerParams(dimension_semantics=sem, vmem_limit_bytes=vmem)


def _sigmoid(v):
    return 0.5 * jnp.tanh(0.5 * v) + 0.5


def _silu(v):
    return v * _sigmoid(v)


def _softplus(v):
    return jnp.maximum(v, 0.0) + jnp.log1p(jnp.exp(-jnp.abs(v)))


def _split_bf16(v):
    hi = v.astype(BF16)
    return hi, (v - hi.astype(F32)).astype(BF16)


def _dot3(x, w):
    xh, xl = _split_bf16(x)
    wh, wl = _split_bf16(w)
    return (jnp.dot(xh, wh, preferred_element_type=F32) + jnp.dot(xl, wh, preferred_element_type=F32)
            + jnp.dot(xh, wl, preferred_element_type=F32))


def _modulated_norm(x, nw, sh_ref, sc_ref, row0, n_ctx):
    ms = jnp.mean(x * x, axis=-1, keepdims=True)
    y = x * lax.rsqrt(ms + EPS) * nw
    row = row0 + lax.broadcasted_iota(jnp.int32, (x.shape[0], 1), 0)
    is_ctx = row < n_ctx
    sc = jnp.where(is_ctx, sc_ref[1:2, :], sc_ref[0:1, :])
    sh = jnp.where(is_ctx, sh_ref[1:2, :], sh_ref[0:1, :])
    return y * (1.0 + sc) + sh


def _mod_kernel(a_ref, w_ref, b_ref, o_ref):
    a = a_ref[...]
    o_ref[...] = jnp.dot(_silu(a), w_ref[...], precision=HIGHEST,
                         preferred_element_type=F32) + b_ref[...]


def _modulation(cvec, w_mod, b_mod):
    depth, d, n = w_mod.shape
    bn = 1024
    return pl.pallas_call(
        _mod_kernel,
        grid=(depth, n // bn),
        in_specs=[pl.BlockSpec((8, d), lambda l, j: (0, 0)),
                  pl.BlockSpec((None, d, bn), lambda l, j: (l, 0, j)),
                  pl.BlockSpec((None, 1, bn), lambda l, j: (l, 0, j))],
        out_specs=pl.BlockSpec((None, 8, bn), lambda l, j: (l, 0, j)),
        out_shape=jax.ShapeDtypeStruct((depth, 8, n), F32),
        compiler_params=_params(("arbitrary", "arbitrary")),
        name="modulation",
    )(cvec, w_mod, b_mod.reshape(depth, 1, n))


def _dot3_nt(x, wt):
    nt = (((1,), (1,)), ((), ()))
    n = wt.shape[0]
    xh, xl = _split_bf16(x)
    wh, wl = _split_bf16(wt)
    both = lax.dot_general(xh, jnp.concatenate([wh, wl], axis=0), nt, preferred_element_type=F32)
    return both[:, :n] + both[:, n:] + lax.dot_general(xl, wh, nt, preferred_element_type=F32)


def _wprep_kernel(w_hbm, o_ref, dt_ref, buf_s, dtb_s, sem, dsem, *, bn, n_aligned, nj):
    l = pl.program_id(0)
    n = pl.program_id(1)
    step = l * nj + n
    total = pl.num_programs(0) * nj

    def fetch(ll, nn, slot):
        row0 = pl.multiple_of(nn * bn + jnp.where(nn >= n_aligned, DT_W, 0), 8)
        return pltpu.make_async_copy(w_hbm.at[ll, pl.ds(row0, bn), :], buf_s.at[slot], sem.at[slot])

    @pl.when(step == 0)
    def _():
        fetch(l, n, 0).start()

    @pl.when(step + 1 < total)
    def _():
        wrap = n + 1 == nj
        fetch(jnp.where(wrap, l + 1, l), jnp.where(wrap, 0, n + 1), (step + 1) % 2).start()

    fetch(l, n, step % 2).wait()
    o_ref[...] = buf_s[step % 2].astype(BF16)

    @pl.when(n == 0)
    def _():
        cp = pltpu.make_async_copy(w_hbm.at[l, pl.ds(DT_COL0, DT_W), :], dtb_s, dsem)
        cp.start()
        cp.wait()
        dt_ref[0:DT_W, :] = dtb_s[...]
        dt_ref[DT_W:, :] = jnp.zeros((LANES - DT_W, dt_ref.shape[1]), F32)


def _wprep(w_in_t):
    depth, _, d = w_in_t.shape
    bn = COL_BLOCK
    nj = MAIN_W // bn
    return pl.pallas_call(
        functools.partial(_wprep_kernel, bn=bn, n_aligned=DT_COL0 // bn, nj=nj),
        grid=(depth, nj),
        in_specs=[pl.BlockSpec(memory_space=pl.ANY)],
        out_specs=[pl.BlockSpec((None, bn, d), lambda l, n: (l, n, 0)),
                   pl.BlockSpec((None, LANES, d), lambda l, n: (l, 0, 0))],
        out_shape=[jax.ShapeDtypeStruct((depth, MAIN_W, d), BF16),
                   jax.ShapeDtypeStruct((depth, LANES, d), F32)],
        scratch_shapes=[pltpu.VMEM((2, bn, d), F32), pltpu.VMEM((DT_W, d), F32),
                        pltpu.SemaphoreType.DMA((2,)), pltpu.SemaphoreType.DMA(())],
        compiler_params=_params(("arbitrary", "arbitrary")),
        name="wprep",
    )(w_in_t)


def _inproj_kernel(h_ref, sh_ref, sc_ref, nw_ref, w_ref, wdt_ref, cos_ref, sin_ref,
                   o_ref, dt_ref, xn_s, *, n_ctx, bm, bn):
    i = pl.program_id(0)
    j = pl.program_id(1)

    @pl.when(j == 0)
    def _():
        xn = _modulated_norm(h_ref[...], nw_ref[...], sh_ref, sc_ref, i * bm, n_ctx)
        xn_s[...] = xn.astype(BF16)
        dt_ref[...] = _dot3_nt(xn, wdt_ref[...])

    acc = lax.dot_general(xn_s[...], w_ref[...], (((1,), (1,)), ((), ())),
                          preferred_element_type=F32)
    q_lo = DT_COL0 // bn
    k_lo = (DT_COL0 + D_MODEL) // bn
    k_hi = (DT_COL0 + 2 * D_MODEL) // bn
    is_rope = (j >= q_lo) & (j < k_hi)

    @pl.when(is_rope)
    def _():
        scale = jnp.where(j >= k_lo, RET_DK ** -0.5, 1.0).astype(F32)
        for gi in range(bn // LANES):
            u = acc[:, gi * LANES:(gi + 1) * LANES]
            half = gi % 2
            c = cos_ref[:, half * LANES:(half + 1) * LANES]
            s = sin_ref[:, half * LANES:(half + 1) * LANES]
            r = (u * c + pltpu.roll(u, LANES // 2, 1) * s) * scale
            o_ref[:, gi * LANES:(gi + 1) * LANES] = r.astype(BF16)

    @pl.when(jnp.logical_not(is_rope))
    def _():
        o_ref[...] = acc.astype(BF16)


def _inproj(h, mod, norm_w, w_nat, w_dt, cos_t, sin_t, l, n_ctx):
    t, d = h.shape
    bm = next(b for b in INPROJ_ROW_BLOCKS if t % b == 0)
    bn = COL_BLOCK
    nj = MAIN_W // bn
    z_blocks = D_MODEL // bn
    xbc_blocks = XBC_D // bn

    def out_block(j):
        return jnp.where(j < z_blocks, j,
                         jnp.where(j < z_blocks + xbc_blocks, j + (nj - z_blocks - xbc_blocks),
                                   j - xbc_blocks))

    once = pl.Buffered(1)
    kern = functools.partial(_inproj_kernel, n_ctx=n_ctx, bm=bm, bn=bn)
    return pl.pallas_call(
        kern,
        grid=(t // bm, nj),
        in_specs=[pl.BlockSpec((bm, d), lambda i, j: (i, 0), pipeline_mode=once),
                  pl.BlockSpec((None, 8, d), lambda i, j: (l, 0, 0)),
                  pl.BlockSpec((None, 8, d), lambda i, j: (l, 0, 1)),
                  pl.BlockSpec((None, 1, d), lambda i, j: (l, 0, 0)),
                  pl.BlockSpec((None, bn, d), lambda i, j: (l, j, 0)),
                  pl.BlockSpec((None, LANES, d), lambda i, j: (l, 0, 0)),
                  pl.BlockSpec((bm, 2 * LANES), lambda i, j: (i, 0), pipeline_mode=once),
                  pl.BlockSpec((bm, 2 * LANES), lambda i, j: (i, 0), pipeline_mode=once)],
        out_specs=[pl.BlockSpec((bm, bn), lambda i, j: (i, out_block(j))),
                   pl.BlockSpec((bm, LANES), lambda i, j: (i, 0))],
        out_shape=[jax.ShapeDtypeStruct((t, MAIN_W), BF16),
                   jax.ShapeDtypeStruct((t, LANES), F32)],
        scratch_shapes=[pltpu.VMEM((bm, d), BF16)],
        compiler_params=_params(("arbitrary", "arbitrary")),
        name="inproj",
    )(h, mod, mod, norm_w, w_nat, w_dt, cos_t, sin_t)


def _conv_kernel(x_ref, w_ref, b_ref, o_ref, pad_s, *, n_ctx, n_lat, rows):
    halo = 8
    zeros = jnp.zeros((halo, LANES), F32)
    c0 = halo
    x0 = 2 * halo + n_ctx
    pad_s[0:halo, :] = zeros
    pad_s[c0:c0 + n_ctx, :] = x_ref[0:n_ctx, :].astype(F32)
    pad_s[c0 + n_ctx:x0, :] = zeros
    pad_s[x0:x0 + n_lat, :] = x_ref[n_ctx:n_ctx + n_lat, :].astype(F32)
    pad_s[x0 + n_lat:x0 + n_lat + halo, :] = zeros
    w = w_ref[...]
    b = b_ref[...]

    def segment(src0, dst0, n):
        def body(s, carry):
            r0 = pl.multiple_of(s * rows, rows)
            acc = jnp.broadcast_to(b, (rows, LANES))
            for kk in range(SSM_CONV):
                acc = acc + w[kk:kk + 1, :] * pad_s[pl.ds(src0 + r0 + kk - SSM_CONV // 2, rows), :]
            o_ref[pl.ds(dst0 + r0, rows), :] = _silu(acc).astype(BF16)
            return carry
        lax.fori_loop(0, n // rows, body, 0)

    segment(c0, 0, n_ctx)
    segment(x0, n_ctx, n_lat)


def _conv(p_main, conv_w, conv_b, l, n_ctx):
    t = p_main.shape[0]
    n_lat = t - n_ctx
    col0 = XBC_COL0 // LANES
    kern = functools.partial(_conv_kernel, n_ctx=n_ctx, n_lat=n_lat, rows=256)
    return pl.pallas_call(
        kern,
        grid=(XBC_D // LANES,),
        in_specs=[pl.BlockSpec((t, LANES), lambda c: (0, col0 + c)),
                  pl.BlockSpec((None, SSM_CONV, LANES), lambda c: (l, 0, c)),
                  pl.BlockSpec((None, 1, LANES), lambda c: (l, 0, c))],
        out_specs=pl.BlockSpec((t, LANES), lambda c: (0, c)),
        out_shape=jax.ShapeDtypeStruct((t, XBC_D), BF16),
        scratch_shapes=[pltpu.VMEM((t + 24, LANES), F32)],
        compiler_params=_params(("arbitrary",)),
        name="conv_silu",
    )(p_main, conv_w, conv_b)


def _chunk_index(s, nc_ctx, nc, rev):
    if not rev:
        return s
    return jnp.where(s < nc_ctx, nc_ctx - 1 - s, nc - 1 - (s - nc_ctx))


def _split_dot(v, e):
    hi, lo = _split_bf16(v)
    return (jnp.dot(hi, e, preferred_element_type=F32) + jnp.dot(lo, e, preferred_element_type=F32))


def _ssd_kernel(xs_f, b_f, c_f, dt_f, xs_b, b_b, c_b, dt_b, bias_ref, alog_ref, e_ref,
                of_ref, ob_ref, h_s):
    @pl.when(pl.program_id(0) == 0)
    def _():
        h_s[...] = jnp.zeros_like(h_s)

    e = e_ref[...]
    _ssd_chunk(xs_f, b_f, c_f, dt_f, bias_ref[0:1, :], alog_ref[0:1, :], e, of_ref, h_s.at[0], False)
    _ssd_chunk(xs_b, b_b, c_b, dt_b, bias_ref[1:2, :], alog_ref[1:2, :], e, ob_ref, h_s.at[1], True)


def _ssd_chunk(xs_ref, b_ref, c_ref, dt_ref, bias, alog, e, o_ref, h_s, rev):
    q = CHUNK
    ii = lax.broadcasted_iota(jnp.int32, (q, q), 0)
    jj = lax.broadcasted_iota(jnp.int32, (q, q), 1)
    seen = (jj >= ii) if rev else (jj <= ii)
    tri = jnp.where(seen, 1.0, 0.0).astype(F32)

    dt = _softplus(dt_ref[...] + bias)
    a = dt * (-jnp.exp(alog))
    cum = jnp.dot(tri, a, precision=HIGHEST, preferred_element_type=F32)
    cum_t = cum.T
    dt_t = dt.T
    total = cum[0:1, :] if rev else cum[q - 1:q, :]
    dfs = jnp.exp(cum)
    w_end = dt * jnp.exp(total - cum)
    cdec = jnp.broadcast_to(jnp.exp(total), (8, LANES))
    expanded = _split_dot(jnp.concatenate([w_end, dfs, cdec], axis=0), e)
    w_end_x = expanded[0:q]
    dfs_x = expanded[q:2 * q]
    cdec_x = expanded[2 * q:2 * q + 1]

    xs = xs_ref[...]
    xw = (xs.astype(F32) * w_end_x).astype(BF16)

    for g in range(SSM_GROUPS):
        cg = c_ref[:, g * SSM_STATE:(g + 1) * SSM_STATE]
        bg = b_ref[:, g * SSM_STATE:(g + 1) * SSM_STATE]
        gs = slice(g * GROUP_W, (g + 1) * GROUP_W)
        cb = lax.dot_general(cg, bg, (((1,), (1,)), ((), ())), preferred_element_type=F32)
        h_prev = h_s[:, gs]
        y_off = jnp.dot(cg, h_prev.astype(BF16), preferred_element_type=F32) * dfs_x[:, gs]
        new = lax.dot_general(bg, xw[:, gs], (((0,), (0,)), ((), ())), preferred_element_type=F32)
        pieces = []
        for hh in range(SSM_HPG):
            hd = g * SSM_HPG + hh
            seg = cum[:, hd:hd + 1] - cum_t[hd:hd + 1, :]
            m = jnp.exp(jnp.where(seen, seg, NEG_BIG)) * cb * dt_t[hd:hd + 1, :]
            pieces.append(jnp.dot(m.astype(BF16), xs[:, hd * SSM_HEADDIM:(hd + 1) * SSM_HEADDIM],
                                  preferred_element_type=F32))
        y = jnp.concatenate(pieces, axis=1) + y_off
        o_ref[:, gs] = y.astype(o_ref.dtype)
        h_s[:, gs] = h_prev * cdec_x[:, gs] + new


def _scan_specs(n_ctx, t, col_blocks, width):
    nc, nc_ctx = t // CHUNK, n_ctx // CHUNK
    specs = []
    for rev in (False, True):
        cidx = functools.partial(_chunk_index, nc_ctx=nc_ctx, nc=nc, rev=rev)
        specs.append([pl.BlockSpec((CHUNK, w), lambda s, c=c, cidx=cidx: (cidx(s), c))
                      for c, w in zip(col_blocks, width)])
    return specs


def _ssd_scan(xbc_act, dt_raw, dt_bias, a_log, e_mat, n_ctx):
    t = xbc_act.shape[0]
    bc0 = D_MODEL // BC_W
    fwd, bwd = _scan_specs(n_ctx, t, (0, bc0, bc0 + 1, 0), (D_MODEL, BC_W, BC_W, LANES))
    out_f, out_b = _scan_specs(n_ctx, t, (0,), (D_MODEL,))
    const = [pl.BlockSpec((2, LANES), lambda s: (0, 0)),
             pl.BlockSpec((2, LANES), lambda s: (0, 0)),
             pl.BlockSpec((LANES, D_MODEL), lambda s: (0, 0))]
    seqs = (xbc_act, xbc_act, xbc_act, dt_raw)
    return pl.pallas_call(
        _ssd_kernel,
        grid=(t // CHUNK,),
        in_specs=fwd + bwd + const,
        out_specs=out_f + out_b,
        out_shape=[jax.ShapeDtypeStruct((t, D_MODEL), BF16)] * 2,
        scratch_shapes=[pltpu.VMEM((2, SSM_STATE, D_MODEL), F32)],
        compiler_params=_params(("arbitrary",)),
        name="ssd_scan",
    )(*seqs, *seqs, dt_bias, a_log, e_mat)


def _ret_kernel(q_f, k_f, v_f, q_b, k_b, v_b, dl_ref, of_ref, ob_ref, r_s, dm_s, kd_s, qd_s):
    qn = CHUNK
    lg_both = -_softplus(-dl_ref[...])

    @pl.when(pl.program_id(0) == 0)
    def _():
        r_s[...] = jnp.zeros_like(r_s)
        ii = lax.broadcasted_iota(jnp.int32, (qn, qn), 0)
        jj = lax.broadcasted_iota(jnp.int32, (qn, qn), 1)
        for di, rev in enumerate((False, True)):
            diff = (jj - ii) if rev else (ii - jj)
            dpos = jnp.maximum(diff, 0).astype(F32)
            k_exp = (ii if rev else (qn - 1 - ii)).astype(F32)
            q_exp = ((qn - ii) if rev else (ii + 1)).astype(F32)
            for hd in range(RET_HEADS):
                lg = lg_both[di:di + 1, hd:hd + 1]
                dm_s[di, hd] = jnp.where(diff >= 0, jnp.exp(dpos * lg), 0.0)
                kd_s[di, hd] = jnp.exp(k_exp * lg)
                qd_s[di, hd] = jnp.exp(q_exp * lg)

    _ret_chunk(q_f, k_f, v_f, lg_both[0:1, :], of_ref, r_s.at[0], dm_s.at[0], kd_s.at[0], qd_s.at[0])
    _ret_chunk(q_b, k_b, v_b, lg_both[1:2, :], ob_ref, r_s.at[1], dm_s.at[1], kd_s.at[1], qd_s.at[1])


def _ret_chunk(q_ref, k_ref, v_ref, lg_all, o_ref, r_s, dm_s, kd_s, qd_s):
    qn = CHUNK
    for hd in range(RET_HEADS):
        hs = slice(hd * RET_DK, (hd + 1) * RET_DK)
        qh = q_ref[:, hs]
        kh = k_ref[:, hs]
        vh = v_ref[:, hs]
        sc = lax.dot_general(qh, kh, (((1,), (1,)), ((), ())), preferred_element_type=F32)
        y = jnp.dot((sc * dm_s[hd]).astype(BF16), vh, preferred_element_type=F32)
        r_prev = r_s[hd]
        qd = qd_s[hd]
        y_int = jnp.dot(qh, r_prev.astype(BF16), preferred_element_type=F32)
        y = y + y_int * jnp.concatenate([qd, qd], axis=1)
        kd = kd_s[hd]
        kdec = (kh.astype(F32) * jnp.concatenate([kd, kd], axis=1)).astype(BF16)
        kv = lax.dot_general(kdec, vh, (((0,), (0,)), ((), ())), preferred_element_type=F32)
        cdec = jnp.exp(float(qn) * lg_all[:, hd:hd + 1])
        r_s[hd] = r_prev * cdec + kv
        o_ref[:, hs] = y.astype(o_ref.dtype)


def _ret_scan(p_main, decay_logit, n_ctx):
    t = p_main.shape[0]
    fwd, bwd = _scan_specs(n_ctx, t, (1, 2, 3), (D_MODEL,) * 3)
    out_f, out_b = _scan_specs(n_ctx, t, (0,), (D_MODEL,))
    seqs = (p_main, p_main, p_main)
    decay_tables = pltpu.VMEM((2, RET_HEADS, CHUNK, CHUNK), F32)
    return pl.pallas_call(
        _ret_kernel,
        grid=(t // CHUNK,),
        in_specs=fwd + bwd + [pl.BlockSpec((2, LANES), lambda s: (0, 0))],
        out_specs=out_f + out_b,
        out_shape=[jax.ShapeDtypeStruct((t, D_MODEL), BF16)] * 2,
        scratch_shapes=[pltpu.VMEM((2, RET_HEADS, RET_DK, RET_DK), F32),
                        decay_tables, decay_tables, decay_tables],
        compiler_params=_params(("arbitrary",)),
        name="ret_scan",
    )(*seqs, *seqs, decay_logit)


def _merge_kernel(ysf_ref, ysb_ref, xs_ref, z_ref, yrf_ref, yrb_ref, g_ref, gs_ref, gr_ref,
                  dsk_ref, snw_ref, rnw_ref, ws_ref, wr_ref, o_ref, *, sub):
    for r0 in range(0, o_ref.shape[0], sub):
        rows = slice(r0, r0 + sub)
        y = (ysf_ref[rows, :].astype(F32) + ysb_ref[rows, :].astype(F32)
             + dsk_ref[...] * xs_ref[rows, :].astype(F32))
        y = y * _silu(z_ref[rows, :].astype(F32))
        ms = jnp.mean(y * y, axis=-1, keepdims=True)
        a = (y * lax.rsqrt(ms + EPS) * snw_ref[...]).astype(BF16)
        merged = _sigmoid(gs_ref[rows, :].astype(F32)) * jnp.dot(a, ws_ref[...],
                                                                 preferred_element_type=F32)
        gate = _silu(g_ref[rows, :].astype(F32)) * rnw_ref[...]
        parts = []
        for hd in range(RET_HEADS):
            hs = slice(hd * RET_DK, (hd + 1) * RET_DK)
            r = yrf_ref[rows, hs].astype(F32) + yrb_ref[rows, hs].astype(F32)
            ms = jnp.mean(r * r, axis=-1, keepdims=True)
            parts.append((r * lax.rsqrt(ms + EPS) * gate[:, hs]).astype(BF16))
        b = jnp.concatenate(parts, axis=1)
        merged = merged + _sigmoid(gr_ref[rows, :].astype(F32)) * jnp.dot(
            b, wr_ref[...], preferred_element_type=F32)
        o_ref[rows, :] = merged.astype(BF16)


def _merge(y_ssd, xbc_act, p_main, y_ret, d_skip, ssm_norm_w, ret_norm_w, w_ssd, w_ret, l):
    t, d = xbc_act.shape[0], D_MODEL
    bm = MERGE_ROW_BLOCK
    once = pl.Buffered(1)
    return pl.pallas_call(
        functools.partial(_merge_kernel, sub=MERGE_SUB_ROWS),
        grid=(t // bm,),
        in_specs=[pl.BlockSpec((bm, d), lambda i: (i, 0)),
                  pl.BlockSpec((bm, d), lambda i: (i, 0)),
                  pl.BlockSpec((bm, d), lambda i: (i, 0)),
                  pl.BlockSpec((bm, d), lambda i: (i, 0)),
                  pl.BlockSpec((bm, d), lambda i: (i, 0)),
                  pl.BlockSpec((bm, d), lambda i: (i, 0)),
                  pl.BlockSpec((bm, d), lambda i: (i, 4)),
                  pl.BlockSpec((bm, d), lambda i: (i, 5)),
                  pl.BlockSpec((bm, d), lambda i: (i, 6)),
                  pl.BlockSpec((None, 1, d), lambda i: (l, 0, 0)),
                  pl.BlockSpec((None, 1, d), lambda i: (l, 0, 0)),
                  pl.BlockSpec((None, 1, d), lambda i: (l, 0, 0)),
                  pl.BlockSpec((None, d, d), lambda i: (l, 0, 0), pipeline_mode=once),
                  pl.BlockSpec((None, d, d), lambda i: (l, 0, 0), pipeline_mode=once)],
        out_specs=pl.BlockSpec((bm, d), lambda i: (i, 0)),
        out_shape=jax.ShapeDtypeStruct((t, d), BF16),
        compiler_params=_params(("arbitrary",)),
        name="merge",
    )(*y_ssd, xbc_act, p_main, *y_ret, p_main, p_main, p_main, d_skip, ssm_norm_w, ret_norm_w,
      w_ssd, w_ret)


def _outproj_kernel(m_ref, w_ref, h_ref, g_ref, o_ref, *, n_ctx, bm):
    i = pl.program_id(0)
    row = i * bm + lax.broadcasted_iota(jnp.int32, (bm, 1), 0)
    gate = jnp.where(row < n_ctx, g_ref[1:2, :], g_ref[0:1, :])
    o_ref[...] = h_ref[...] + gate * jnp.dot(m_ref[...], w_ref[...], preferred_element_type=F32)


def _outproj(merged, w_out, h, mod, l, n_ctx):
    t, d = h.shape
    bm = OUT_ROW_BLOCK
    return pl.pallas_call(
        functools.partial(_outproj_kernel, n_ctx=n_ctx, bm=bm),
        grid=(t // bm,),
        in_specs=[pl.BlockSpec((bm, d), lambda i: (i, 0)),
                  pl.BlockSpec((None, d, d), lambda i: (l, 0, 0), pipeline_mode=pl.Buffered(1)),
                  pl.BlockSpec((bm, d), lambda i: (i, 0)),
                  pl.BlockSpec((None, 8, d), lambda i: (l, 0, 2))],
        out_specs=pl.BlockSpec((bm, d), lambda i: (i, 0)),
        out_shape=jax.ShapeDtypeStruct((t, d), F32),
        compiler_params=_params(("arbitrary",)),
        name="outproj",
    )(merged, w_out, h, mod)


def _pack_bf16_pairs(v):
    n = v.shape[1] // 2
    bits = lax.bitcast_convert_type(v.astype(BF16).astype(F32), jnp.uint32)
    return bits[:, :n] | (bits[:, n:] >> 16)


def _unpack_bf16_pairs(p):
    hi = lax.bitcast_convert_type(p & jnp.uint32(0xFFFF0000), F32)
    lo = lax.bitcast_convert_type(p << 16, F32)
    return jnp.concatenate([hi, lo], axis=1).astype(BF16)


def _router_kernel(h_ref, sh_ref, sc_ref, nw_ref, wr_ref, br_ref, xn_ref, meta_ref, cnt_ref,
                   carry_s, *, n_ctx, bm):
    i = pl.program_id(0)

    @pl.when(i == 0)
    def _():
        carry_s[...] = jnp.zeros_like(carry_s)

    xn = _modulated_norm(h_ref[...], nw_ref[...], sh_ref, sc_ref, i * bm, n_ctx)
    xn_ref[...] = _pack_bf16_pairs(xn)
    logits = _dot3(xn, wr_ref[...]) + br_ref[...]
    lane = lax.broadcasted_iota(jnp.int32, (bm, LANES), 1)
    far = jnp.int32(4 * LANES)

    glog = jnp.where(lane < N_GROUPS_E, logits, NEG_BIG)
    gmax = jnp.max(glog, axis=1, keepdims=True)
    gidx = jnp.min(jnp.where(glog == gmax, lane, far), axis=1, keepdims=True)
    p_top = 1.0 / jnp.sum(jnp.exp(glog - gmax), axis=1, keepdims=True)

    in_group = ((lane >= N_GROUPS_E) & (lane < N_GROUPS_E + N_EXPERTS)
                & (((lane - N_GROUPS_E) >> 3) == gidx))
    elog = jnp.where(in_group, logits, NEG_BIG)
    m1 = jnp.max(elog, axis=1, keepdims=True)
    i1 = jnp.min(jnp.where(elog == m1, lane, far), axis=1, keepdims=True)
    elog2 = jnp.where(lane == i1, NEG_BIG, elog)
    m2 = jnp.max(elog2, axis=1, keepdims=True)
    i2 = jnp.min(jnp.where(elog2 == m2, lane, far), axis=1, keepdims=True)
    e2 = jnp.exp(m2 - m1)
    w1 = p_top / (1.0 + e2)
    w2 = p_top * e2 / (1.0 + e2)
    ex1 = i1 - N_GROUPS_E
    ex2 = i2 - N_GROUPS_E

    onehot = jnp.where((lane == ex1) | (lane == ex2), 1.0, 0.0).astype(F32)
    ri = lax.broadcasted_iota(jnp.int32, (bm, bm), 0)
    ci = lax.broadcasted_iota(jnp.int32, (bm, bm), 1)
    earlier = jnp.where(ci < ri, 1.0, 0.0).astype(BF16)
    before = jnp.dot(earlier, onehot.astype(BF16), preferred_element_type=F32) + carry_s[0:1, :]
    r1 = jnp.sum(jnp.where(lane == ex1, before, 0.0), axis=1, keepdims=True)
    r2 = jnp.sum(jnp.where(lane == ex2, before, 0.0), axis=1, keepdims=True)
    carry = carry_s[...] + jnp.sum(onehot, axis=0, keepdims=True)
    carry_s[...] = carry
    cnt_ref[...] = carry

    meta = jnp.where(lane == 0, ex1.astype(F32), 0.0)
    meta = jnp.where(lane == 1, ex2.astype(F32), meta)
    meta = jnp.where(lane == 2, w1, meta)
    meta = jnp.where(lane == 3, w2, meta)
    meta = jnp.where(lane == 4, r1, meta)
    meta = jnp.where(lane == 5, r2, meta)
    meta_ref[...] = meta


def _router(h, mod, norm_w, w_route, b_route, l, n_ctx):
    t, d = h.shape
    bm = TOK_BLOCK
    return pl.pallas_call(
        functools.partial(_router_kernel, n_ctx=n_ctx, bm=bm),
        grid=(t // bm,),
        in_specs=[pl.BlockSpec((bm, d), lambda i: (i, 0)),
                  pl.BlockSpec((None, 8, d), lambda i: (l, 0, 3)),
                  pl.BlockSpec((None, 8, d), lambda i: (l, 0, 4)),
                  pl.BlockSpec((None, 1, d), lambda i: (l, 0, 0)),
                  pl.BlockSpec((None, d, LANES), lambda i: (l, 0, 0)),
                  pl.BlockSpec((None, 1, LANES), lambda i: (l, 0, 0))],
        out_specs=[pl.BlockSpec((bm, d // 2), lambda i: (i, 0)),
                   pl.BlockSpec((bm, LANES), lambda i: (i, 0)),
                   pl.BlockSpec((8, LANES), lambda i: (0, 0))],
        out_shape=[jax.ShapeDtypeStruct((t, d // 2), jnp.uint32),
                   jax.ShapeDtypeStruct((t, LANES), F32),
                   jax.ShapeDtypeStruct((8, LANES), F32)],
        scratch_shapes=[pltpu.VMEM((8, LANES), F32)],
        compiler_params=_params(("arbitrary",)),
        name="router",
    )(h, mod, mod, norm_w, w_route, b_route)


def _row_copy(src, src_row, dst, dst_row, sem):
    return pltpu.make_async_copy(src.at[pl.ds(src_row, 1), :], dst.at[pl.ds(dst_row, 1), :], sem)


def _dispatch_kernel(pos_ref, xn_ref, init_ref, o_ref, sem, *, bm):
    del init_ref
    base = pl.program_id(0) * bm

    def issue(r, carry):
        for k in range(2):
            _row_copy(xn_ref, r, o_ref, pos_ref[(base + r) * 2 + k], sem).start()
        return carry

    lax.fori_loop(0, bm, issue, 0, unroll=4)
    for k in range(2):
        pltpu.make_async_copy(xn_ref, o_ref.at[pl.ds(0, bm), :], sem).wait()


def _dispatch(pos, xn, init):
    t, d = xn.shape
    n_rows = init.shape[0]
    bm = TOK_BLOCK
    return pl.pallas_call(
        functools.partial(_dispatch_kernel, bm=bm),
        grid_spec=pltpu.PrefetchScalarGridSpec(
            num_scalar_prefetch=1,
            grid=(t // bm,),
            in_specs=[pl.BlockSpec((bm, d), lambda i, pos: (i, 0)),
                      pl.BlockSpec(memory_space=pl.ANY)],
            out_specs=pl.BlockSpec(memory_space=pl.ANY),
            scratch_shapes=[pltpu.SemaphoreType.DMA(())]),
        out_shape=jax.ShapeDtypeStruct((n_rows, d), xn.dtype),
        input_output_aliases={2: 0},
        compiler_params=_params(("arbitrary",)),
        name="dispatch",
    )(pos, xn, init)


def _expert_kernel(te_ref, nx_ref, xi_ref, first_ref, valid_ref, x_ref, wg_hbm, wu_hbm, wd_hbm,
                   o_ref, lg_s, lu_s, ld_s, wg_s, wu_s, wd_s, sem, *, l):
    i = pl.program_id(0)

    def fetch(e):
        return (pltpu.make_async_copy(wg_hbm.at[l, e], lg_s, sem.at[0]),
                pltpu.make_async_copy(wu_hbm.at[l, e], lu_s, sem.at[1]),
                pltpu.make_async_copy(wd_hbm.at[l, e], ld_s, sem.at[2]))

    @pl.when(i == 0)
    def _():
        for cp in fetch(te_ref[0]):
            cp.start()

    @pl.when(first_ref[i] == 1)
    def _():
        for cp in fetch(te_ref[i]):
            cp.wait()
        wg_s[...] = lg_s[...].astype(BF16)
        wu_s[...] = lu_s[...].astype(BF16)
        wd_s[...] = ld_s[...].astype(BF16)

        @pl.when(nx_ref[i] >= 0)
        def _():
            for cp in fetch(nx_ref[i]):
                cp.start()

    @pl.when(valid_ref[i] == 1)
    def _():
        x = _unpack_bf16_pairs(x_ref[...])
        gate = jnp.dot(x, wg_s[...], preferred_element_type=F32)
        up = jnp.dot(x, wu_s[...], preferred_element_type=F32)
        mid = (_silu(gate) * up).astype(BF16)
        o_ref[...] = jnp.dot(mid, wd_s[...], preferred_element_type=F32)

    @pl.when(valid_ref[i] == 0)
    def _():
        o_ref[...] = jnp.zeros_like(o_ref)


def _experts(tile_meta, x_sorted, w_gate, w_up, w_down, l, n_tiles):
    d = D_MODEL
    tm = EXPERT_TILE
    hbm = pl.BlockSpec(memory_space=pl.ANY)
    return pl.pallas_call(
        functools.partial(_expert_kernel, l=l),
        grid_spec=pltpu.PrefetchScalarGridSpec(
            num_scalar_prefetch=5,
            grid=(n_tiles,),
            in_specs=[pl.BlockSpec((tm, d // 2), lambda i, te, nx, xi, fi, va: (xi[i], 0)),
                      hbm, hbm, hbm],
            out_specs=pl.BlockSpec((tm, d), lambda i, te, nx, xi, fi, va: (i, 0)),
            scratch_shapes=[pltpu.VMEM((d, D_EXPERT), F32), pltpu.VMEM((d, D_EXPERT), F32),
                            pltpu.VMEM((D_EXPERT, d), F32),
                            pltpu.VMEM((d, D_EXPERT), BF16), pltpu.VMEM((d, D_EXPERT), BF16),
                            pltpu.VMEM((D_EXPERT, d), BF16),
                            pltpu.SemaphoreType.DMA((3,))]),
        out_shape=jax.ShapeDtypeStruct((n_tiles * tm, d), F32),
        compiler_params=_params(("arbitrary",)),
        name="experts",
    )(*tile_meta, x_sorted, w_gate, w_up, w_down)


def _combine_kernel(pos_ref, h_ref, meta_ref, g_ref, ys_ref, o_ref, buf_s, sem, *, n_ctx, bm):
    i = pl.program_id(0)
    n = pl.num_programs(0)

    def issue(blk, slot):
        base = blk * bm

        def body(r, carry):
            for k in range(2):
                _row_copy(ys_ref, pos_ref[(base + r) * 2 + k], buf_s.at[slot, k], r,
                          sem.at[slot]).start()
            return carry

        lax.fori_loop(0, bm, body, 0, unroll=4)

    @pl.when(i == 0)
    def _():
        issue(0, 0)

    @pl.when(i + 1 < n)
    def _():
        issue(i + 1, (i + 1) % 2)

    slot = i % 2
    for k in range(2):
        pltpu.make_async_copy(ys_ref.at[pl.ds(0, bm), :], buf_s.at[slot, k], sem.at[slot]).wait()
    meta = meta_ref[...]
    moe = meta[:, 2:3] * buf_s[slot, 0] + meta[:, 3:4] * buf_s[slot, 1]
    row = i * bm + lax.broadcasted_iota(jnp.int32, (bm, 1), 0)
    gate = jnp.where(row < n_ctx, g_ref[1:2, :], g_ref[0:1, :])
    o_ref[...] = h_ref[...] + gate * moe


def _combine(pos, h, meta, mod, y_sorted, l, n_ctx):
    t, d = h.shape
    bm = TOK_BLOCK
    return pl.pallas_call(
        functools.partial(_combine_kernel, n_ctx=n_ctx, bm=bm),
        grid_spec=pltpu.PrefetchScalarGridSpec(
            num_scalar_prefetch=1,
            grid=(t // bm,),
            in_specs=[pl.BlockSpec((bm, d), lambda i, pos: (i, 0)),
                      pl.BlockSpec((bm, LANES), lambda i, pos: (i, 0)),
                      pl.BlockSpec((None, 8, d), lambda i, pos: (l, 0, 5)),
                      pl.BlockSpec(memory_space=pl.ANY)],
            out_specs=pl.BlockSpec((bm, d), lambda i, pos: (i, 0)),
            scratch_shapes=[pltpu.VMEM((2, 2, bm, d), F32), pltpu.SemaphoreType.DMA((2,))]),
        out_shape=jax.ShapeDtypeStruct((t, d), F32),
        compiler_params=_params(("arbitrary",)),
        name="combine",
    )(pos, h, meta, mod, y_sorted)


def _tile_plan(meta, counts, n_tiles):
    tm = EXPERT_TILE
    cnt = counts[0, :N_EXPERTS].astype(jnp.int32)
    tiles_e = (cnt + tm - 1) // tm
    tile_end = jnp.cumsum(tiles_e)
    offset = (tile_end - tiles_e) * tm
    expert = meta[:, 0:2].astype(jnp.int32)
    rank = meta[:, 4:6].astype(jnp.int32)
    pos = (offset[expert] + rank).reshape(-1)
    n_used = tile_end[-1]
    tid = jnp.arange(n_tiles, dtype=jnp.int32)
    valid = tid < n_used
    tsrc = jnp.minimum(tid, n_used - 1)
    te = jnp.minimum(jnp.searchsorted(tile_end, tsrc, side="right"), N_EXPERTS - 1).astype(jnp.int32)
    first = (valid & ((tid == 0) | (te != jnp.roll(te, 1)))).astype(jnp.int32)
    after = tile_end[te]
    nxt = jnp.where(after < n_used, te[jnp.minimum(after, n_tiles - 1)], -1).astype(jnp.int32)
    return pos, (te, nxt, tsrc.astype(jnp.int32), first, valid.astype(jnp.int32))


def _final_kernel(h_ref, w_ref, o_ref):
    x = h_ref[...]
    ms = jnp.mean(x * x, axis=-1, keepdims=True)
    o_ref[...] = x * lax.rsqrt(ms + EPS) * w_ref[...]


def _final_norm(h, w, n_ctx):
    t, d = h.shape
    bm = TOK_BLOCK
    skip = n_ctx // bm
    return pl.pallas_call(
        _final_kernel,
        grid=((t - n_ctx) // bm,),
        in_specs=[pl.BlockSpec((bm, d), lambda i: (i + skip, 0)),
                  pl.BlockSpec((1, d), lambda i: (0, 0))],
        out_specs=pl.BlockSpec((bm, d), lambda i: (i, 0)),
        out_shape=jax.ShapeDtypeStruct((t - n_ctx, d), F32),
        compiler_params=_params(("arbitrary",)),
        name="final_norm",
    )(h, w)


def _pad_lanes(v):
    return jnp.pad(v, [(0, 0)] * (v.ndim - 1) + [(0, LANES - v.shape[-1])])


def _rope_tables(n_ctx, n_lat):
    rows = n_lat // GRID_W
    row = jnp.broadcast_to(jnp.arange(rows, dtype=F32)[:, None], (rows, GRID_W)).reshape(n_lat)
    col = jnp.broadcast_to(jnp.arange(GRID_W, dtype=F32)[None, :], (rows, GRID_W)).reshape(n_lat)
    inv_freq = ROPE_BASE ** (-jnp.arange(ROPE_FREQS, dtype=F32) / ROPE_FREQS)
    ang_r = row[:, None] * inv_freq[None, :]
    ang_c = col[:, None] * inv_freq[None, :]
    cos_l = jnp.concatenate([jnp.cos(ang_r)] * 2 + [jnp.cos(ang_c)] * 2, axis=1)
    sin_l = jnp.concatenate([-jnp.sin(ang_r), jnp.sin(ang_r), -jnp.sin(ang_c), jnp.sin(ang_c)], axis=1)
    cos_t = jnp.concatenate([jnp.ones((n_ctx, 2 * LANES), F32), cos_l], axis=0)
    sin_t = jnp.concatenate([jnp.zeros((n_ctx, 2 * LANES), F32), sin_l], axis=0)
    return cos_t, sin_t


def kernel(x, c, ctx, c_ctx, w_mod, b_mod, norm1_w, w_in, conv_w, conv_b, ssm_A_log, ssm_dt_bias,
           ssm_D, ssm_norm_w, ret_decay_logit, ret_norm_w, w_ssd_proj, w_ret_proj, w_out, norm2_w,
           w_route_group, b_route_group, w_route_expert, b_route_expert, w_exp_gate, w_exp_up,
           w_exp_down, final_norm_w):
    batch, n_lat, d = x.shape
    n_ctx = ctx.shape[1]
    depth = w_mod.shape[0]
    assert batch == 1 and d == D_MODEL and w_in.shape[-1] == MAIN_W + DT_W
    t = n_ctx + n_lat
    assert t % OUT_ROW_BLOCK == 0 and n_ctx % TOK_BLOCK == 0 and n_lat % TOK_BLOCK == 0

    w_nat, w_dt = _wprep(jnp.swapaxes(w_in, 1, 2))
    w_ssd_b = w_ssd_proj.astype(BF16)
    w_ret_b = w_ret_proj.astype(BF16)
    w_out_b = w_out.astype(BF16)
    w_route = _pad_lanes(jnp.concatenate([w_route_group, w_route_expert], axis=-1))
    b_route = _pad_lanes(jnp.concatenate([b_route_group, b_route_expert], axis=-1))[:, None, :]
    d_skip = jnp.repeat(ssm_D, SSM_HEADDIM, axis=-1)[:, None, :]
    dt_bias = _pad_lanes(ssm_dt_bias)
    a_log = _pad_lanes(ssm_A_log)
    ret_dl = _pad_lanes(ret_decay_logit)
    head_of_col = jnp.arange(D_MODEL, dtype=jnp.int32) // SSM_HEADDIM
    e_mat = (jnp.arange(LANES, dtype=jnp.int32)[:, None] == head_of_col[None, :]).astype(BF16)
    cos_t, sin_t = _rope_tables(n_ctx, n_lat)

    cvec = jnp.concatenate([c, c_ctx[None, :], jnp.zeros((6, d), F32)], axis=0)
    mod = _modulation(cvec, w_mod, b_mod)

    n_tiles = (2 * t) // EXPERT_TILE + N_EXPERTS
    h = jnp.concatenate([ctx[0], x[0]], axis=0)
    x_sorted = jnp.zeros((n_tiles * EXPERT_TILE, d // 2), jnp.uint32)
    for l in range(depth):
        p_main, dt_raw = _inproj(h, mod, norm1_w[:, None, :], w_nat, w_dt, cos_t, sin_t, l, n_ctx)
        xbc_act = _conv(p_main, conv_w, conv_b[:, None, :], l, n_ctx)
        y_ssd = _ssd_scan(xbc_act, dt_raw, dt_bias[l], a_log[l], e_mat, n_ctx)
        y_ret = _ret_scan(p_main, ret_dl[l], n_ctx)
        merged = _merge(y_ssd, xbc_act, p_main, y_ret, d_skip, ssm_norm_w[:, None, :],
                        ret_norm_w[:, None, :], w_ssd_b, w_ret_b, l)
        h = _outproj(merged, w_out_b, h, mod, l, n_ctx)
        xn, meta, counts = _router(h, mod, norm2_w[:, None, :], w_route, b_route, l, n_ctx)
        pos, tile_meta = _tile_plan(meta, counts, n_tiles)
        x_sorted = _dispatch(pos, xn, x_sorted)
        y_sorted = _experts(tile_meta, x_sorted, w_exp_gate, w_exp_up, w_exp_down, l, n_tiles)
        h = _combine(pos, h, meta, mod, y_sorted, l, n_ctx)

    return _final_norm(h, final_norm_w[None, :], n_ctx)[None]
```

```python
import functools

import jax
import jax.numpy as jnp
from jax import lax
from jax.experimental import pallas as pl
from jax.experimental.pallas import tpu as pltpu

F32 = jnp.float32
BF16 = jnp.bfloat16
HIGHEST = lax.Precision.HIGHEST

D_MODEL = 2048
GRID_W = 64
CHUNK = 128
EPS = 1e-6

SSM_HEADDIM = 64
SSM_HEADS = 32
SSM_GROUPS = 4
SSM_HPG = 8
SSM_STATE = 128
SSM_CONV = 5
GROUP_W = SSM_HPG * SSM_HEADDIM
BC_W = SSM_GROUPS * SSM_STATE
XBC_D = D_MODEL + 2 * BC_W

RET_HEADS = 8
RET_DK = 256
ROPE_FREQS = 64
ROPE_BASE = 10000.0

N_GROUPS_E = 4
EXPERTS_PER_GROUP = 8
N_EXPERTS = 32
D_EXPERT = 512

LANES = 128
MAIN_W = 7 * D_MODEL + XBC_D
XBC_COL0 = 7 * D_MODEL
DT_W = SSM_HEADS
DT_COL0 = D_MODEL + XBC_D

INPROJ_ROW_BLOCKS = (1056, 768)
COL_BLOCK = 1024
OUT_ROW_BLOCK = 384
MERGE_ROW_BLOCK = 256
MERGE_SUB_ROWS = 256
TOK_BLOCK = 256
EXPERT_TILE = 256
NEG_BIG = -1e30
VMEM_LIMIT = 56 * 1024 * 1024


def _params(sem, vmem=VMEM_LIMIT):
    return pltpu.CompilerParams(dimension_semantics=sem, vmem_limit_bytes=vmem)


def _sigmoid(v):
    return 0.5 * jnp.tanh(0.5 * v) + 0.5


def _silu(v):
    return v * _sigmoid(v)


def _softplus(v):
    return jnp.maximum(v, 0.0) + jnp.log1p(jnp.exp(-jnp.abs(v)))


def _split_bf16(v):
    hi = v.astype(BF16)
    return hi, (v - hi.astype(F32)).astype(BF16)


def _dot3(x, w):
    n = w.shape[1]
    xh, xl = _split_bf16(x)
    wh, wl = _split_bf16(w)
    both = jnp.dot(xh, jnp.concatenate([wh, wl], axis=1), preferred_element_type=F32)
    return both[:, :n] + both[:, n:] + jnp.dot(xl, wh, preferred_element_type=F32)


def _modulated_norm(x, nw, sh_ref, sc_ref, row0, n_ctx):
    ms = jnp.mean(x * x, axis=-1, keepdims=True)
    y = x * lax.rsqrt(ms + EPS) * nw
    row = row0 + lax.broadcasted_iota(jnp.int32, (x.shape[0], 1), 0)
    is_ctx = row < n_ctx
    sc = jnp.where(is_ctx, sc_ref[1:2, :], sc_ref[0:1, :])
    sh = jnp.where(is_ctx, sh_ref[1:2, :], sh_ref[0:1, :])
    return y * (1.0 + sc) + sh


def _mod_kernel(a_ref, w_ref, b_ref, o_ref):
    a = a_ref[...]
    o_ref[...] = jnp.dot(_silu(a), w_ref[...], precision=HIGHEST,
                         preferred_element_type=F32) + b_ref[...]


def _modulation(cvec, w_mod, b_mod):
    depth, d, n = w_mod.shape
    bn = 1024
    return pl.pallas_call(
        _mod_kernel,
        grid=(depth, n // bn),
        in_specs=[pl.BlockSpec((8, d), lambda l, j: (0, 0)),
                  pl.BlockSpec((None, d, bn), lambda l, j: (l, 0, j)),
                  pl.BlockSpec((None, 1, bn), lambda l, j: (l, 0, j))],
        out_specs=pl.BlockSpec((None, 8, bn), lambda l, j: (l, 0, j)),
        out_shape=jax.ShapeDtypeStruct((depth, 8, n), F32),
        compiler_params=_params(("arbitrary", "arbitrary")),
        name="modulation",
    )(cvec, w_mod, b_mod.reshape(depth, 1, n))


def _dot3_nt(x, wt):
    nt = (((1,), (1,)), ((), ()))
    n = wt.shape[0]
    xh, xl = _split_bf16(x)
    wh, wl = _split_bf16(wt)
    both = lax.dot_general(xh, jnp.concatenate([wh, wl], axis=0), nt, preferred_element_type=F32)
    return both[:, :n] + both[:, n:] + lax.dot_general(xl, wh, nt, preferred_element_type=F32)


def _wprep_kernel(w_hbm, o_ref, dt_ref, buf_s, dtb_s, sem, dsem, *, bn, n_aligned, nj):
    l = pl.program_id(0)
    n = pl.program_id(1)
    step = l * nj + n
    total = pl.num_programs(0) * nj

    def fetch(ll, nn, slot):
        row0 = pl.multiple_of(nn * bn + jnp.where(nn >= n_aligned, DT_W, 0), 8)
        return pltpu.make_async_copy(w_hbm.at[ll, pl.ds(row0, bn), :], buf_s.at[slot], sem.at[slot])

    @pl.when(step == 0)
    def _():
        fetch(l, n, 0).start()

    @pl.when(step + 1 < total)
    def _():
        wrap = n + 1 == nj
        fetch(jnp.where(wrap, l + 1, l), jnp.where(wrap, 0, n + 1), (step + 1) % 2).start()

    fetch(l, n, step % 2).wait()
    o_ref[...] = buf_s[step % 2].astype(BF16)

    @pl.when(n == 0)
    def _():
        cp = pltpu.make_async_copy(w_hbm.at[l, pl.ds(DT_COL0, DT_W), :], dtb_s, dsem)
        cp.start()
        cp.wait()
        dt_ref[0:DT_W, :] = dtb_s[...]
        dt_ref[DT_W:, :] = jnp.zeros((LANES - DT_W, dt_ref.shape[1]), F32)


def _wprep(w_in_t):
    depth, _, d = w_in_t.shape
    bn = COL_BLOCK
    nj = MAIN_W // bn
    return pl.pallas_call(
        functools.partial(_wprep_kernel, bn=bn, n_aligned=DT_COL0 // bn, nj=nj),
        grid=(depth, nj),
        in_specs=[pl.BlockSpec(memory_space=pl.ANY)],
        out_specs=[pl.BlockSpec((None, bn, d), lambda l, n: (l, n, 0)),
                   pl.BlockSpec((None, LANES, d), lambda l, n: (l, 0, 0))],
        out_shape=[jax.ShapeDtypeStruct((depth, MAIN_W, d), BF16),
                   jax.ShapeDtypeStruct((depth, LANES, d), F32)],
        scratch_shapes=[pltpu.VMEM((2, bn, d), F32), pltpu.VMEM((DT_W, d), F32),
                        pltpu.SemaphoreType.DMA((2,)), pltpu.SemaphoreType.DMA(())],
        compiler_params=_params(("arbitrary", "arbitrary")),
        name="wprep",
    )(w_in_t)


def _inproj_kernel(h_ref, sh_ref, sc_ref, nw_ref, w_ref, wdt_ref, cos_ref, sin_ref,
                   o_ref, dt_ref, xn_s, *, n_ctx, bm, bn):
    i = pl.program_id(0)
    j = pl.program_id(1)

    @pl.when(j == 0)
    def _():
        xn = _modulated_norm(h_ref[...], nw_ref[...], sh_ref, sc_ref, i * bm, n_ctx)
        xn_s[...] = xn.astype(BF16)
        dt_ref[...] = _dot3_nt(xn, wdt_ref[...])

    acc = lax.dot_general(xn_s[...], w_ref[...], (((1,), (1,)), ((), ())),
                          preferred_element_type=F32)
    q_lo = DT_COL0 // bn
    k_lo = (DT_COL0 + D_MODEL) // bn
    k_hi = (DT_COL0 + 2 * D_MODEL) // bn
    is_rope = (j >= q_lo) & (j < k_hi)

    @pl.when(is_rope)
    def _():
        scale = jnp.where(j >= k_lo, RET_DK ** -0.5, 1.0).astype(F32)
        for gi in range(bn // LANES):
            u = acc[:, gi * LANES:(gi + 1) * LANES]
            half = gi % 2
            c = cos_ref[:, half * LANES:(half + 1) * LANES]
            s = sin_ref[:, half * LANES:(half + 1) * LANES]
            r = (u * c + pltpu.roll(u, LANES // 2, 1) * s) * scale
            o_ref[:, gi * LANES:(gi + 1) * LANES] = r.astype(BF16)

    @pl.when(jnp.logical_not(is_rope))
    def _():
        o_ref[...] = acc.astype(BF16)


def _inproj(h, mod, norm_w, w_nat, w_dt, cos_t, sin_t, l, n_ctx):
    t, d = h.shape
    bm = next(b for b in INPROJ_ROW_BLOCKS if t % b == 0)
    bn = COL_BLOCK
    nj = MAIN_W // bn
    z_blocks = D_MODEL // bn
    xbc_blocks = XBC_D // bn

    def out_block(j):
        return jnp.where(j < z_blocks, j,
                         jnp.where(j < z_blocks + xbc_blocks, j + (nj - z_blocks - xbc_blocks),
                                   j - xbc_blocks))

    once = pl.Buffered(1)
    kern = functools.partial(_inproj_kernel, n_ctx=n_ctx, bm=bm, bn=bn)
    return pl.pallas_call(
        kern,
        grid=(t // bm, nj),
        in_specs=[pl.BlockSpec((bm, d), lambda i, j: (i, 0), pipeline_mode=once),
                  pl.BlockSpec((None, 8, d), lambda i, j: (l, 0, 0)),
                  pl.BlockSpec((None, 8, d), lambda i, j: (l, 0, 1)),
                  pl.BlockSpec((None, 1, d), lambda i, j: (l, 0, 0)),
                  pl.BlockSpec((None, bn, d), lambda i, j: (l, j, 0)),
                  pl.BlockSpec((None, LANES, d), lambda i, j: (l, 0, 0)),
                  pl.BlockSpec((bm, 2 * LANES), lambda i, j: (i, 0), pipeline_mode=once),
                  pl.BlockSpec((bm, 2 * LANES), lambda i, j: (i, 0), pipeline_mode=once)],
        out_specs=[pl.BlockSpec((bm, bn), lambda i, j: (i, out_block(j))),
                   pl.BlockSpec((bm, LANES), lambda i, j: (i, 0))],
        out_shape=[jax.ShapeDtypeStruct((t, MAIN_W), BF16),
                   jax.ShapeDtypeStruct((t, LANES), F32)],
        scratch_shapes=[pltpu.VMEM((bm, d), BF16)],
        compiler_params=_params(("arbitrary", "arbitrary")),
        name="inproj",
    )(h, mod, mod, norm_w, w_nat, w_dt, cos_t, sin_t)


def _conv_kernel(x_ref, w_ref, b_ref, o_ref, pad_s, *, n_ctx, n_lat, rows):
    halo = 8
    zeros = jnp.zeros((halo, LANES), F32)
    c0 = halo
    x0 = 2 * halo + n_ctx
    pad_s[0:halo, :] = zeros
    pad_s[c0:c0 + n_ctx, :] = x_ref[0:n_ctx, :].astype(F32)
    pad_s[c0 + n_ctx:x0, :] = zeros
    pad_s[x0:x0 + n_lat, :] = x_ref[n_ctx:n_ctx + n_lat, :].astype(F32)
    pad_s[x0 + n_lat:x0 + n_lat + halo, :] = zeros
    w = w_ref[...]
    b = b_ref[...]

    def segment(src0, dst0, n):
        def body(s, carry):
            r0 = pl.multiple_of(s * rows, rows)
            acc = jnp.broadcast_to(b, (rows, LANES))
            for kk in range(SSM_CONV):
                acc = acc + w[kk:kk + 1, :] * pad_s[pl.ds(src0 + r0 + kk - SSM_CONV // 2, rows), :]
            o_ref[pl.ds(dst0 + r0, rows), :] = _silu(acc).astype(BF16)
            return carry
        lax.fori_loop(0, n // rows, body, 0)

    segment(c0, 0, n_ctx)
    segment(x0, n_ctx, n_lat)


def _conv(p_main, conv_w, conv_b, l, n_ctx):
    t = p_main.shape[0]
    n_lat = t - n_ctx
    col0 = XBC_COL0 // LANES
    kern = functools.partial(_conv_kernel, n_ctx=n_ctx, n_lat=n_lat, rows=256)
    return pl.pallas_call(
        kern,
        grid=(XBC_D // LANES,),
        in_specs=[pl.BlockSpec((t, LANES), lambda c: (0, col0 + c)),
                  pl.BlockSpec((None, SSM_CONV, LANES), lambda c: (l, 0, c)),
                  pl.BlockSpec((None, 1, LANES), lambda c: (l, 0, c))],
        out_specs=pl.BlockSpec((t, LANES), lambda c: (0, c)),
        out_shape=jax.ShapeDtypeStruct((t, XBC_D), BF16),
        scratch_shapes=[pltpu.VMEM((t + 24, LANES), F32)],
        compiler_params=_params(("arbitrary",)),
        name="conv_silu",
    )(p_main, conv_w, conv_b)


def _chunk_index(s, nc_ctx, nc, rev):
    if not rev:
        return s
    return jnp.where(s < nc_ctx, nc_ctx - 1 - s, nc - 1 - (s - nc_ctx))


def _split_dot(v, e2):
    hi, lo = _split_bf16(v)
    return jnp.dot(jnp.concatenate([hi, lo], axis=1), e2, preferred_element_type=F32)


def _ssd_kernel(xs_f, b_f, c_f, dt_f, xs_b, b_b, c_b, dt_b, bias_ref, alog_ref, e_ref,
                of_ref, ob_ref, h_s):
    @pl.when(pl.program_id(0) == 0)
    def _():
        h_s[...] = jnp.zeros_like(h_s)

    e = e_ref[...]
    _ssd_chunk(xs_f, b_f, c_f, dt_f, bias_ref[0:1, :], alog_ref[0:1, :], e, of_ref, h_s.at[0], False)
    _ssd_chunk(xs_b, b_b, c_b, dt_b, bias_ref[1:2, :], alog_ref[1:2, :], e, ob_ref, h_s.at[1], True)


def _ssd_chunk(xs_ref, b_ref, c_ref, dt_ref, bias, alog, e, o_ref, h_s, rev):
    q = CHUNK
    ii = lax.broadcasted_iota(jnp.int32, (q, q), 0)
    jj = lax.broadcasted_iota(jnp.int32, (q, q), 1)
    seen = (jj >= ii) if rev else (jj <= ii)
    tri = jnp.where(seen, 1.0, 0.0).astype(F32)

    dt = _softplus(dt_ref[...] + bias)
    a = dt * (-jnp.exp(alog))
    cum = jnp.dot(tri, a, precision=HIGHEST, preferred_element_type=F32)
    cum_t = cum.T
    dt_t = dt.T
    total = cum[0:1, :] if rev else cum[q - 1:q, :]
    dfs = jnp.exp(cum)
    w_end = dt * jnp.exp(total - cum)
    cdec = jnp.broadcast_to(jnp.exp(total), (8, LANES))
    expanded = _split_dot(jnp.concatenate([w_end, dfs, cdec], axis=0), e)
    w_end_x = expanded[0:q]
    dfs_x = expanded[q:2 * q]
    cdec_x = expanded[2 * q:2 * q + 1]

    xs = xs_ref[...]
    xw = (xs.astype(F32) * w_end_x).astype(BF16)
    first_head = jj < SSM_HEADDIM

    for g in range(SSM_GROUPS):
        cg = c_ref[:, g * SSM_STATE:(g + 1) * SSM_STATE]
        bg = b_ref[:, g * SSM_STATE:(g + 1) * SSM_STATE]
        gs = slice(g * GROUP_W, (g + 1) * GROUP_W)
        cb = lax.dot_general(cg, bg, (((1,), (1,)), ((), ())), preferred_element_type=F32)
        h_prev = h_s[:, gs]
        y_off = jnp.dot(cg, h_prev.astype(BF16), preferred_element_type=F32) * dfs_x[:, gs]
        new = lax.dot_general(bg, xw[:, gs], (((0,), (0,)), ((), ())), preferred_element_type=F32)
        pieces = []
        for hh in range(0, SSM_HPG, 2):
            ms = []
            for hd in (g * SSM_HPG + hh, g * SSM_HPG + hh + 1):
                seg = cum[:, hd:hd + 1] - cum_t[hd:hd + 1, :]
                m = jnp.exp(jnp.where(seen, seg, NEG_BIG)) * cb * dt_t[hd:hd + 1, :]
                ms.append(m.astype(BF16))
            c0 = (g * SSM_HPG + hh) * SSM_HEADDIM
            xp = xs[:, c0:c0 + LANES]
            zero = jnp.zeros_like(xp)
            rhs = jnp.concatenate([jnp.where(first_head, xp, zero), jnp.where(first_head, zero, xp)],
                                  axis=0)
            pieces.append(jnp.dot(jnp.concatenate(ms, axis=1), rhs, preferred_element_type=F32))
        y = jnp.concatenate(pieces, axis=1) + y_off
        o_ref[:, gs] = y.astype(o_ref.dtype)
        h_s[:, gs] = h_prev * cdec_x[:, gs] + new


def _scan_specs(n_ctx, t, col_blocks, width):
    nc, nc_ctx = t // CHUNK, n_ctx // CHUNK
    specs = []
    for rev in (False, True):
        cidx = functools.partial(_chunk_index, nc_ctx=nc_ctx, nc=nc, rev=rev)
        specs.append([pl.BlockSpec((CHUNK, w), lambda s, c=c, cidx=cidx: (cidx(s), c))
                      for c, w in zip(col_blocks, width)])
    return specs


def _ssd_scan(xbc_act, dt_raw, dt_bias, a_log, e_mat, n_ctx):
    t = xbc_act.shape[0]
    bc0 = D_MODEL // BC_W
    fwd, bwd = _scan_specs(n_ctx, t, (0, bc0, bc0 + 1, 0), (D_MODEL, BC_W, BC_W, LANES))
    out_f, out_b = _scan_specs(n_ctx, t, (0,), (D_MODEL,))
    const = [pl.BlockSpec((2, LANES), lambda s: (0, 0)),
             pl.BlockSpec((2, LANES), lambda s: (0, 0)),
             pl.BlockSpec((2 * LANES, D_MODEL), lambda s: (0, 0))]
    seqs = (xbc_act, xbc_act, xbc_act, dt_raw)
    return pl.pallas_call(
        _ssd_kernel,
        grid=(t // CHUNK,),
        in_specs=fwd + bwd + const,
        out_specs=out_f + out_b,
        out_shape=[jax.ShapeDtypeStruct((t, D_MODEL), BF16)] * 2,
        scratch_shapes=[pltpu.VMEM((2, SSM_STATE, D_MODEL), F32)],
        compiler_params=_params(("arbitrary",)),
        name="ssd_scan",
    )(*seqs, *seqs, dt_bias, a_log, e_mat)


def _ret_kernel(q_f, k_f, v_f, q_b, k_b, v_b, dl_ref, of_ref, ob_ref, r_s, dm_s, kd_s, qd_s, cd_s):
    qn = CHUNK

    @pl.when(pl.program_id(0) == 0)
    def _():
        lg_both = -_softplus(-dl_ref[...])
        cd_s[...] = jnp.exp(float(qn) * lg_both)
        r_s[...] = jnp.zeros_like(r_s)
        ii = lax.broadcasted_iota(jnp.int32, (qn, qn), 0)
        jj = lax.broadcasted_iota(jnp.int32, (qn, qn), 1)
        for di, rev in enumerate((False, True)):
            diff = (jj - ii) if rev else (ii - jj)
            dpos = jnp.maximum(diff, 0).astype(F32)
            k_exp = (ii if rev else (qn - 1 - ii)).astype(F32)
            q_exp = ((qn - ii) if rev else (ii + 1)).astype(F32)
            for hd in range(RET_HEADS):
                lg = lg_both[di:di + 1, hd:hd + 1]
                dm_s[di, hd] = jnp.where(diff >= 0, jnp.exp(dpos * lg), 0.0)
                kd_s[di, hd] = jnp.exp(k_exp * lg)
                qd_s[di, hd] = jnp.exp(q_exp * lg)

    _ret_chunk(q_f, k_f, v_f, cd_s[0:1, :], of_ref, r_s.at[0], dm_s.at[0], kd_s.at[0], qd_s.at[0])
    _ret_chunk(q_b, k_b, v_b, cd_s[1:2, :], ob_ref, r_s.at[1], dm_s.at[1], kd_s.at[1], qd_s.at[1])


def _ret_chunk(q_ref, k_ref, v_ref, chunk_decay, o_ref, r_s, dm_s, kd_s, qd_s):
    for hd in range(RET_HEADS):
        hs = slice(hd * RET_DK, (hd + 1) * RET_DK)
        qh = q_ref[:, hs]
        kh = k_ref[:, hs]
        vh = v_ref[:, hs]
        sc = lax.dot_general(qh, kh, (((1,), (1,)), ((), ())), preferred_element_type=F32)
        y = jnp.dot((sc * dm_s[hd]).astype(BF16), vh, preferred_element_type=F32)
        r_prev = r_s[hd]
        qd = qd_s[hd]
        y_int = jnp.dot(qh, r_prev.astype(BF16), preferred_element_type=F32)
        y = y + y_int * jnp.concatenate([qd, qd], axis=1)
        kd = kd_s[hd]
        kdec = (kh.astype(F32) * jnp.concatenate([kd, kd], axis=1)).astype(BF16)
        kv = lax.dot_general(kdec, vh, (((0,), (0,)), ((), ())), preferred_element_type=F32)
        r_s[hd] = r_prev * chunk_decay[:, hd:hd + 1] + kv
        o_ref[:, hs] = y.astype(o_ref.dtype)


def _ret_scan(p_main, decay_logit, n_ctx):
    t = p_main.shape[0]
    fwd, bwd = _scan_specs(n_ctx, t, (1, 2, 3), (D_MODEL,) * 3)
    out_f, out_b = _scan_specs(n_ctx, t, (0,), (D_MODEL,))
    seqs = (p_main, p_main, p_main)
    decay_tables = pltpu.VMEM((2, RET_HEADS, CHUNK, CHUNK), F32)
    return pl.pallas_call(
        _ret_kernel,
        grid=(t // CHUNK,),
        in_specs=fwd + bwd + [pl.BlockSpec((2, LANES), lambda s: (0, 0))],
        out_specs=out_f + out_b,
        out_shape=[jax.ShapeDtypeStruct((t, D_MODEL), BF16)] * 2,
        scratch_shapes=[pltpu.VMEM((2, RET_HEADS, RET_DK, RET_DK), F32),
                        decay_tables, decay_tables, decay_tables, pltpu.VMEM((2, LANES), F32)],
        compiler_params=_params(("arbitrary",)),
        name="ret_scan",
    )(*seqs, *seqs, decay_logit)


def _merge_kernel(ysf_ref, ysb_ref, xs_ref, z_ref, yrf_ref, yrb_ref, g_ref, gs_ref, gr_ref,
                  dsk_ref, snw_ref, rnw_ref, ws_ref, wr_ref, o_ref, *, sub):
    for r0 in range(0, o_ref.shape[0], sub):
        rows = slice(r0, r0 + sub)
        y = (ysf_ref[rows, :].astype(F32) + ysb_ref[rows, :].astype(F32)
             + dsk_ref[...] * xs_ref[rows, :].astype(F32))
        y = y * _silu(z_ref[rows, :].astype(F32))
        ms = jnp.mean(y * y, axis=-1, keepdims=True)
        a = (y * lax.rsqrt(ms + EPS) * snw_ref[...]).astype(BF16)
        merged = _sigmoid(gs_ref[rows, :].astype(F32)) * jnp.dot(a, ws_ref[...],
                                                                 preferred_element_type=F32)
        gate = _silu(g_ref[rows, :].astype(F32)) * rnw_ref[...]
        parts = []
        for hd in range(RET_HEADS):
            hs = slice(hd * RET_DK, (hd + 1) * RET_DK)
            r = yrf_ref[rows, hs].astype(F32) + yrb_ref[rows, hs].astype(F32)
            ms = jnp.mean(r * r, axis=-1, keepdims=True)
            parts.append((r * lax.rsqrt(ms + EPS) * gate[:, hs]).astype(BF16))
        b = jnp.concatenate(parts, axis=1)
        merged = merged + _sigmoid(gr_ref[rows, :].astype(F32)) * jnp.dot(
            b, wr_ref[...], preferred_element_type=F32)
        o_ref[rows, :] = merged.astype(BF16)


def _merge(y_ssd, xbc_act, p_main, y_ret, d_skip, ssm_norm_w, ret_norm_w, w_ssd, w_ret, l):
    t, d = xbc_act.shape[0], D_MODEL
    bm = MERGE_ROW_BLOCK
    once = pl.Buffered(1)
    return pl.pallas_call(
        functools.partial(_merge_kernel, sub=MERGE_SUB_ROWS),
        grid=(t // bm,),
        in_specs=[pl.BlockSpec((bm, d), lambda i: (i, 0)),
                  pl.BlockSpec((bm, d), lambda i: (i, 0)),
                  pl.BlockSpec((bm, d), lambda i: (i, 0)),
                  pl.BlockSpec((bm, d), lambda i: (i, 0)),
                  pl.BlockSpec((bm, d), lambda i: (i, 0)),
                  pl.BlockSpec((bm, d), lambda i: (i, 0)),
                  pl.BlockSpec((bm, d), lambda i: (i, 4)),
                  pl.BlockSpec((bm, d), lambda i: (i, 5)),
                  pl.BlockSpec((bm, d), lambda i: (i, 6)),
                  pl.BlockSpec((None, 1, d), lambda i: (l, 0, 0)),
                  pl.BlockSpec((None, 1, d), lambda i: (l, 0, 0)),
                  pl.BlockSpec((None, 1, d), lambda i: (l, 0, 0)),
                  pl.BlockSpec((None, d, d), lambda i: (l, 0, 0), pipeline_mode=once),
                  pl.BlockSpec((None, d, d), lambda i: (l, 0, 0), pipeline_mode=once)],
        out_specs=pl.BlockSpec((bm, d), lambda i: (i, 0)),
        out_shape=jax.ShapeDtypeStruct((t, d), BF16),
        compiler_params=_params(("arbitrary",)),
        name="merge",
    )(*y_ssd, xbc_act, p_main, *y_ret, p_main, p_main, p_main, d_skip, ssm_norm_w, ret_norm_w,
      w_ssd, w_ret)


def _outproj_kernel(m_ref, w_ref, h_ref, g_ref, o_ref, *, n_ctx, bm):
    i = pl.program_id(0)
    row = i * bm + lax.broadcasted_iota(jnp.int32, (bm, 1), 0)
    gate = jnp.where(row < n_ctx, g_ref[1:2, :], g_ref[0:1, :])
    o_ref[...] = h_ref[...] + gate * jnp.dot(m_ref[...], w_ref[...], preferred_element_type=F32)


def _outproj(merged, w_out, h, mod, l, n_ctx):
    t, d = h.shape
    bm = OUT_ROW_BLOCK
    return pl.pallas_call(
        functools.partial(_outproj_kernel, n_ctx=n_ctx, bm=bm),
        grid=(t // bm,),
        in_specs=[pl.BlockSpec((bm, d), lambda i: (i, 0)),
                  pl.BlockSpec((None, d, d), lambda i: (l, 0, 0), pipeline_mode=pl.Buffered(1)),
                  pl.BlockSpec((bm, d), lambda i: (i, 0)),
                  pl.BlockSpec((None, 8, d), lambda i: (l, 0, 2))],
        out_specs=pl.BlockSpec((bm, d), lambda i: (i, 0)),
        out_shape=jax.ShapeDtypeStruct((t, d), F32),
        compiler_params=_params(("arbitrary",)),
        name="outproj",
    )(merged, w_out, h, mod)


def _pack_bf16_pairs(v):
    n = v.shape[1] // 2
    bits = lax.bitcast_convert_type(v.astype(BF16).astype(F32), jnp.uint32)
    return bits[:, :n] | (bits[:, n:] >> 16)


def _unpack_bf16_pairs(p):
    hi = lax.bitcast_convert_type(p & jnp.uint32(0xFFFF0000), F32)
    lo = lax.bitcast_convert_type(p << 16, F32)
    return jnp.concatenate([hi, lo], axis=1).astype(BF16)


def _router_kernel(h_ref, sh_ref, sc_ref, nw_ref, wr_ref, br_ref, xn_ref, meta_ref, cnt_ref,
                   carry_s, *, n_ctx, bm):
    i = pl.program_id(0)

    @pl.when(i == 0)
    def _():
        carry_s[...] = jnp.zeros_like(carry_s)

    xn = _modulated_norm(h_ref[...], nw_ref[...], sh_ref, sc_ref, i * bm, n_ctx)
    xn_ref[...] = _pack_bf16_pairs(xn)
    logits = _dot3(xn, wr_ref[...]) + br_ref[...]
    lane = lax.broadcasted_iota(jnp.int32, (bm, LANES), 1)
    far = jnp.int32(4 * LANES)

    glog = jnp.where(lane < N_GROUPS_E, logits, NEG_BIG)
    gmax = jnp.max(glog, axis=1, keepdims=True)
    gidx = jnp.min(jnp.where(glog == gmax, lane, far), axis=1, keepdims=True)
    p_top = 1.0 / jnp.sum(jnp.exp(glog - gmax), axis=1, keepdims=True)

    in_group = ((lane >= N_GROUPS_E) & (lane < N_GROUPS_E + N_EXPERTS)
                & (((lane - N_GROUPS_E) >> 3) == gidx))
    elog = jnp.where(in_group, logits, NEG_BIG)
    m1 = jnp.max(elog, axis=1, keepdims=True)
    i1 = jnp.min(jnp.where(elog == m1, lane, far), axis=1, keepdims=True)
    elog2 = jnp.where(lane == i1, NEG_BIG, elog)
    m2 = jnp.max(elog2, axis=1, keepdims=True)
    i2 = jnp.min(jnp.where(elog2 == m2, lane, far), axis=1, keepdims=True)
    e2 = jnp.exp(m2 - m1)
    w1 = p_top / (1.0 + e2)
    w2 = p_top * e2 / (1.0 + e2)
    ex1 = i1 - N_GROUPS_E
    ex2 = i2 - N_GROUPS_E

    onehot = jnp.where((lane == ex1) | (lane == ex2), 1.0, 0.0).astype(F32)
    ri = lax.broadcasted_iota(jnp.int32, (bm, bm), 0)
    ci = lax.broadcasted_iota(jnp.int32, (bm, bm), 1)
    earlier = jnp.where(ci < ri, 1.0, 0.0).astype(BF16)
    before = jnp.dot(earlier, onehot.astype(BF16), preferred_element_type=F32) + carry_s[0:1, :]
    r1 = jnp.sum(jnp.where(lane == ex1, before, 0.0), axis=1, keepdims=True)
    r2 = jnp.sum(jnp.where(lane == ex2, before, 0.0), axis=1, keepdims=True)
    carry = carry_s[...] + jnp.sum(onehot, axis=0, keepdims=True)
    carry_s[...] = carry
    cnt_ref[...] = carry

    meta = jnp.where(lane == 0, ex1.astype(F32), 0.0)
    meta = jnp.where(lane == 1, ex2.astype(F32), meta)
    meta = jnp.where(lane == 2, w1, meta)
    meta = jnp.where(lane == 3, w2, meta)
    meta = jnp.where(lane == 4, r1, meta)
    meta = jnp.where(lane == 5, r2, meta)
    meta_ref[...] = meta


def _router(h, mod, norm_w, w_route, b_route, l, n_ctx):
    t, d = h.shape
    bm = TOK_BLOCK
    return pl.pallas_call(
        functools.partial(_router_kernel, n_ctx=n_ctx, bm=bm),
        grid=(t // bm,),
        in_specs=[pl.BlockSpec((bm, d), lambda i: (i, 0)),
                  pl.BlockSpec((None, 8, d), lambda i: (l, 0, 3)),
                  pl.BlockSpec((None, 8, d), lambda i: (l, 0, 4)),
                  pl.BlockSpec((None, 1, d), lambda i: (l, 0, 0)),
                  pl.BlockSpec((None, d, LANES), lambda i: (l, 0, 0)),
                  pl.BlockSpec((None, 1, LANES), lambda i: (l, 0, 0))],
        out_specs=[pl.BlockSpec((bm, d // 2), lambda i: (i, 0)),
                   pl.BlockSpec((bm, LANES), lambda i: (i, 0)),
                   pl.BlockSpec((8, LANES), lambda i: (0, 0))],
        out_shape=[jax.ShapeDtypeStruct((t, d // 2), jnp.uint32),
                   jax.ShapeDtypeStruct((t, LANES), F32),
                   jax.ShapeDtypeStruct((8, LANES), F32)],
        scratch_shapes=[pltpu.VMEM((8, LANES), F32)],
        compiler_params=_params(("arbitrary",)),
        name="router",
    )(h, mod, mod, norm_w, w_route, b_route)


def _row_copy(src, src_row, dst, dst_row, sem):
    return pltpu.make_async_copy(src.at[pl.ds(src_row, 1), :], dst.at[pl.ds(dst_row, 1), :], sem)


def _dispatch_kernel(pos_ref, xn_ref, init_ref, o_ref, sem, *, bm, t):
    del init_ref
    base = pl.program_id(0) * bm

    def issue(r, carry):
        for k in range(2):
            _row_copy(xn_ref, r, o_ref, pos_ref[k * t + base + r], sem).start(priority=k)
        return carry

    lax.fori_loop(0, bm, issue, 0, unroll=4)
    for k in range(2):
        pltpu.make_async_copy(xn_ref, o_ref.at[pl.ds(0, bm), :], sem).wait()


def _dispatch(pos, xn, init):
    t, d = xn.shape
    n_rows = init.shape[0]
    bm = TOK_BLOCK
    return pl.pallas_call(
        functools.partial(_dispatch_kernel, bm=bm, t=t),
        grid_spec=pltpu.PrefetchScalarGridSpec(
            num_scalar_prefetch=1,
            grid=(t // bm,),
            in_specs=[pl.BlockSpec((bm, d), lambda i, pos: (i, 0)),
                      pl.BlockSpec(memory_space=pl.ANY)],
            out_specs=pl.BlockSpec(memory_space=pl.ANY),
            scratch_shapes=[pltpu.SemaphoreType.DMA(())]),
        out_shape=jax.ShapeDtypeStruct((n_rows, d), xn.dtype),
        input_output_aliases={2: 0},
        compiler_params=_params(("arbitrary",)),
        name="dispatch",
    )(pos, xn, init)


def _expert_kernel(te_ref, nx_ref, xi_ref, first_ref, valid_ref, x_ref, wg_hbm, wu_hbm, wd_hbm,
                   o_ref, lg_s, lu_s, ld_s, wg_s, wu_s, wd_s, sem, *, l):
    i = pl.program_id(0)

    def fetch(e):
        return (pltpu.make_async_copy(wg_hbm.at[l, e], lg_s, sem.at[0]),
                pltpu.make_async_copy(wu_hbm.at[l, e], lu_s, sem.at[1]),
                pltpu.make_async_copy(wd_hbm.at[l, e], ld_s, sem.at[2]))

    @pl.when(i == 0)
    def _():
        for cp in fetch(te_ref[0]):
            cp.start()

    @pl.when(first_ref[i] == 1)
    def _():
        for cp in fetch(te_ref[i]):
            cp.wait()
        wg_s[...] = lg_s[...].astype(BF16)
        wu_s[...] = lu_s[...].astype(BF16)
        wd_s[...] = ld_s[...].astype(BF16)

        @pl.when(nx_ref[i] >= 0)
        def _():
            for cp in fetch(nx_ref[i]):
                cp.start()

    @pl.when(valid_ref[i] == 1)
    def _():
        x = _unpack_bf16_pairs(x_ref[...])
        gate = jnp.dot(x, wg_s[...], preferred_element_type=F32)
        up = jnp.dot(x, wu_s[...], preferred_element_type=F32)
        mid = (_silu(gate) * up).astype(BF16)
        o_ref[...] = jnp.dot(mid, wd_s[...], preferred_element_type=F32)

    @pl.when(valid_ref[i] == 0)
    def _():
        o_ref[...] = jnp.zeros_like(o_ref)


def _experts(tile_meta, x_sorted, w_gate, w_up, w_down, l, n_tiles):
    d = D_MODEL
    tm = EXPERT_TILE
    hbm = pl.BlockSpec(memory_space=pl.ANY)
    return pl.pallas_call(
        functools.partial(_expert_kernel, l=l),
        grid_spec=pltpu.PrefetchScalarGridSpec(
            num_scalar_prefetch=5,
            grid=(n_tiles,),
            in_specs=[pl.BlockSpec((tm, d // 2), lambda i, te, nx, xi, fi, va: (xi[i], 0)),
                      hbm, hbm, hbm],
            out_specs=pl.BlockSpec((tm, d), lambda i, te, nx, xi, fi, va: (i, 0)),
            scratch_shapes=[pltpu.VMEM((d, D_EXPERT), F32), pltpu.VMEM((d, D_EXPERT), F32),
                            pltpu.VMEM((D_EXPERT, d), F32),
                            pltpu.VMEM((d, D_EXPERT), BF16), pltpu.VMEM((d, D_EXPERT), BF16),
                            pltpu.VMEM((D_EXPERT, d), BF16),
                            pltpu.SemaphoreType.DMA((3,))]),
        out_shape=jax.ShapeDtypeStruct((n_tiles * tm, d), F32),
        compiler_params=_params(("arbitrary",)),
        name="experts",
    )(*tile_meta, x_sorted, w_gate, w_up, w_down)


def _combine_kernel(pos_ref, h_ref, meta_ref, g_ref, ys_ref, o_ref, buf_s, sem, *, n_ctx, bm, t):
    i = pl.program_id(0)
    n = pl.num_programs(0)

    def issue(blk, slot):
        base = blk * bm

        def body(r, carry):
            for k in range(2):
                _row_copy(ys_ref, pos_ref[k * t + base + r], buf_s.at[slot, k], r,
                          sem.at[slot]).start(priority=k)
            return carry

        lax.fori_loop(0, bm, body, 0, unroll=4)

    @pl.when(i == 0)
    def _():
        issue(0, 0)

    @pl.when(i + 1 < n)
    def _():
        issue(i + 1, (i + 1) % 2)

    slot = i % 2
    for k in range(2):
        pltpu.make_async_copy(ys_ref.at[pl.ds(0, bm), :], buf_s.at[slot, k], sem.at[slot]).wait()
    meta = meta_ref[...]
    moe = meta[:, 2:3] * buf_s[slot, 0] + meta[:, 3:4] * buf_s[slot, 1]
    row = i * bm + lax.broadcasted_iota(jnp.int32, (bm, 1), 0)
    gate = jnp.where(row < n_ctx, g_ref[1:2, :], g_ref[0:1, :])
    o_ref[...] = h_ref[...] + gate * moe


def _combine(pos, h, meta, mod, y_sorted, l, n_ctx):
    t, d = h.shape
    bm = TOK_BLOCK
    return pl.pallas_call(
        functools.partial(_combine_kernel, n_ctx=n_ctx, bm=bm, t=t),
        grid_spec=pltpu.PrefetchScalarGridSpec(
            num_scalar_prefetch=1,
            grid=(t // bm,),
            in_specs=[pl.BlockSpec((bm, d), lambda i, pos: (i, 0)),
                      pl.BlockSpec((bm, LANES), lambda i, pos: (i, 0)),
                      pl.BlockSpec((None, 8, d), lambda i, pos: (l, 0, 5)),
                      pl.BlockSpec(memory_space=pl.ANY)],
            out_specs=pl.BlockSpec((bm, d), lambda i, pos: (i, 0)),
            scratch_shapes=[pltpu.VMEM((2, 2, bm, d), F32), pltpu.SemaphoreType.DMA((2,))]),
        out_shape=jax.ShapeDtypeStruct((t, d), F32),
        compiler_params=_params(("arbitrary",)),
        name="combine",
    )(pos, h, meta, mod, y_sorted)


def _tile_plan(meta, counts, n_tiles):
    tm = EXPERT_TILE
    cnt = counts[0, :N_EXPERTS].astype(jnp.int32)
    tiles_e = (cnt + tm - 1) // tm
    tile_end = jnp.cumsum(tiles_e)
    offset = (tile_end - tiles_e) * tm
    meta_t = meta[:, :8].T
    expert = meta_t[0:2].astype(jnp.int32)
    rank = meta_t[4:6].astype(jnp.int32)
    pos = (offset[expert] + rank).reshape(-1)
    n_used = tile_end[-1]
    tid = jnp.arange(n_tiles, dtype=jnp.int32)
    valid = tid < n_used
    tsrc = jnp.minimum(tid, n_used - 1)
    te = jnp.minimum(jnp.searchsorted(tile_end, tsrc, side="right"), N_EXPERTS - 1).astype(jnp.int32)
    first = (valid & ((tid == 0) | (te != jnp.roll(te, 1)))).astype(jnp.int32)
    after = tile_end[te]
    nxt = jnp.where(after < n_used, te[jnp.minimum(after, n_tiles - 1)], -1).astype(jnp.int32)
    return pos, (te, nxt, tsrc.astype(jnp.int32), first, valid.astype(jnp.int32))


def _final_kernel(h_ref, w_ref, o_ref):
    x = h_ref[...]
    ms = jnp.mean(x * x, axis=-1, keepdims=True)
    o_ref[...] = x * lax.rsqrt(ms + EPS) * w_ref[...]


def _final_norm(h, w, n_ctx):
    t, d = h.shape
    bm = TOK_BLOCK
    skip = n_ctx // bm
    return pl.pallas_call(
        _final_kernel,
        grid=((t - n_ctx) // bm,),
        in_specs=[pl.BlockSpec((bm, d), lambda i: (i + skip, 0)),
                  pl.BlockSpec((1, d), lambda i: (0, 0))],
        out_specs=pl.BlockSpec((bm, d), lambda i: (i, 0)),
        out_shape=jax.ShapeDtypeStruct((t - n_ctx, d), F32),
        compiler_params=_params(("arbitrary",)),
        name="final_norm",
    )(h, w)


def _pad_lanes(v):
    return jnp.pad(v, [(0, 0)] * (v.ndim - 1) + [(0, LANES - v.shape[-1])])


def _rope_tables(n_ctx, n_lat):
    rows = n_lat // GRID_W
    row = jnp.broadcast_to(jnp.arange(rows, dtype=F32)[:, None], (rows, GRID_W)).reshape(n_lat)
    col = jnp.broadcast_to(jnp.arange(GRID_W, dtype=F32)[None, :], (rows, GRID_W)).reshape(n_lat)
    inv_freq = ROPE_BASE ** (-jnp.arange(ROPE_FREQS, dtype=F32) / ROPE_FREQS)
    ang_r = row[:, None] * inv_freq[None, :]
    ang_c = col[:, None] * inv_freq[None, :]
    cos_l = jnp.concatenate([jnp.cos(ang_r)] * 2 + [jnp.cos(ang_c)] * 2, axis=1)
    sin_l = jnp.concatenate([-jnp.sin(ang_r), jnp.sin(ang_r), -jnp.sin(ang_c), jnp.sin(ang_c)], axis=1)
    cos_t = jnp.concatenate([jnp.ones((n_ctx, 2 * LANES), F32), cos_l], axis=0)
    sin_t = jnp.concatenate([jnp.zeros((n_ctx, 2 * LANES), F32), sin_l], axis=0)
    return cos_t, sin_t


def kernel(x, c, ctx, c_ctx, w_mod, b_mod, norm1_w, w_in, conv_w, conv_b, ssm_A_log, ssm_dt_bias,
           ssm_D, ssm_norm_w, ret_decay_logit, ret_norm_w, w_ssd_proj, w_ret_proj, w_out, norm2_w,
           w_route_group, b_route_group, w_route_expert, b_route_expert, w_exp_gate, w_exp_up,
           w_exp_down, final_norm_w):
    batch, n_lat, d = x.shape
    n_ctx = ctx.shape[1]
    depth = w_mod.shape[0]
    assert batch == 1 and d == D_MODEL and w_in.shape[-1] == MAIN_W + DT_W
    t = n_ctx + n_lat
    assert t % OUT_ROW_BLOCK == 0 and n_ctx % TOK_BLOCK == 0 and n_lat % TOK_BLOCK == 0

    w_nat, w_dt = _wprep(jnp.swapaxes(w_in, 1, 2))
    w_ssd_b = w_ssd_proj.astype(BF16)
    w_ret_b = w_ret_proj.astype(BF16)
    w_out_b = w_out.astype(BF16)
    w_route = _pad_lanes(jnp.concatenate([w_route_group, w_route_expert], axis=-1))
    b_route = _pad_lanes(jnp.concatenate([b_route_group, b_route_expert], axis=-1))[:, None, :]
    d_skip = jnp.repeat(ssm_D, SSM_HEADDIM, axis=-1)[:, None, :]
    dt_bias = _pad_lanes(ssm_dt_bias)
    a_log = _pad_lanes(ssm_A_log)
    ret_dl = _pad_lanes(ret_decay_logit)
    head_of_col = jnp.arange(D_MODEL, dtype=jnp.int32) // SSM_HEADDIM
    e_mat = ((jnp.arange(2 * LANES, dtype=jnp.int32) % LANES)[:, None]
             == head_of_col[None, :]).astype(BF16)
    cos_t, sin_t = _rope_tables(n_ctx, n_lat)

    cvec = jnp.concatenate([c, c_ctx[None, :], jnp.zeros((6, d), F32)], axis=0)
    mod = _modulation(cvec, w_mod, b_mod)

    n_tiles = (2 * t) // EXPERT_TILE + N_EXPERTS
    h = jnp.concatenate([ctx[0], x[0]], axis=0)
    x_sorted = jnp.zeros((n_tiles * EXPERT_TILE, d // 2), jnp.uint32)
    for l in range(depth):
        p_main, dt_raw = _inproj(h, mod, norm1_w[:, None, :], w_nat, w_dt, cos_t, sin_t, l, n_ctx)
        xbc_act = _conv(p_main, conv_w, conv_b[:, None, :], l, n_ctx)
        y_ssd = _ssd_scan(xbc_act, dt_raw, dt_bias[l], a_log[l], e_mat, n_ctx)
        y_ret = _ret_scan(p_main, ret_dl[l], n_ctx)
        merged = _merge(y_ssd, xbc_act, p_main, y_ret, d_skip, ssm_norm_w[:, None, :],
                        ret_norm_w[:, None, :], w_ssd_b, w_ret_b, l)
        h = _outproj(merged, w_out_b, h, mod, l, n_ctx)
        xn, meta, counts = _router(h, mod, norm2_w[:, None, :], w_route, b_route, l, n_ctx)
        pos, tile_meta = _tile_plan(meta, counts, n_tiles)
        x_sorted = _dispatch(pos, xn, x_sorted)
        y_sorted = _experts(tile_meta, x_sorted, w_exp_gate, w_exp_up, w_exp_down, l, n_tiles)
        h = _combine(pos, h, meta, mod, y_sorted, l, n_ctx)

    return _final_norm(h, final_norm_w[None, :], n_ctx)[None]
```

```python
import functools

import jax
import jax.numpy as jnp
from jax import lax
from jax.experimental import pallas as pl
from jax.experimental.pallas import tpu as pltpu

F32 = jnp.float32
BF16 = jnp.bfloat16
HIGHEST = lax.Precision.HIGHEST

D_MODEL = 2048
GRID_W = 64
CHUNK = 128
EPS = 1e-6

SSM_HEADDIM = 64
SSM_HEADS = 32
SSM_GROUPS = 4
SSM_HPG = 8
SSM_STATE = 128
SSM_CONV = 5
GROUP_W = SSM_HPG * SSM_HEADDIM
BC_W = SSM_GROUPS * SSM_STATE
XBC_D = D_MODEL + 2 * BC_W

RET_HEADS = 8
RET_DK = 256
ROPE_FREQS = 64
ROPE_BASE = 10000.0

N_GROUPS_E = 4
EXPERTS_PER_GROUP = 8
N_EXPERTS = 32
D_EXPERT = 512

LANES = 128
MAIN_W = 7 * D_MODEL + XBC_D
XBC_COL0 = 7 * D_MODEL
DT_W = SSM_HEADS
DT_COL0 = D_MODEL + XBC_D

INPROJ_ROW_BLOCKS = (1056, 768)
COL_BLOCK = 1024
OUT_ROW_BLOCK = 384
MERGE_ROW_BLOCK = 256
TOK_BLOCK = 256
DMA_GROUP = 8
EXPERT_TILE = 256
NEG_BIG = -1e30
VMEM_LIMIT = 56 * 1024 * 1024


def _params(sem, vmem=VMEM_LIMIT):
    return pltpu.CompilerParams(dimension_semantics=sem, vmem_limit_bytes=vmem)


def _sigmoid(v):
    return 0.5 * jnp.tanh(0.5 * v) + 0.5


def _silu(v):
    return v * _sigmoid(v)


def _softplus(v):
    return jnp.maximum(v, 0.0) + jnp.log1p(jnp.exp(-jnp.abs(v)))


def _split_bf16(v):
    hi = v.astype(BF16)
    return hi, (v - hi.astype(F32)).astype(BF16)


def _dot3(x, w):
    n = w.shape[1]
    xh, xl = _split_bf16(x)
    wh, wl = _split_bf16(w)
    both = jnp.dot(xh, jnp.concatenate([wh, wl], axis=1), preferred_element_type=F32)
    return both[:, :n] + both[:, n:] + jnp.dot(xl, wh, preferred_element_type=F32)


def _modulated_norm(x, nw, sh_ref, sc_ref, row0, n_ctx):
    ms = jnp.mean(x * x, axis=-1, keepdims=True)
    y = x * lax.rsqrt(ms + EPS) * nw
    row = row0 + lax.broadcasted_iota(jnp.int32, (x.shape[0], 1), 0)
    is_ctx = row < n_ctx
    sc = jnp.where(is_ctx, sc_ref[1:2, :], sc_ref[0:1, :])
    sh = jnp.where(is_ctx, sh_ref[1:2, :], sh_ref[0:1, :])
    return y * (1.0 + sc) + sh


def _mod_kernel(a_ref, w_ref, b_ref, o_ref):
    a = a_ref[...]
    o_ref[...] = jnp.dot(_silu(a), w_ref[...], precision=HIGHEST,
                         preferred_element_type=F32) + b_ref[...]


def _modulation(cvec, w_mod, b_mod):
    depth, d, n = w_mod.shape
    bn = 1024
    return pl.pallas_call(
        _mod_kernel,
        grid=(depth, n // bn),
        in_specs=[pl.BlockSpec((8, d), lambda l, j: (0, 0)),
                  pl.BlockSpec((None, d, bn), lambda l, j: (l, 0, j)),
                  pl.BlockSpec((None, 1, bn), lambda l, j: (l, 0, j))],
        out_specs=pl.BlockSpec((None, 8, bn), lambda l, j: (l, 0, j)),
        out_shape=jax.ShapeDtypeStruct((depth, 8, n), F32),
        compiler_params=_params(("arbitrary", "arbitrary")),
        name="modulation",
    )(cvec, w_mod, b_mod.reshape(depth, 1, n))


def _dot3_nt(x, wt):
    nt = (((1,), (1,)), ((), ()))
    n = wt.shape[0]
    xh, xl = _split_bf16(x)
    wh, wl = _split_bf16(wt)
    both = lax.dot_general(xh, jnp.concatenate([wh, wl], axis=0), nt, preferred_element_type=F32)
    return both[:, :n] + both[:, n:] + lax.dot_general(xl, wh, nt, preferred_element_type=F32)


def _wprep_kernel(w_hbm, o_ref, dt_ref, buf_s, dtb_s, sem, dsem, *, bn, n_aligned, nj):
    l = pl.program_id(0)
    n = pl.program_id(1)
    step = l * nj + n
    total = pl.num_programs(0) * nj

    def fetch(ll, nn, slot):
        row0 = pl.multiple_of(nn * bn + jnp.where(nn >= n_aligned, DT_W, 0), 8)
        return pltpu.make_async_copy(w_hbm.at[ll, pl.ds(row0, bn), :], buf_s.at[slot], sem.at[slot])

    @pl.when(step == 0)
    def _():
        fetch(l, n, 0).start()

    @pl.when(step + 1 < total)
    def _():
        wrap = n + 1 == nj
        fetch(jnp.where(wrap, l + 1, l), jnp.where(wrap, 0, n + 1), (step + 1) % 2).start()

    fetch(l, n, step % 2).wait()
    o_ref[...] = buf_s[step % 2].astype(BF16)

    @pl.when(n == 0)
    def _():
        cp = pltpu.make_async_copy(w_hbm.at[l, pl.ds(DT_COL0, DT_W), :], dtb_s, dsem)
        cp.start()
        cp.wait()
        dt_ref[0:DT_W, :] = dtb_s[...]
        dt_ref[DT_W:, :] = jnp.zeros((LANES - DT_W, dt_ref.shape[1]), F32)


def _wprep(w_in_t):
    depth, _, d = w_in_t.shape
    bn = COL_BLOCK
    nj = MAIN_W // bn
    return pl.pallas_call(
        functools.partial(_wprep_kernel, bn=bn, n_aligned=DT_COL0 // bn, nj=nj),
        grid=(depth, nj),
        in_specs=[pl.BlockSpec(memory_space=pl.ANY)],
        out_specs=[pl.BlockSpec((None, bn, d), lambda l, n: (l, n, 0)),
                   pl.BlockSpec((None, LANES, d), lambda l, n: (l, 0, 0))],
        out_shape=[jax.ShapeDtypeStruct((depth, MAIN_W, d), BF16),
                   jax.ShapeDtypeStruct((depth, LANES, d), F32)],
        scratch_shapes=[pltpu.VMEM((2, bn, d), F32), pltpu.VMEM((DT_W, d), F32),
                        pltpu.SemaphoreType.DMA((2,)), pltpu.SemaphoreType.DMA(())],
        compiler_params=_params(("arbitrary", "arbitrary")),
        name="wprep",
    )(w_in_t)


def _inproj_kernel(h_ref, sh_ref, sc_ref, nw_ref, w_ref, wdt_ref, cos_ref, sin_ref,
                   o_ref, dt_ref, xn_s, *, n_ctx, bm, bn):
    i = pl.program_id(0)
    j = pl.program_id(1)

    @pl.when(j == 0)
    def _():
        xn = _modulated_norm(h_ref[...], nw_ref[...], sh_ref, sc_ref, i * bm, n_ctx)
        xn_s[...] = xn.astype(BF16)
        dt_ref[...] = _dot3_nt(xn, wdt_ref[...])

    acc = lax.dot_general(xn_s[...], w_ref[...], (((1,), (1,)), ((), ())),
                          preferred_element_type=F32)
    q_lo = DT_COL0 // bn
    k_lo = (DT_COL0 + D_MODEL) // bn
    k_hi = (DT_COL0 + 2 * D_MODEL) // bn
    is_rope = (j >= q_lo) & (j < k_hi)

    @pl.when(is_rope)
    def _():
        scale = jnp.where(j >= k_lo, RET_DK ** -0.5, 1.0).astype(F32)
        for gi in range(bn // LANES):
            u = acc[:, gi * LANES:(gi + 1) * LANES]
            half = gi % 2
            c = cos_ref[:, half * LANES:(half + 1) * LANES]
            s = sin_ref[:, half * LANES:(half + 1) * LANES]
            r = (u * c + pltpu.roll(u, LANES // 2, 1) * s) * scale
            o_ref[:, gi * LANES:(gi + 1) * LANES] = r.astype(BF16)

    @pl.when(jnp.logical_not(is_rope))
    def _():
        o_ref[...] = acc.astype(BF16)


def _inproj(h, mod, norm_w, w_nat, w_dt, cos_t, sin_t, l, n_ctx):
    t, d = h.shape
    bm = next(b for b in INPROJ_ROW_BLOCKS if t % b == 0)
    bn = COL_BLOCK
    nj = MAIN_W // bn
    z_blocks = D_MODEL // bn
    xbc_blocks = XBC_D // bn

    def out_block(j):
        return jnp.where(j < z_blocks, j,
                         jnp.where(j < z_blocks + xbc_blocks, j + (nj - z_blocks - xbc_blocks),
                                   j - xbc_blocks))

    once = pl.Buffered(1)
    kern = functools.partial(_inproj_kernel, n_ctx=n_ctx, bm=bm, bn=bn)
    return pl.pallas_call(
        kern,
        grid=(t // bm, nj),
        in_specs=[pl.BlockSpec((bm, d), lambda i, j: (i, 0), pipeline_mode=once),
                  pl.BlockSpec((None, 8, d), lambda i, j: (l, 0, 0)),
                  pl.BlockSpec((None, 8, d), lambda i, j: (l, 0, 1)),
                  pl.BlockSpec((None, 1, d), lambda i, j: (l, 0, 0)),
                  pl.BlockSpec((None, bn, d), lambda i, j: (l, j, 0)),
                  pl.BlockSpec((None, LANES, d), lambda i, j: (l, 0, 0)),
                  pl.BlockSpec((bm, 2 * LANES), lambda i, j: (i, 0), pipeline_mode=once),
                  pl.BlockSpec((bm, 2 * LANES), lambda i, j: (i, 0), pipeline_mode=once)],
        out_specs=[pl.BlockSpec((bm, bn), lambda i, j: (i, out_block(j))),
                   pl.BlockSpec((bm, LANES), lambda i, j: (i, 0))],
        out_shape=[jax.ShapeDtypeStruct((t, MAIN_W), BF16),
                   jax.ShapeDtypeStruct((t, LANES), F32)],
        scratch_shapes=[pltpu.VMEM((bm, d), BF16)],
        compiler_params=_params(("arbitrary", "arbitrary")),
        name="inproj",
    )(h, mod, mod, norm_w, w_nat, w_dt, cos_t, sin_t)


def _conv_kernel(x_ref, w_ref, b_ref, o_ref, pad_s, *, n_ctx, n_lat, rows):
    halo = 8
    zeros = jnp.zeros((halo, LANES), F32)
    c0 = halo
    x0 = 2 * halo + n_ctx
    pad_s[0:halo, :] = zeros
    pad_s[c0:c0 + n_ctx, :] = x_ref[0:n_ctx, :].astype(F32)
    pad_s[c0 + n_ctx:x0, :] = zeros
    pad_s[x0:x0 + n_lat, :] = x_ref[n_ctx:n_ctx + n_lat, :].astype(F32)
    pad_s[x0 + n_lat:x0 + n_lat + halo, :] = zeros
    w = w_ref[...]
    b = b_ref[...]

    def segment(src0, dst0, n):
        def body(s, carry):
            r0 = pl.multiple_of(s * rows, rows)
            acc = jnp.broadcast_to(b, (rows, LANES))
            for kk in range(SSM_CONV):
                acc = acc + w[kk:kk + 1, :] * pad_s[pl.ds(src0 + r0 + kk - SSM_CONV // 2, rows), :]
            o_ref[pl.ds(dst0 + r0, rows), :] = _silu(acc).astype(BF16)
            return carry
        lax.fori_loop(0, n // rows, body, 0)

    segment(c0, 0, n_ctx)
    segment(x0, n_ctx, n_lat)


def _conv(p_main, conv_w, conv_b, l, n_ctx):
    t = p_main.shape[0]
    n_lat = t - n_ctx
    col0 = XBC_COL0 // LANES
    kern = functools.partial(_conv_kernel, n_ctx=n_ctx, n_lat=n_lat, rows=256)
    return pl.pallas_call(
        kern,
        grid=(XBC_D // LANES,),
        in_specs=[pl.BlockSpec((t, LANES), lambda c: (0, col0 + c)),
                  pl.BlockSpec((None, SSM_CONV, LANES), lambda c: (l, 0, c)),
                  pl.BlockSpec((None, 1, LANES), lambda c: (l, 0, c))],
        out_specs=pl.BlockSpec((t, LANES), lambda c: (0, c)),
        out_shape=jax.ShapeDtypeStruct((t, XBC_D), BF16),
        scratch_shapes=[pltpu.VMEM((t + 24, LANES), F32)],
        compiler_params=_params(("arbitrary",)),
        name="conv_silu",
    )(p_main, conv_w, conv_b)


def _chunk_index(s, nc_ctx, nc, rev):
    if not rev:
        return s
    return jnp.where(s < nc_ctx, nc_ctx - 1 - s, nc - 1 - (s - nc_ctx))


def _split_dot(v, e2):
    hi, lo = _split_bf16(v)
    return jnp.dot(jnp.concatenate([hi, lo], axis=1), e2, preferred_element_type=F32)


def _ssd_kernel(xs_f, b_f, c_f, dt_f, xs_b, b_b, c_b, dt_b, bias_ref, alog_ref, e_ref,
                of_ref, ob_ref, h_s):
    @pl.when(pl.program_id(0) == 0)
    def _():
        h_s[...] = jnp.zeros_like(h_s)

    e = e_ref[...]
    _ssd_chunk(xs_f, b_f, c_f, dt_f, bias_ref[0:1, :], alog_ref[0:1, :], e, of_ref, h_s.at[0], False)
    _ssd_chunk(xs_b, b_b, c_b, dt_b, bias_ref[1:2, :], alog_ref[1:2, :], e, ob_ref, h_s.at[1], True)


def _ssd_chunk(xs_ref, b_ref, c_ref, dt_ref, bias, alog, e, o_ref, h_s, rev):
    q = CHUNK
    ii = lax.broadcasted_iota(jnp.int32, (q, q), 0)
    jj = lax.broadcasted_iota(jnp.int32, (q, q), 1)
    seen = (jj >= ii) if rev else (jj <= ii)
    tri = jnp.where(seen, 1.0, 0.0).astype(F32)

    dt = _softplus(dt_ref[...] + bias)
    a = dt * (-jnp.exp(alog))
    cum = jnp.dot(tri, a, precision=HIGHEST, preferred_element_type=F32)
    cum_t = cum.T
    dt_t = dt.T
    total = cum[0:1, :] if rev else cum[q - 1:q, :]
    dfs = jnp.exp(cum)
    w_end = dt * jnp.exp(total - cum)
    cdec = jnp.broadcast_to(jnp.exp(total), (8, LANES))
    expanded = _split_dot(jnp.concatenate([w_end, dfs, cdec], axis=0), e)
    w_end_x = expanded[0:q]
    dfs_x = expanded[q:2 * q]
    cdec_x = expanded[2 * q:2 * q + 1]

    xs = xs_ref[...]
    xw = (xs.astype(F32) * w_end_x).astype(BF16)
    first_head = jj < SSM_HEADDIM

    for g in range(SSM_GROUPS):
        cg = c_ref[:, g * SSM_STATE:(g + 1) * SSM_STATE]
        bg = b_ref[:, g * SSM_STATE:(g + 1) * SSM_STATE]
        gs = slice(g * GROUP_W, (g + 1) * GROUP_W)
        cb = lax.dot_general(cg, bg, (((1,), (1,)), ((), ())), preferred_element_type=F32)
        h_prev = h_s[:, gs]
        y_off = jnp.dot(cg, h_prev.astype(BF16), preferred_element_type=F32) * dfs_x[:, gs]
        new = lax.dot_general(bg, xw[:, gs], (((0,), (0,)), ((), ())), preferred_element_type=F32)
        pieces = []
        for hh in range(0, SSM_HPG, 2):
            ms = []
            for hd in (g * SSM_HPG + hh, g * SSM_HPG + hh + 1):
                seg = cum[:, hd:hd + 1] - cum_t[hd:hd + 1, :]
                m = jnp.exp(jnp.where(seen, seg, NEG_BIG)) * cb * dt_t[hd:hd + 1, :]
                ms.append(m.astype(BF16))
            c0 = (g * SSM_HPG + hh) * SSM_HEADDIM
            xp = xs[:, c0:c0 + LANES]
            zero = jnp.zeros_like(xp)
            rhs = jnp.concatenate([jnp.where(first_head, xp, zero), jnp.where(first_head, zero, xp)],
                                  axis=0)
            pieces.append(jnp.dot(jnp.concatenate(ms, axis=1), rhs, preferred_element_type=F32))
        y = jnp.concatenate(pieces, axis=1) + y_off
        o_ref[:, gs] = y.astype(o_ref.dtype)
        h_s[:, gs] = h_prev * cdec_x[:, gs] + new


def _scan_specs(n_ctx, t, col_blocks, width):
    nc, nc_ctx = t // CHUNK, n_ctx // CHUNK
    specs = []
    for rev in (False, True):
        cidx = functools.partial(_chunk_index, nc_ctx=nc_ctx, nc=nc, rev=rev)
        specs.append([pl.BlockSpec((CHUNK, w), lambda s, c=c, cidx=cidx: (cidx(s), c))
                      for c, w in zip(col_blocks, width)])
    return specs


def _ssd_scan(xbc_act, dt_raw, dt_bias, a_log, e_mat, n_ctx):
    t = xbc_act.shape[0]
    bc0 = D_MODEL // BC_W
    fwd, bwd = _scan_specs(n_ctx, t, (0, bc0, bc0 + 1, 0), (D_MODEL, BC_W, BC_W, LANES))
    out_f, out_b = _scan_specs(n_ctx, t, (0,), (D_MODEL,))
    const = [pl.BlockSpec((2, LANES), lambda s: (0, 0)),
             pl.BlockSpec((2, LANES), lambda s: (0, 0)),
             pl.BlockSpec((2 * LANES, D_MODEL), lambda s: (0, 0))]
    seqs = (xbc_act, xbc_act, xbc_act, dt_raw)
    return pl.pallas_call(
        _ssd_kernel,
        grid=(t // CHUNK,),
        in_specs=fwd + bwd + const,
        out_specs=out_f + out_b,
        out_shape=[jax.ShapeDtypeStruct((t, D_MODEL), BF16)] * 2,
        scratch_shapes=[pltpu.VMEM((2, SSM_STATE, D_MODEL), F32)],
        compiler_params=_params(("arbitrary",)),
        name="ssd_scan",
    )(*seqs, *seqs, dt_bias, a_log, e_mat)


def _ret_kernel(q_f, k_f, v_f, q_b, k_b, v_b, dl_ref, of_ref, ob_ref, r_s, dm_s, kd_s, qd_s, cd_s):
    qn = CHUNK

    @pl.when(pl.program_id(0) == 0)
    def _():
        lg_both = -_softplus(-dl_ref[...])
        cd_s[...] = jnp.exp(float(qn) * lg_both)
        r_s[...] = jnp.zeros_like(r_s)
        ii = lax.broadcasted_iota(jnp.int32, (qn, qn), 0)
        jj = lax.broadcasted_iota(jnp.int32, (qn, qn), 1)
        for di, rev in enumerate((False, True)):
            diff = (jj - ii) if rev else (ii - jj)
            dpos = jnp.maximum(diff, 0).astype(F32)
            k_exp = (ii if rev else (qn - 1 - ii)).astype(F32)
            q_exp = ((qn - ii) if rev else (ii + 1)).astype(F32)
            for hd in range(RET_HEADS):
                lg = lg_both[di:di + 1, hd:hd + 1]
                dm_s[di, hd] = jnp.where(diff >= 0, jnp.exp(dpos * lg), 0.0)
                kd_s[di, hd] = jnp.exp(k_exp * lg)
                qd_s[di, hd] = jnp.exp(q_exp * lg)

    _ret_chunk(q_f, k_f, v_f, cd_s[0:1, :], of_ref, r_s.at[0], dm_s.at[0], kd_s.at[0], qd_s.at[0])
    _ret_chunk(q_b, k_b, v_b, cd_s[1:2, :], ob_ref, r_s.at[1], dm_s.at[1], kd_s.at[1], qd_s.at[1])


def _ret_chunk(q_ref, k_ref, v_ref, chunk_decay, o_ref, r_s, dm_s, kd_s, qd_s):
    for hd in range(RET_HEADS):
        hs = slice(hd * RET_DK, (hd + 1) * RET_DK)
        qh = q_ref[:, hs]
        kh = k_ref[:, hs]
        vh = v_ref[:, hs]
        sc = lax.dot_general(qh, kh, (((1,), (1,)), ((), ())), preferred_element_type=F32)
        y = jnp.dot((sc * dm_s[hd]).astype(BF16), vh, preferred_element_type=F32)
        r_prev = r_s[hd]
        qd = qd_s[hd]
        y_int = jnp.dot(qh, r_prev.astype(BF16), preferred_element_type=F32)
        y = y + y_int * jnp.concatenate([qd, qd], axis=1)
        kd = kd_s[hd]
        kdec = (kh.astype(F32) * jnp.concatenate([kd, kd], axis=1)).astype(BF16)
        kv = lax.dot_general(kdec, vh, (((0,), (0,)), ((), ())), preferred_element_type=F32)
        r_s[hd] = r_prev * chunk_decay[:, hd:hd + 1] + kv
        o_ref[:, hs] = y.astype(o_ref.dtype)


def _ret_scan(p_main, decay_logit, n_ctx):
    t = p_main.shape[0]
    fwd, bwd = _scan_specs(n_ctx, t, (1, 2, 3), (D_MODEL,) * 3)
    out_f, out_b = _scan_specs(n_ctx, t, (0,), (D_MODEL,))
    seqs = (p_main, p_main, p_main)
    decay_tables = pltpu.VMEM((2, RET_HEADS, CHUNK, CHUNK), F32)
    return pl.pallas_call(
        _ret_kernel,
        grid=(t // CHUNK,),
        in_specs=fwd + bwd + [pl.BlockSpec((2, LANES), lambda s: (0, 0))],
        out_specs=out_f + out_b,
        out_shape=[jax.ShapeDtypeStruct((t, D_MODEL), BF16)] * 2,
        scratch_shapes=[pltpu.VMEM((2, RET_HEADS, RET_DK, RET_DK), F32),
                        decay_tables, decay_tables, decay_tables, pltpu.VMEM((2, LANES), F32)],
        compiler_params=_params(("arbitrary",)),
        name="ret_scan",
    )(*seqs, *seqs, decay_logit)


def _merge_kernel(ysf_ref, ysb_ref, xs_ref, z_ref, yrf_ref, yrb_ref, g_ref, gs_ref, gr_ref,
                  dsk_ref, snw_ref, rnw_ref, ws_ref, wr_ref, o_ref):
    y = (ysf_ref[...].astype(F32) + ysb_ref[...].astype(F32)
         + dsk_ref[...] * xs_ref[...].astype(F32))
    y = y * _silu(z_ref[...].astype(F32))
    ms = jnp.mean(y * y, axis=-1, keepdims=True)
    a = (y * lax.rsqrt(ms + EPS) * snw_ref[...]).astype(BF16)
    merged = _sigmoid(gs_ref[...].astype(F32)) * jnp.dot(a, ws_ref[...],
                                                         preferred_element_type=F32)
    gate = _silu(g_ref[...].astype(F32)) * rnw_ref[...]
    parts = []
    for hd in range(RET_HEADS):
        hs = slice(hd * RET_DK, (hd + 1) * RET_DK)
        r = yrf_ref[:, hs].astype(F32) + yrb_ref[:, hs].astype(F32)
        ms = jnp.mean(r * r, axis=-1, keepdims=True)
        parts.append((r * lax.rsqrt(ms + EPS) * gate[:, hs]).astype(BF16))
    b = jnp.concatenate(parts, axis=1)
    merged = merged + _sigmoid(gr_ref[...].astype(F32)) * jnp.dot(
        b, wr_ref[...], preferred_element_type=F32)
    o_ref[...] = merged.astype(BF16)


def _merge(y_ssd, xbc_act, p_main, y_ret, d_skip, ssm_norm_w, ret_norm_w, w_ssd, w_ret, l):
    t, d = xbc_act.shape[0], D_MODEL
    bm = MERGE_ROW_BLOCK
    once = pl.Buffered(1)
    return pl.pallas_call(
        _merge_kernel,
        grid=(t // bm,),
        in_specs=[pl.BlockSpec((bm, d), lambda i: (i, 0)),
                  pl.BlockSpec((bm, d), lambda i: (i, 0)),
                  pl.BlockSpec((bm, d), lambda i: (i, 0)),
                  pl.BlockSpec((bm, d), lambda i: (i, 0)),
                  pl.BlockSpec((bm, d), lambda i: (i, 0)),
                  pl.BlockSpec((bm, d), lambda i: (i, 0)),
                  pl.BlockSpec((bm, d), lambda i: (i, 4)),
                  pl.BlockSpec((bm, d), lambda i: (i, 5)),
                  pl.BlockSpec((bm, d), lambda i: (i, 6)),
                  pl.BlockSpec((None, 1, d), lambda i: (l, 0, 0)),
                  pl.BlockSpec((None, 1, d), lambda i: (l, 0, 0)),
                  pl.BlockSpec((None, 1, d), lambda i: (l, 0, 0)),
                  pl.BlockSpec((None, d, d), lambda i: (l, 0, 0), pipeline_mode=once),
                  pl.BlockSpec((None, d, d), lambda i: (l, 0, 0), pipeline_mode=once)],
        out_specs=pl.BlockSpec((bm, d), lambda i: (i, 0)),
        out_shape=jax.ShapeDtypeStruct((t, d), BF16),
        compiler_params=_params(("arbitrary",)),
        name="merge",
    )(*y_ssd, xbc_act, p_main, *y_ret, p_main, p_main, p_main, d_skip, ssm_norm_w, ret_norm_w,
      w_ssd, w_ret)


def _outproj_kernel(m_ref, w_ref, h_ref, g_ref, o_ref, *, n_ctx, bm):
    i = pl.program_id(0)
    row = i * bm + lax.broadcasted_iota(jnp.int32, (bm, 1), 0)
    gate = jnp.where(row < n_ctx, g_ref[1:2, :], g_ref[0:1, :])
    o_ref[...] = h_ref[...] + gate * jnp.dot(m_ref[...], w_ref[...], preferred_element_type=F32)


def _outproj(merged, w_out, h, mod, l, n_ctx):
    t, d = h.shape
    bm = OUT_ROW_BLOCK
    return pl.pallas_call(
        functools.partial(_outproj_kernel, n_ctx=n_ctx, bm=bm),
        grid=(t // bm,),
        in_specs=[pl.BlockSpec((bm, d), lambda i: (i, 0)),
                  pl.BlockSpec((None, d, d), lambda i: (l, 0, 0), pipeline_mode=pl.Buffered(1)),
                  pl.BlockSpec((bm, d), lambda i: (i, 0)),
                  pl.BlockSpec((None, 8, d), lambda i: (l, 0, 2))],
        out_specs=pl.BlockSpec((bm, d), lambda i: (i, 0)),
        out_shape=jax.ShapeDtypeStruct((t, d), F32),
        compiler_params=_params(("arbitrary",)),
        name="outproj",
    )(merged, w_out, h, mod)


def _pack_bf16_pairs(v):
    n = v.shape[1] // 2
    bits = lax.bitcast_convert_type(v.astype(BF16).astype(F32), jnp.uint32)
    return bits[:, :n] | (bits[:, n:] >> 16)


def _unpack_bf16_pairs(p):
    hi = lax.bitcast_convert_type(p & jnp.uint32(0xFFFF0000), F32)
    lo = lax.bitcast_convert_type(p << 16, F32)
    return jnp.concatenate([hi, lo], axis=1).astype(BF16)


def _router_kernel(h_ref, sh_ref, sc_ref, nw_ref, wr_ref, br_ref, xn_ref, meta_ref, cnt_ref,
                   carry_s, *, n_ctx, bm):
    i = pl.program_id(0)

    @pl.when(i == 0)
    def _():
        carry_s[...] = jnp.zeros_like(carry_s)

    xn = _modulated_norm(h_ref[...], nw_ref[...], sh_ref, sc_ref, i * bm, n_ctx)
    xn_ref[...] = _pack_bf16_pairs(xn)
    logits = _dot3(xn, wr_ref[...]) + br_ref[...]
    lane = lax.broadcasted_iota(jnp.int32, (bm, LANES), 1)
    far = jnp.int32(4 * LANES)

    glog = jnp.where(lane < N_GROUPS_E, logits, NEG_BIG)
    gmax = jnp.max(glog, axis=1, keepdims=True)
    gidx = jnp.min(jnp.where(glog == gmax, lane, far), axis=1, keepdims=True)
    p_top = 1.0 / jnp.sum(jnp.exp(glog - gmax), axis=1, keepdims=True)

    in_group = ((lane >= N_GROUPS_E) & (lane < N_GROUPS_E + N_EXPERTS)
                & (((lane - N_GROUPS_E) >> 3) == gidx))
    elog = jnp.where(in_group, logits, NEG_BIG)
    m1 = jnp.max(elog, axis=1, keepdims=True)
    i1 = jnp.min(jnp.where(elog == m1, lane, far), axis=1, keepdims=True)
    elog2 = jnp.where(lane == i1, NEG_BIG, elog)
    m2 = jnp.max(elog2, axis=1, keepdims=True)
    i2 = jnp.min(jnp.where(elog2 == m2, lane, far), axis=1, keepdims=True)
    e2 = jnp.exp(m2 - m1)
    w1 = p_top / (1.0 + e2)
    w2 = p_top * e2 / (1.0 + e2)
    ex1 = i1 - N_GROUPS_E
    ex2 = i2 - N_GROUPS_E

    onehot = jnp.where((lane == ex1) | (lane == ex2), 1.0, 0.0).astype(F32)
    ri = lax.broadcasted_iota(jnp.int32, (bm, bm), 0)
    ci = lax.broadcasted_iota(jnp.int32, (bm, bm), 1)
    earlier = jnp.where(ci < ri, 1.0, 0.0).astype(BF16)
    before = jnp.dot(earlier, onehot.astype(BF16), preferred_element_type=F32) + carry_s[0:1, :]
    r1 = jnp.sum(jnp.where(lane == ex1, before, 0.0), axis=1, keepdims=True)
    r2 = jnp.sum(jnp.where(lane == ex2, before, 0.0), axis=1, keepdims=True)
    carry = carry_s[...] + jnp.sum(onehot, axis=0, keepdims=True)
    carry_s[...] = carry
    cnt_ref[...] = carry

    meta = jnp.where(lane == 0, ex1.astype(F32), 0.0)
    meta = jnp.where(lane == 1, ex2.astype(F32), meta)
    meta = jnp.where(lane == 2, w1, meta)
    meta = jnp.where(lane == 3, w2, meta)
    meta = jnp.where(lane == 4, r1, meta)
    meta = jnp.where(lane == 5, r2, meta)
    meta_ref[...] = meta


def _router(h, mod, norm_w, w_route, b_route, l, n_ctx):
    t, d = h.shape
    bm = TOK_BLOCK
    return pl.pallas_call(
        functools.partial(_router_kernel, n_ctx=n_ctx, bm=bm),
        grid=(t // bm,),
        in_specs=[pl.BlockSpec((bm, d), lambda i: (i, 0)),
                  pl.BlockSpec((None, 8, d), lambda i: (l, 0, 3)),
                  pl.BlockSpec((None, 8, d), lambda i: (l, 0, 4)),
                  pl.BlockSpec((None, 1, d), lambda i: (l, 0, 0)),
                  pl.BlockSpec((None, d, LANES), lambda i: (l, 0, 0)),
                  pl.BlockSpec((None, 1, LANES), lambda i: (l, 0, 0))],
        out_specs=[pl.BlockSpec((bm, d // 2), lambda i: (i, 0)),
                   pl.BlockSpec((bm, LANES), lambda i: (i, 0)),
                   pl.BlockSpec((8, LANES), lambda i: (0, 0))],
        out_shape=[jax.ShapeDtypeStruct((t, d // 2), jnp.uint32),
                   jax.ShapeDtypeStruct((t, LANES), F32),
                   jax.ShapeDtypeStruct((8, LANES), F32)],
        scratch_shapes=[pltpu.VMEM((8, LANES), F32)],
        compiler_params=_params(("arbitrary",)),
        name="router",
    )(h, mod, mod, norm_w, w_route, b_route)


def _row_copy(src, src_row, dst, dst_row, sem):
    return pltpu.make_async_copy(src.at[pl.ds(src_row, 1), :], dst.at[pl.ds(dst_row, 1), :], sem)


def _dispatch_kernel(pos_ref, xn_ref, init_ref, o_ref, sem, *, bm, t):
    del init_ref
    base = pl.program_id(0) * bm

    def issue(g, carry):
        r0 = g * DMA_GROUP
        ps = [[pos_ref[k * t + base + r0 + u] for k in range(2)] for u in range(DMA_GROUP)]
        for u in range(DMA_GROUP):
            for k in range(2):
                _row_copy(xn_ref, r0 + u, o_ref, ps[u][k], sem).start(priority=k)
        return carry

    lax.fori_loop(0, bm // DMA_GROUP, issue, 0)
    for k in range(2):
        pltpu.make_async_copy(xn_ref, o_ref.at[pl.ds(0, bm), :], sem).wait()


def _dispatch(pos, xn, init):
    t, d = xn.shape
    n_rows = init.shape[0]
    bm = TOK_BLOCK
    return pl.pallas_call(
        functools.partial(_dispatch_kernel, bm=bm, t=t),
        grid_spec=pltpu.PrefetchScalarGridSpec(
            num_scalar_prefetch=1,
            grid=(t // bm,),
            in_specs=[pl.BlockSpec((bm, d), lambda i, pos: (i, 0)),
                      pl.BlockSpec(memory_space=pl.ANY)],
            out_specs=pl.BlockSpec(memory_space=pl.ANY),
            scratch_shapes=[pltpu.SemaphoreType.DMA(())]),
        out_shape=jax.ShapeDtypeStruct((n_rows, d), xn.dtype),
        input_output_aliases={2: 0},
        compiler_params=_params(("arbitrary",)),
        name="dispatch",
    )(pos, xn, init)


def _expert_kernel(te_ref, nx_ref, xi_ref, first_ref, valid_ref, x_ref, wg_hbm, wu_hbm, wd_hbm,
                   o_ref, lg_s, lu_s, ld_s, wg_s, wu_s, wd_s, sem, *, l):
    i = pl.program_id(0)

    def fetch(e):
        return (pltpu.make_async_copy(wg_hbm.at[l, e], lg_s, sem.at[0]),
                pltpu.make_async_copy(wu_hbm.at[l, e], lu_s, sem.at[1]),
                pltpu.make_async_copy(wd_hbm.at[l, e], ld_s, sem.at[2]))

    @pl.when(i == 0)
    def _():
        for cp in fetch(te_ref[0]):
            cp.start()

    @pl.when(first_ref[i] == 1)
    def _():
        for cp in fetch(te_ref[i]):
            cp.wait()
        wg_s[...] = lg_s[...].astype(BF16)
        wu_s[...] = lu_s[...].astype(BF16)
        wd_s[...] = ld_s[...].astype(BF16)

        @pl.when(nx_ref[i] >= 0)
        def _():
            for cp in fetch(nx_ref[i]):
                cp.start()

    @pl.when(valid_ref[i] == 1)
    def _():
        x = _unpack_bf16_pairs(x_ref[...])
        gate = jnp.dot(x, wg_s[...], preferred_element_type=F32)
        up = jnp.dot(x, wu_s[...], preferred_element_type=F32)
        mid = (_silu(gate) * up).astype(BF16)
        o_ref[...] = jnp.dot(mid, wd_s[...], preferred_element_type=F32)

    @pl.when(valid_ref[i] == 0)
    def _():
        o_ref[...] = jnp.zeros_like(o_ref)


def _experts(tile_meta, x_sorted, w_gate, w_up, w_down, l, n_tiles):
    d = D_MODEL
    tm = EXPERT_TILE
    hbm = pl.BlockSpec(memory_space=pl.ANY)
    return pl.pallas_call(
        functools.partial(_expert_kernel, l=l),
        grid_spec=pltpu.PrefetchScalarGridSpec(
            num_scalar_prefetch=5,
            grid=(n_tiles,),
            in_specs=[pl.BlockSpec((tm, d // 2), lambda i, te, nx, xi, fi, va: (xi[i], 0)),
                      hbm, hbm, hbm],
            out_specs=pl.BlockSpec((tm, d), lambda i, te, nx, xi, fi, va: (i, 0)),
            scratch_shapes=[pltpu.VMEM((d, D_EXPERT), F32), pltpu.VMEM((d, D_EXPERT), F32),
                            pltpu.VMEM((D_EXPERT, d), F32),
                            pltpu.VMEM((d, D_EXPERT), BF16), pltpu.VMEM((d, D_EXPERT), BF16),
                            pltpu.VMEM((D_EXPERT, d), BF16),
                            pltpu.SemaphoreType.DMA((3,))]),
        out_shape=jax.ShapeDtypeStruct((n_tiles * tm, d), F32),
        compiler_params=_params(("arbitrary",)),
        name="experts",
    )(*tile_meta, x_sorted, w_gate, w_up, w_down)


def _combine_kernel(pos_ref, h_ref, meta_ref, g_ref, ys_ref, o_ref, buf_s, sem, *, n_ctx, bm, t):
    i = pl.program_id(0)
    n = pl.num_programs(0)

    def issue(blk, slot):
        base = blk * bm

        def body(g, carry):
            r0 = g * DMA_GROUP
            ps = [[pos_ref[k * t + base + r0 + u] for k in range(2)] for u in range(DMA_GROUP)]
            for u in range(DMA_GROUP):
                for k in range(2):
                    _row_copy(ys_ref, ps[u][k], buf_s.at[slot, k], r0 + u,
                              sem.at[slot]).start(priority=k)
            return carry

        lax.fori_loop(0, bm // DMA_GROUP, body, 0)

    @pl.when(i == 0)
    def _():
        issue(0, 0)

    @pl.when(i + 1 < n)
    def _():
        issue(i + 1, (i + 1) % 2)

    slot = i % 2
    for k in range(2):
        pltpu.make_async_copy(ys_ref.at[pl.ds(0, bm), :], buf_s.at[slot, k], sem.at[slot]).wait()
    meta = meta_ref[...]
    moe = meta[:, 2:3] * buf_s[slot, 0] + meta[:, 3:4] * buf_s[slot, 1]
    row = i * bm + lax.broadcasted_iota(jnp.int32, (bm, 1), 0)
    gate = jnp.where(row < n_ctx, g_ref[1:2, :], g_ref[0:1, :])
    o_ref[...] = h_ref[...] + gate * moe


def _combine(pos, h, meta, mod, y_sorted, l, n_ctx):
    t, d = h.shape
    bm = TOK_BLOCK
    return pl.pallas_call(
        functools.partial(_combine_kernel, n_ctx=n_ctx, bm=bm, t=t),
        grid_spec=pltpu.PrefetchScalarGridSpec(
            num_scalar_prefetch=1,
            grid=(t // bm,),
            in_specs=[pl.BlockSpec((bm, d), lambda i, pos: (i, 0)),
                      pl.BlockSpec((bm, LANES), lambda i, pos: (i, 0)),
                      pl.BlockSpec((None, 8, d), lambda i, pos: (l, 0, 5)),
                      pl.BlockSpec(memory_space=pl.ANY)],
            out_specs=pl.BlockSpec((bm, d), lambda i, pos: (i, 0)),
            scratch_shapes=[pltpu.VMEM((2, 2, bm, d), F32), pltpu.SemaphoreType.DMA((2,))]),
        out_shape=jax.ShapeDtypeStruct((t, d), F32),
        compiler_params=_params(("arbitrary",)),
        name="combine",
    )(pos, h, meta, mod, y_sorted)


def _tile_plan(meta, counts, n_tiles):
    tm = EXPERT_TILE
    cnt = counts[0, :N_EXPERTS].astype(jnp.int32)
    tiles_e = (cnt + tm - 1) // tm
    tile_end = jnp.cumsum(tiles_e)
    offset = (tile_end - tiles_e) * tm
    meta_t = meta[:, :8].T
    expert = meta_t[0:2].astype(jnp.int32)
    rank = meta_t[4:6].astype(jnp.int32)
    ids = jnp.arange(N_EXPERTS, dtype=jnp.int32)[:, None, None]
    base = jnp.sum(jnp.where(expert[None] == ids, offset[:, None, None], 0), axis=0)
    pos = (base + rank).reshape(-1)
    n_used = tile_end[-1]
    tid = jnp.arange(n_tiles, dtype=jnp.int32)
    valid = tid < n_used
    tsrc = jnp.minimum(tid, n_used - 1)
    te = jnp.minimum(jnp.searchsorted(tile_end, tsrc, side="right"), N_EXPERTS - 1).astype(jnp.int32)
    first = (valid & ((tid == 0) | (te != jnp.roll(te, 1)))).astype(jnp.int32)
    after = tile_end[te]
    nxt = jnp.where(after < n_used, te[jnp.minimum(after, n_tiles - 1)], -1).astype(jnp.int32)
    return pos, (te, nxt, tsrc.astype(jnp.int32), first, valid.astype(jnp.int32))


def _final_kernel(h_ref, w_ref, o_ref):
    x = h_ref[...]
    ms = jnp.mean(x * x, axis=-1, keepdims=True)
    o_ref[...] = x * lax.rsqrt(ms + EPS) * w_ref[...]


def _final_norm(h, w, n_ctx):
    t, d = h.shape
    bm = TOK_BLOCK
    skip = n_ctx // bm
    return pl.pallas_call(
        _final_kernel,
        grid=((t - n_ctx) // bm,),
        in_specs=[pl.BlockSpec((bm, d), lambda i: (i + skip, 0)),
                  pl.BlockSpec((1, d), lambda i: (0, 0))],
        out_specs=pl.BlockSpec((bm, d), lambda i: (i, 0)),
        out_shape=jax.ShapeDtypeStruct((t - n_ctx, d), F32),
        compiler_params=_params(("arbitrary",)),
        name="final_norm",
    )(h, w)


def _pad_lanes(v):
    return jnp.pad(v, [(0, 0)] * (v.ndim - 1) + [(0, LANES - v.shape[-1])])


def _rope_tables(n_ctx, n_lat):
    rows = n_lat // GRID_W
    row = jnp.broadcast_to(jnp.arange(rows, dtype=F32)[:, None], (rows, GRID_W)).reshape(n_lat)
    col = jnp.broadcast_to(jnp.arange(GRID_W, dtype=F32)[None, :], (rows, GRID_W)).reshape(n_lat)
    inv_freq = ROPE_BASE ** (-jnp.arange(ROPE_FREQS, dtype=F32) / ROPE_FREQS)
    ang_r = row[:, None] * inv_freq[None, :]
    ang_c = col[:, None] * inv_freq[None, :]
    cos_l = jnp.concatenate([jnp.cos(ang_r)] * 2 + [jnp.cos(ang_c)] * 2, axis=1)
    sin_l = jnp.concatenate([-jnp.sin(ang_r), jnp.sin(ang_r), -jnp.sin(ang_c), jnp.sin(ang_c)], axis=1)
    cos_t = jnp.concatenate([jnp.ones((n_ctx, 2 * LANES), F32), cos_l], axis=0)
    sin_t = jnp.concatenate([jnp.zeros((n_ctx, 2 * LANES), F32), sin_l], axis=0)
    return cos_t, sin_t


def kernel(x, c, ctx, c_ctx, w_mod, b_mod, norm1_w, w_in, conv_w, conv_b, ssm_A_log, ssm_dt_bias,
           ssm_D, ssm_norm_w, ret_decay_logit, ret_norm_w, w_ssd_proj, w_ret_proj, w_out, norm2_w,
           w_route_group, b_route_group, w_route_expert, b_route_expert, w_exp_gate, w_exp_up,
           w_exp_down, final_norm_w):
    batch, n_lat, d = x.shape
    n_ctx = ctx.shape[1]
    depth = w_mod.shape[0]
    assert batch == 1 and d == D_MODEL and w_in.shape[-1] == MAIN_W + DT_W
    t = n_ctx + n_lat
    assert t % OUT_ROW_BLOCK == 0 and n_ctx % TOK_BLOCK == 0 and n_lat % TOK_BLOCK == 0

    w_nat, w_dt = _wprep(jnp.swapaxes(w_in, 1, 2))
    w_ssd_b = w_ssd_proj.astype(BF16)
    w_ret_b = w_ret_proj.astype(BF16)
    w_out_b = w_out.astype(BF16)
    w_route = _pad_lanes(jnp.concatenate([w_route_group, w_route_expert], axis=-1))
    b_route = _pad_lanes(jnp.concatenate([b_route_group, b_route_expert], axis=-1))[:, None, :]
    d_skip = jnp.repeat(ssm_D, SSM_HEADDIM, axis=-1)[:, None, :]
    dt_bias = _pad_lanes(ssm_dt_bias)
    a_log = _pad_lanes(ssm_A_log)
    ret_dl = _pad_lanes(ret_decay_logit)
    head_of_col = jnp.arange(D_MODEL, dtype=jnp.int32) // SSM_HEADDIM
    e_mat = ((jnp.arange(2 * LANES, dtype=jnp.int32) % LANES)[:, None]
             == head_of_col[None, :]).astype(BF16)
    cos_t, sin_t = _rope_tables(n_ctx, n_lat)

    cvec = jnp.concatenate([c, c_ctx[None, :], jnp.zeros((6, d), F32)], axis=0)
    mod = _modulation(cvec, w_mod, b_mod)

    n_tiles = (2 * t) // EXPERT_TILE + N_EXPERTS
    h = jnp.concatenate([ctx[0], x[0]], axis=0)
    x_sorted = jnp.zeros((n_tiles * EXPERT_TILE, d // 2), jnp.uint32)
    for l in range(depth):
        p_main, dt_raw = _inproj(h, mod, norm1_w[:, None, :], w_nat, w_dt, cos_t, sin_t, l, n_ctx)
        xbc_act = _conv(p_main, conv_w, conv_b[:, None, :], l, n_ctx)
        y_ssd = _ssd_scan(xbc_act, dt_raw, dt_bias[l], a_log[l], e_mat, n_ctx)
        y_ret = _ret_scan(p_main, ret_dl[l], n_ctx)
        merged = _merge(y_ssd, xbc_act, p_main, y_ret, d_skip, ssm_norm_w[:, None, :],
                        ret_norm_w[:, None, :], w_ssd_b, w_ret_b, l)
        h = _outproj(merged, w_out_b, h, mod, l, n_ctx)
        xn, meta, counts = _router(h, mod, norm2_w[:, None, :], w_route, b_route, l, n_ctx)
        pos, tile_meta = _tile_plan(meta, counts, n_tiles)
        x_sorted = _dispatch(pos, xn, x_sorted)
        y_sorted = _experts(tile_meta, x_sorted, w_exp_gate, w_exp_up, w_exp_down, l, n_tiles)
        h = _combine(pos, h, meta, mod, y_sorted, l, n_ctx)

    return _final_norm(h, final_norm_w[None, :], n_ctx)[None]
```

```python
import functools

import jax
import jax.numpy as jnp
from jax import lax
from jax.experimental import pallas as pl
from jax.experimental.pallas import tpu as pltpu

F32 = jnp.float32
BF16 = jnp.bfloat16
HIGHEST = lax.Precision.HIGHEST

D_MODEL = 2048
GRID_W = 64
CHUNK = 128
EPS = 1e-6

SSM_HEADDIM = 64
SSM_HEADS = 32
SSM_GROUPS = 4
SSM_HPG = 8
SSM_STATE = 128
SSM_CONV = 5
GROUP_W = SSM_HPG * SSM_HEADDIM
BC_W = SSM_GROUPS * SSM_STATE
XBC_D = D_MODEL + 2 * BC_W

RET_HEADS = 8
RET_DK = 256
ROPE_FREQS = 64
ROPE_BASE = 10000.0

N_GROUPS_E = 4
EXPERTS_PER_GROUP = 8
N_EXPERTS = 32
D_EXPERT = 512

LANES = 128
MAIN_W = 7 * D_MODEL + XBC_D
XBC_COL0 = 7 * D_MODEL
DT_W = SSM_HEADS
DT_COL0 = D_MODEL + XBC_D

INPROJ_ROW_BLOCKS = (1056, 768)
COL_BLOCK = 1024
OUT_ROW_BLOCK = 384
MERGE_ROW_BLOCK = 256
TOK_BLOCK = 256
DMA_GROUP = 8
EXPERT_TILE = 256
NEG_BIG = -1e30
VMEM_LIMIT = 56 * 1024 * 1024


def _params(sem, vmem=VMEM_LIMIT):
    return pltpu.CompilerParams(dimension_semantics=sem, vmem_limit_bytes=vmem)


def _sigmoid(v):
    return 0.5 * jnp.tanh(0.5 * v) + 0.5


def _silu(v):
    return v * _sigmoid(v)


def _softplus(v):
    return jnp.maximum(v, 0.0) + jnp.log1p(jnp.exp(-jnp.abs(v)))


def _split_bf16(v):
    hi = v.astype(BF16)
    return hi, (v - hi.astype(F32)).astype(BF16)


def _dot3(x, w):
    n = w.shape[1]
    xh, xl = _split_bf16(x)
    wh, wl = _split_bf16(w)
    both = jnp.dot(xh, jnp.concatenate([wh, wl], axis=1), preferred_element_type=F32)
    return both[:, :n] + both[:, n:] + jnp.dot(xl, wh, preferred_element_type=F32)


def _modulated_norm(x, nw, sh_ref, sc_ref, row0, n_ctx):
    ms = jnp.mean(x * x, axis=-1, keepdims=True)
    y = x * lax.rsqrt(ms + EPS) * nw
    row = row0 + lax.broadcasted_iota(jnp.int32, (x.shape[0], 1), 0)
    is_ctx = row < n_ctx
    sc = jnp.where(is_ctx, sc_ref[1:2, :], sc_ref[0:1, :])
    sh = jnp.where(is_ctx, sh_ref[1:2, :], sh_ref[0:1, :])
    return y * (1.0 + sc) + sh


def _mod_kernel(a_ref, w_ref, b_ref, o_ref):
    xh, xl = _split_bf16(_silu(a_ref[...]))
    wh, wl = _split_bf16(w_ref[...])
    o_ref[...] = (jnp.dot(xh, wh, preferred_element_type=F32)
                  + jnp.dot(xl, wh, preferred_element_type=F32)
                  + jnp.dot(xh, wl, preferred_element_type=F32)) + b_ref[...]


def _modulation(cvec, w_mod, b_mod):
    depth, d, n = w_mod.shape
    bn = 1024
    return pl.pallas_call(
        _mod_kernel,
        grid=(depth, n // bn),
        in_specs=[pl.BlockSpec((8, d), lambda l, j: (0, 0)),
                  pl.BlockSpec((None, d, bn), lambda l, j: (l, 0, j)),
                  pl.BlockSpec((None, 1, bn), lambda l, j: (l, 0, j))],
        out_specs=pl.BlockSpec((None, 8, bn), lambda l, j: (l, 0, j)),
        out_shape=jax.ShapeDtypeStruct((depth, 8, n), F32),
        compiler_params=_params(("arbitrary", "arbitrary")),
        name="modulation",
    )(cvec, w_mod, b_mod.reshape(depth, 1, n))


def _dot3_nt(x, wt):
    nt = (((1,), (1,)), ((), ()))
    n = wt.shape[0]
    xh, xl = _split_bf16(x)
    wh, wl = _split_bf16(wt)
    both = lax.dot_general(xh, jnp.concatenate([wh, wl], axis=0), nt, preferred_element_type=F32)
    return both[:, :n] + both[:, n:] + lax.dot_general(xl, wh, nt, preferred_element_type=F32)


def _wprep_kernel(w_hbm, o_ref, dt_ref, buf_s, dtb_s, sem, dsem, *, bn, n_aligned, nj):
    l = pl.program_id(0)
    n = pl.program_id(1)
    step = l * nj + n
    total = pl.num_programs(0) * nj

    def fetch(ll, nn, slot):
        row0 = pl.multiple_of(nn * bn + jnp.where(nn >= n_aligned, DT_W, 0), 8)
        return pltpu.make_async_copy(w_hbm.at[ll, pl.ds(row0, bn), :], buf_s.at[slot], sem.at[slot])

    @pl.when(step == 0)
    def _():
        fetch(l, n, 0).start()

    @pl.when(step + 1 < total)
    def _():
        wrap = n + 1 == nj
        fetch(jnp.where(wrap, l + 1, l), jnp.where(wrap, 0, n + 1), (step + 1) % 2).start()

    fetch(l, n, step % 2).wait()
    o_ref[...] = buf_s[step % 2].astype(BF16)

    @pl.when(n == 0)
    def _():
        cp = pltpu.make_async_copy(w_hbm.at[l, pl.ds(DT_COL0, DT_W), :], dtb_s, dsem)
        cp.start()
        cp.wait()
        dt_ref[0:DT_W, :] = dtb_s[...]
        dt_ref[DT_W:, :] = jnp.zeros((LANES - DT_W, dt_ref.shape[1]), F32)


def _wprep(w_in_t):
    depth, _, d = w_in_t.shape
    bn = COL_BLOCK
    nj = MAIN_W // bn
    return pl.pallas_call(
        functools.partial(_wprep_kernel, bn=bn, n_aligned=DT_COL0 // bn, nj=nj),
        grid=(depth, nj),
        in_specs=[pl.BlockSpec(memory_space=pl.ANY)],
        out_specs=[pl.BlockSpec((None, bn, d), lambda l, n: (l, n, 0)),
                   pl.BlockSpec((None, LANES, d), lambda l, n: (l, 0, 0))],
        out_shape=[jax.ShapeDtypeStruct((depth, MAIN_W, d), BF16),
                   jax.ShapeDtypeStruct((depth, LANES, d), F32)],
        scratch_shapes=[pltpu.VMEM((2, bn, d), F32), pltpu.VMEM((DT_W, d), F32),
                        pltpu.SemaphoreType.DMA((2,)), pltpu.SemaphoreType.DMA(())],
        compiler_params=_params(("arbitrary", "arbitrary")),
        name="wprep",
    )(w_in_t)


def _inproj_kernel(h_ref, sh_ref, sc_ref, nw_ref, w_ref, wdt_ref, cos_ref, sin_ref,
                   o_ref, dt_ref, xn_s, *, n_ctx, bm, bn):
    i = pl.program_id(0)
    j = pl.program_id(1)

    @pl.when(j == 0)
    def _():
        xn = _modulated_norm(h_ref[...], nw_ref[...], sh_ref, sc_ref, i * bm, n_ctx)
        xn_s[...] = xn.astype(BF16)
        dt_ref[...] = _dot3_nt(xn, wdt_ref[...])

    acc = lax.dot_general(xn_s[...], w_ref[...], (((1,), (1,)), ((), ())),
                          preferred_element_type=F32)
    q_lo = DT_COL0 // bn
    k_lo = (DT_COL0 + D_MODEL) // bn
    k_hi = (DT_COL0 + 2 * D_MODEL) // bn
    is_rope = (j >= q_lo) & (j < k_hi)

    @pl.when(is_rope)
    def _():
        scale = jnp.where(j >= k_lo, RET_DK ** -0.5, 1.0).astype(F32)
        for gi in range(bn // LANES):
            u = acc[:, gi * LANES:(gi + 1) * LANES]
            half = gi % 2
            c = cos_ref[:, half * LANES:(half + 1) * LANES]
            s = sin_ref[:, half * LANES:(half + 1) * LANES]
            r = (u * c + pltpu.roll(u, LANES // 2, 1) * s) * scale
            o_ref[:, gi * LANES:(gi + 1) * LANES] = r.astype(BF16)

    @pl.when(jnp.logical_not(is_rope))
    def _():
        o_ref[...] = acc.astype(BF16)


def _inproj(h, mod, norm_w, w_nat, w_dt, cos_t, sin_t, l, n_ctx):
    t, d = h.shape
    bm = next(b for b in INPROJ_ROW_BLOCKS if t % b == 0)
    bn = COL_BLOCK
    nj = MAIN_W // bn
    z_blocks = D_MODEL // bn
    xbc_blocks = XBC_D // bn

    def out_block(j):
        return jnp.where(j < z_blocks, j,
                         jnp.where(j < z_blocks + xbc_blocks, j + (nj - z_blocks - xbc_blocks),
                                   j - xbc_blocks))

    once = pl.Buffered(1)
    kern = functools.partial(_inproj_kernel, n_ctx=n_ctx, bm=bm, bn=bn)
    return pl.pallas_call(
        kern,
        grid=(t // bm, nj),
        in_specs=[pl.BlockSpec((bm, d), lambda i, j: (i, 0)),
                  pl.BlockSpec((None, 8, d), lambda i, j: (l, 0, 0)),
                  pl.BlockSpec((None, 8, d), lambda i, j: (l, 0, 1)),
                  pl.BlockSpec((None, 1, d), lambda i, j: (l, 0, 0)),
                  pl.BlockSpec((None, bn, d), lambda i, j: (l, j, 0)),
                  pl.BlockSpec((None, LANES, d), lambda i, j: (l, 0, 0)),
                  pl.BlockSpec((bm, 2 * LANES), lambda i, j: (i, 0), pipeline_mode=once),
                  pl.BlockSpec((bm, 2 * LANES), lambda i, j: (i, 0), pipeline_mode=once)],
        out_specs=[pl.BlockSpec((bm, bn), lambda i, j: (i, out_block(j))),
                   pl.BlockSpec((bm, LANES), lambda i, j: (i, 0))],
        out_shape=[jax.ShapeDtypeStruct((t, MAIN_W), BF16),
                   jax.ShapeDtypeStruct((t, LANES), F32)],
        scratch_shapes=[pltpu.VMEM((bm, d), BF16)],
        compiler_params=_params(("arbitrary", "arbitrary")),
        name="inproj",
    )(h, mod, mod, norm_w, w_nat, w_dt, cos_t, sin_t)


def _conv_kernel(x_ref, w_ref, b_ref, o_ref, pad_s, *, n_ctx, n_lat, rows):
    halo = 8
    zeros = jnp.zeros((halo, LANES), F32)
    c0 = halo
    x0 = 2 * halo + n_ctx
    pad_s[0:halo, :] = zeros
    pad_s[c0:c0 + n_ctx, :] = x_ref[0:n_ctx, :].astype(F32)
    pad_s[c0 + n_ctx:x0, :] = zeros
    pad_s[x0:x0 + n_lat, :] = x_ref[n_ctx:n_ctx + n_lat, :].astype(F32)
    pad_s[x0 + n_lat:x0 + n_lat + halo, :] = zeros
    w = w_ref[...]
    b = b_ref[...]

    def segment(src0, dst0, n):
        def body(s, carry):
            r0 = pl.multiple_of(s * rows, rows)
            acc = jnp.broadcast_to(b, (rows, LANES))
            for kk in range(SSM_CONV):
                acc = acc + w[kk:kk + 1, :] * pad_s[pl.ds(src0 + r0 + kk - SSM_CONV // 2, rows), :]
            o_ref[pl.ds(dst0 + r0, rows), :] = _silu(acc).astype(BF16)
            return carry
        lax.fori_loop(0, n // rows, body, 0)

    segment(c0, 0, n_ctx)
    segment(x0, n_ctx, n_lat)


def _conv(p_main, conv_w, conv_b, l, n_ctx):
    t = p_main.shape[0]
    n_lat = t - n_ctx
    col0 = XBC_COL0 // LANES
    kern = functools.partial(_conv_kernel, n_ctx=n_ctx, n_lat=n_lat, rows=256)
    return pl.pallas_call(
        kern,
        grid=(XBC_D // LANES,),
        in_specs=[pl.BlockSpec((t, LANES), lambda c: (0, col0 + c)),
                  pl.BlockSpec((None, SSM_CONV, LANES), lambda c: (l, 0, c)),
                  pl.BlockSpec((None, 1, LANES), lambda c: (l, 0, c))],
        out_specs=pl.BlockSpec((t, LANES), lambda c: (0, c)),
        out_shape=jax.ShapeDtypeStruct((t, XBC_D), BF16),
        scratch_shapes=[pltpu.VMEM((t + 24, LANES), F32)],
        compiler_params=_params(("arbitrary",)),
        name="conv_silu",
    )(p_main, conv_w, conv_b)


def _chunk_index(s, nc_ctx, nc, rev):
    if not rev:
        return s
    return jnp.where(s < nc_ctx, nc_ctx - 1 - s, nc - 1 - (s - nc_ctx))


def _split_dot(v, e2):
    hi, lo = _split_bf16(v)
    return jnp.dot(jnp.concatenate([hi, lo], axis=1), e2, preferred_element_type=F32)


def _ssd_kernel(xs_f, b_f, c_f, dt_f, xs_b, b_b, c_b, dt_b, bias_ref, alog_ref, e_ref,
                of_ref, ob_ref, h_s):
    @pl.when(pl.program_id(0) == 0)
    def _():
        h_s[...] = jnp.zeros_like(h_s)

    e = e_ref[...]
    _ssd_chunk(xs_f, b_f, c_f, dt_f, bias_ref[0:1, :], alog_ref[0:1, :], e, of_ref, h_s.at[0], False)
    _ssd_chunk(xs_b, b_b, c_b, dt_b, bias_ref[1:2, :], alog_ref[1:2, :], e, ob_ref, h_s.at[1], True)


def _ssd_chunk(xs_ref, b_ref, c_ref, dt_ref, bias, alog, e, o_ref, h_s, rev):
    q = CHUNK
    ii = lax.broadcasted_iota(jnp.int32, (q, q), 0)
    jj = lax.broadcasted_iota(jnp.int32, (q, q), 1)
    seen = (jj >= ii) if rev else (jj <= ii)
    tri = jnp.where(seen, 1.0, 0.0).astype(F32)

    dt = _softplus(dt_ref[...] + bias)
    a = dt * (-jnp.exp(alog))
    cum = jnp.dot(tri, a, precision=HIGHEST, preferred_element_type=F32)
    cum_t = cum.T
    dt_t = dt.T
    total = cum[0:1, :] if rev else cum[q - 1:q, :]
    dfs = jnp.exp(cum)
    w_end = dt * jnp.exp(total - cum)
    cdec = jnp.broadcast_to(jnp.exp(total), (8, LANES))
    expanded = _split_dot(jnp.concatenate([w_end, dfs, cdec], axis=0), e)
    w_end_x = expanded[0:q]
    dfs_x = expanded[q:2 * q]
    cdec_x = expanded[2 * q:2 * q + 1]

    xs = xs_ref[...]
    xw = (xs.astype(F32) * w_end_x).astype(BF16)
    first_head = jj < SSM_HEADDIM

    for g in range(SSM_GROUPS):
        cg = c_ref[:, g * SSM_STATE:(g + 1) * SSM_STATE]
        bg = b_ref[:, g * SSM_STATE:(g + 1) * SSM_STATE]
        gs = slice(g * GROUP_W, (g + 1) * GROUP_W)
        cb = lax.dot_general(cg, bg, (((1,), (1,)), ((), ())), preferred_element_type=F32)
        h_prev = h_s[:, gs]
        y_off = jnp.dot(cg, h_prev.astype(BF16), preferred_element_type=F32) * dfs_x[:, gs]
        new = lax.dot_general(bg, xw[:, gs], (((0,), (0,)), ((), ())), preferred_element_type=F32)
        pieces = []
        for hh in range(0, SSM_HPG, 2):
            ms = []
            for hd in (g * SSM_HPG + hh, g * SSM_HPG + hh + 1):
                seg = cum[:, hd:hd + 1] - cum_t[hd:hd + 1, :]
                m = jnp.exp(jnp.where(seen, seg, NEG_BIG)) * cb * dt_t[hd:hd + 1, :]
                ms.append(m.astype(BF16))
            c0 = (g * SSM_HPG + hh) * SSM_HEADDIM
            xp = xs[:, c0:c0 + LANES]
            zero = jnp.zeros_like(xp)
            rhs = jnp.concatenate([jnp.where(first_head, xp, zero), jnp.where(first_head, zero, xp)],
                                  axis=0)
            pieces.append(jnp.dot(jnp.concatenate(ms, axis=1), rhs, preferred_element_type=F32))
        y = jnp.concatenate(pieces, axis=1) + y_off
        o_ref[:, gs] = y.astype(o_ref.dtype)
        h_s[:, gs] = h_prev * cdec_x[:, gs] + new


def _scan_specs(n_ctx, t, col_blocks, width):
    nc, nc_ctx = t // CHUNK, n_ctx // CHUNK
    specs = []
    for rev in (False, True):
        cidx = functools.partial(_chunk_index, nc_ctx=nc_ctx, nc=nc, rev=rev)
        specs.append([pl.BlockSpec((CHUNK, w), lambda s, c=c, cidx=cidx: (cidx(s), c))
                      for c, w in zip(col_blocks, width)])
    return specs


def _ssd_scan(xbc_act, dt_raw, dt_bias, a_log, e_mat, n_ctx):
    t = xbc_act.shape[0]
    bc0 = D_MODEL // BC_W
    fwd, bwd = _scan_specs(n_ctx, t, (0, bc0, bc0 + 1, 0), (D_MODEL, BC_W, BC_W, LANES))
    out_f, out_b = _scan_specs(n_ctx, t, (0,), (D_MODEL,))
    const = [pl.BlockSpec((2, LANES), lambda s: (0, 0)),
             pl.BlockSpec((2, LANES), lambda s: (0, 0)),
             pl.BlockSpec((2 * LANES, D_MODEL), lambda s: (0, 0))]
    seqs = (xbc_act, xbc_act, xbc_act, dt_raw)
    return pl.pallas_call(
        _ssd_kernel,
        grid=(t // CHUNK,),
        in_specs=fwd + bwd + const,
        out_specs=out_f + out_b,
        out_shape=[jax.ShapeDtypeStruct((t, D_MODEL), BF16)] * 2,
        scratch_shapes=[pltpu.VMEM((2, SSM_STATE, D_MODEL), F32)],
        compiler_params=_params(("arbitrary",)),
        name="ssd_scan",
    )(*seqs, *seqs, dt_bias, a_log, e_mat)


def _ret_kernel(q_f, k_f, v_f, q_b, k_b, v_b, dl_ref, of_ref, ob_ref, r_s, dm_s, kd_s, qd_s, cd_s):
    qn = CHUNK

    @pl.when(pl.program_id(0) == 0)
    def _():
        lg_both = -_softplus(-dl_ref[...])
        cd_s[...] = jnp.exp(float(qn) * lg_both)
        r_s[...] = jnp.zeros_like(r_s)
        ii = lax.broadcasted_iota(jnp.int32, (qn, qn), 0)
        jj = lax.broadcasted_iota(jnp.int32, (qn, qn), 1)
        for di, rev in enumerate((False, True)):
            diff = (jj - ii) if rev else (ii - jj)
            dpos = jnp.maximum(diff, 0).astype(F32)
            k_exp = (ii if rev else (qn - 1 - ii)).astype(F32)
            q_exp = ((qn - ii) if rev else (ii + 1)).astype(F32)
            for hd in range(RET_HEADS):
                lg = lg_both[di:di + 1, hd:hd + 1]
                dm_s[di, hd] = jnp.where(diff >= 0, jnp.exp(dpos * lg), 0.0)
                kd_s[di, hd] = jnp.exp(k_exp * lg)
                qd_s[di, hd] = jnp.exp(q_exp * lg)

    _ret_chunk(q_f, k_f, v_f, cd_s[0:1, :], of_ref, r_s.at[0], dm_s.at[0], kd_s.at[0], qd_s.at[0])
    _ret_chunk(q_b, k_b, v_b, cd_s[1:2, :], ob_ref, r_s.at[1], dm_s.at[1], kd_s.at[1], qd_s.at[1])


def _ret_chunk(q_ref, k_ref, v_ref, chunk_decay, o_ref, r_s, dm_s, kd_s, qd_s):
    for hd in range(RET_HEADS):
        hs = slice(hd * RET_DK, (hd + 1) * RET_DK)
        qh = q_ref[:, hs]
        kh = k_ref[:, hs]
        vh = v_ref[:, hs]
        sc = lax.dot_general(qh, kh, (((1,), (1,)), ((), ())), preferred_element_type=F32)
        y = jnp.dot((sc * dm_s[hd]).astype(BF16), vh, preferred_element_type=F32)
        r_prev = r_s[hd]
        qd = qd_s[hd]
        y_int = jnp.dot(qh, r_prev.astype(BF16), preferred_element_type=F32)
        y = y + y_int * jnp.concatenate([qd, qd], axis=1)
        kd = kd_s[hd]
        kdec = (kh.astype(F32) * jnp.concatenate([kd, kd], axis=1)).astype(BF16)
        kv = lax.dot_general(kdec, vh, (((0,), (0,)), ((), ())), preferred_element_type=F32)
        r_s[hd] = r_prev * chunk_decay[:, hd:hd + 1] + kv
        o_ref[:, hs] = y.astype(o_ref.dtype)


def _ret_scan(p_main, decay_logit, n_ctx):
    t = p_main.shape[0]
    fwd, bwd = _scan_specs(n_ctx, t, (1, 2, 3), (D_MODEL,) * 3)
    out_f, out_b = _scan_specs(n_ctx, t, (0,), (D_MODEL,))
    seqs = (p_main, p_main, p_main)
    decay_tables = pltpu.VMEM((2, RET_HEADS, CHUNK, CHUNK), F32)
    return pl.pallas_call(
        _ret_kernel,
        grid=(t // CHUNK,),
        in_specs=fwd + bwd + [pl.BlockSpec((2, LANES), lambda s: (0, 0))],
        out_specs=out_f + out_b,
        out_shape=[jax.ShapeDtypeStruct((t, D_MODEL), BF16)] * 2,
        scratch_shapes=[pltpu.VMEM((2, RET_HEADS, RET_DK, RET_DK), F32),
                        decay_tables, decay_tables, decay_tables, pltpu.VMEM((2, LANES), F32)],
        compiler_params=_params(("arbitrary",)),
        name="ret_scan",
    )(*seqs, *seqs, decay_logit)


def _merge_kernel(ysf_ref, ysb_ref, xs_ref, z_ref, yrf_ref, yrb_ref, g_ref, gs_ref, gr_ref,
                  dsk_ref, snw_ref, rnw_ref, ws_ref, wr_ref, o_ref):
    y = (ysf_ref[...].astype(F32) + ysb_ref[...].astype(F32)
         + dsk_ref[...] * xs_ref[...].astype(F32))
    y = y * _silu(z_ref[...].astype(F32))
    ms = jnp.mean(y * y, axis=-1, keepdims=True)
    a = (y * lax.rsqrt(ms + EPS) * snw_ref[...]).astype(BF16)
    merged = _sigmoid(gs_ref[...].astype(F32)) * jnp.dot(a, ws_ref[...],
                                                         preferred_element_type=F32)
    gate = _silu(g_ref[...].astype(F32)) * rnw_ref[...]
    parts = []
    for hd in range(RET_HEADS):
        hs = slice(hd * RET_DK, (hd + 1) * RET_DK)
        r = yrf_ref[:, hs].astype(F32) + yrb_ref[:, hs].astype(F32)
        ms = jnp.mean(r * r, axis=-1, keepdims=True)
        parts.append((r * lax.rsqrt(ms + EPS) * gate[:, hs]).astype(BF16))
    b = jnp.concatenate(parts, axis=1)
    merged = merged + _sigmoid(gr_ref[...].astype(F32)) * jnp.dot(
        b, wr_ref[...], preferred_element_type=F32)
    o_ref[...] = merged.astype(BF16)


def _merge(y_ssd, xbc_act, p_main, y_ret, d_skip, ssm_norm_w, ret_norm_w, w_ssd, w_ret, l):
    t, d = xbc_act.shape[0], D_MODEL
    bm = MERGE_ROW_BLOCK
    once = pl.Buffered(1)
    return pl.pallas_call(
        _merge_kernel,
        grid=(t // bm,),
        in_specs=[pl.BlockSpec((bm, d), lambda i: (i, 0)),
                  pl.BlockSpec((bm, d), lambda i: (i, 0)),
                  pl.BlockSpec((bm, d), lambda i: (i, 0)),
                  pl.BlockSpec((bm, d), lambda i: (i, 0)),
                  pl.BlockSpec((bm, d), lambda i: (i, 0)),
                  pl.BlockSpec((bm, d), lambda i: (i, 0)),
                  pl.BlockSpec((bm, d), lambda i: (i, 4)),
                  pl.BlockSpec((bm, d), lambda i: (i, 5)),
                  pl.BlockSpec((bm, d), lambda i: (i, 6)),
                  pl.BlockSpec((None, 1, d), lambda i: (l, 0, 0)),
                  pl.BlockSpec((None, 1, d), lambda i: (l, 0, 0)),
                  pl.BlockSpec((None, 1, d), lambda i: (l, 0, 0)),
                  pl.BlockSpec((None, d, d), lambda i: (l, 0, 0), pipeline_mode=once),
                  pl.BlockSpec((None, d, d), lambda i: (l, 0, 0), pipeline_mode=once)],
        out_specs=pl.BlockSpec((bm, d), lambda i: (i, 0)),
        out_shape=jax.ShapeDtypeStruct((t, d), BF16),
        compiler_params=_params(("arbitrary",)),
        name="merge",
    )(*y_ssd, xbc_act, p_main, *y_ret, p_main, p_main, p_main, d_skip, ssm_norm_w, ret_norm_w,
      w_ssd, w_ret)


def _pack_bf16_pairs(v):
    n = v.shape[1] // 2
    bits = lax.bitcast_convert_type(v.astype(BF16).astype(F32), jnp.uint32)
    return bits[:, :n] | (bits[:, n:] >> 16)


def _unpack_bf16_pairs(p):
    hi = lax.bitcast_convert_type(p & jnp.uint32(0xFFFF0000), F32)
    lo = lax.bitcast_convert_type(p << 16, F32)
    return jnp.concatenate([hi, lo], axis=1).astype(BF16)


def _route_rows(x, i, sh_ref, sc_ref, nw_ref, wr_ref, br_ref, xn_ref, meta_ref, cnt_ref,
                carry_s, *, n_ctx, bm):
    @pl.when(i == 0)
    def _():
        carry_s[...] = jnp.zeros_like(carry_s)

    xn = _modulated_norm(x, nw_ref[...], sh_ref, sc_ref, i * bm, n_ctx)
    xn_ref[...] = _pack_bf16_pairs(xn)
    logits = _dot3(xn, wr_ref[...]) + br_ref[...]
    lane = lax.broadcasted_iota(jnp.int32, (bm, LANES), 1)
    far = jnp.int32(4 * LANES)

    glog = jnp.where(lane < N_GROUPS_E, logits, NEG_BIG)
    gmax = jnp.max(glog, axis=1, keepdims=True)
    gidx = jnp.min(jnp.where(glog == gmax, lane, far), axis=1, keepdims=True)
    p_top = 1.0 / jnp.sum(jnp.exp(glog - gmax), axis=1, keepdims=True)

    in_group = ((lane >= N_GROUPS_E) & (lane < N_GROUPS_E + N_EXPERTS)
                & (((lane - N_GROUPS_E) >> 3) == gidx))
    elog = jnp.where(in_group, logits, NEG_BIG)
    m1 = jnp.max(elog, axis=1, keepdims=True)
    i1 = jnp.min(jnp.where(elog == m1, lane, far), axis=1, keepdims=True)
    elog2 = jnp.where(lane == i1, NEG_BIG, elog)
    m2 = jnp.max(elog2, axis=1, keepdims=True)
    i2 = jnp.min(jnp.where(elog2 == m2, lane, far), axis=1, keepdims=True)
    e2 = jnp.exp(m2 - m1)
    w1 = p_top / (1.0 + e2)
    w2 = p_top * e2 / (1.0 + e2)
    ex1 = i1 - N_GROUPS_E
    ex2 = i2 - N_GROUPS_E

    onehot = jnp.where((lane == ex1) | (lane == ex2), 1.0, 0.0).astype(F32)
    ri = lax.broadcasted_iota(jnp.int32, (bm, bm), 0)
    ci = lax.broadcasted_iota(jnp.int32, (bm, bm), 1)
    earlier = jnp.where(ci < ri, 1.0, 0.0).astype(BF16)
    before = jnp.dot(earlier, onehot.astype(BF16), preferred_element_type=F32) + carry_s[0:1, :]
    r1 = jnp.sum(jnp.where(lane == ex1, before, 0.0), axis=1, keepdims=True)
    r2 = jnp.sum(jnp.where(lane == ex2, before, 0.0), axis=1, keepdims=True)
    carry = carry_s[...] + jnp.sum(onehot, axis=0, keepdims=True)
    carry_s[...] = carry
    cnt_ref[...] = carry

    meta = jnp.where(lane == 0, ex1.astype(F32), 0.0)
    meta = jnp.where(lane == 1, ex2.astype(F32), meta)
    meta = jnp.where(lane == 2, w1, meta)
    meta = jnp.where(lane == 3, w2, meta)
    meta = jnp.where(lane == 4, r1, meta)
    meta = jnp.where(lane == 5, r2, meta)
    meta_ref[...] = meta


def _outproj_kernel(m_ref, w_ref, h_ref, g_ref, sh_ref, sc_ref, nw_ref, wr_ref, br_ref,
                    o_ref, xn_ref, meta_ref, cnt_ref, carry_s, *, n_ctx, bm):
    i = pl.program_id(0)
    row = i * bm + lax.broadcasted_iota(jnp.int32, (bm, 1), 0)
    gate = jnp.where(row < n_ctx, g_ref[1:2, :], g_ref[0:1, :])
    h_new = h_ref[...] + gate * jnp.dot(m_ref[...], w_ref[...], preferred_element_type=F32)
    o_ref[...] = h_new
    _route_rows(h_new, i, sh_ref, sc_ref, nw_ref, wr_ref, br_ref, xn_ref, meta_ref, cnt_ref,
                carry_s, n_ctx=n_ctx, bm=bm)


def _outproj_route(merged, w_out, h, mod, norm_w, w_route, b_route, l, n_ctx):
    t, d = h.shape
    bm = OUT_ROW_BLOCK
    return pl.pallas_call(
        functools.partial(_outproj_kernel, n_ctx=n_ctx, bm=bm),
        grid=(t // bm,),
        in_specs=[pl.BlockSpec((bm, d), lambda i: (i, 0)),
                  pl.BlockSpec((None, d, d), lambda i: (l, 0, 0), pipeline_mode=pl.Buffered(1)),
                  pl.BlockSpec((bm, d), lambda i: (i, 0)),
                  pl.BlockSpec((None, 8, d), lambda i: (l, 0, 2)),
                  pl.BlockSpec((None, 8, d), lambda i: (l, 0, 3)),
                  pl.BlockSpec((None, 8, d), lambda i: (l, 0, 4)),
                  pl.BlockSpec((None, 1, d), lambda i: (l, 0, 0)),
                  pl.BlockSpec((None, d, LANES), lambda i: (l, 0, 0)),
                  pl.BlockSpec((None, 1, LANES), lambda i: (l, 0, 0))],
        out_specs=[pl.BlockSpec((bm, d), lambda i: (i, 0)),
                   pl.BlockSpec((bm, d // 2), lambda i: (i, 0)),
                   pl.BlockSpec((bm, LANES), lambda i: (i, 0)),
                   pl.BlockSpec((8, LANES), lambda i: (0, 0))],
        out_shape=[jax.ShapeDtypeStruct((t, d), F32),
                   jax.ShapeDtypeStruct((t, d // 2), jnp.uint32),
                   jax.ShapeDtypeStruct((t, LANES), F32),
                   jax.ShapeDtypeStruct((8, LANES), F32)],
        scratch_shapes=[pltpu.VMEM((8, LANES), F32)],
        compiler_params=_params(("arbitrary",)),
        name="outproj_route",
    )(merged, w_out, h, mod, mod, mod, norm_w, w_route, b_route)


def _row_copy(src, src_row, dst, dst_row, sem):
    return pltpu.make_async_copy(src.at[pl.ds(src_row, 1), :], dst.at[pl.ds(dst_row, 1), :], sem)


def _dispatch_kernel(pos_ref, xn_ref, init_ref, o_ref, sem, *, bm, t):
    del init_ref
    base = pl.program_id(0) * bm

    def issue(g, carry):
        r0 = g * DMA_GROUP
        ps = [[pos_ref[k * t + base + r0 + u] for k in range(2)] for u in range(DMA_GROUP)]
        for u in range(DMA_GROUP):
            for k in range(2):
                _row_copy(xn_ref, r0 + u, o_ref, ps[u][k], sem).start(priority=k)
        return carry

    lax.fori_loop(0, bm // DMA_GROUP, issue, 0)
    for k in range(2):
        pltpu.make_async_copy(xn_ref, o_ref.at[pl.ds(0, bm), :], sem).wait()


def _dispatch(pos, xn, init):
    t, d = xn.shape
    n_rows = init.shape[0]
    bm = TOK_BLOCK
    return pl.pallas_call(
        functools.partial(_dispatch_kernel, bm=bm, t=t),
        grid_spec=pltpu.PrefetchScalarGridSpec(
            num_scalar_prefetch=1,
            grid=(t // bm,),
            in_specs=[pl.BlockSpec((bm, d), lambda i, pos: (i, 0)),
                      pl.BlockSpec(memory_space=pl.ANY)],
            out_specs=pl.BlockSpec(memory_space=pl.ANY),
            scratch_shapes=[pltpu.SemaphoreType.DMA(())]),
        out_shape=jax.ShapeDtypeStruct((n_rows, d), xn.dtype),
        input_output_aliases={2: 0},
        compiler_params=_params(("arbitrary",)),
        name="dispatch",
    )(pos, xn, init)


def _expert_kernel(te_ref, nx_ref, xi_ref, first_ref, valid_ref, x_ref, wg_hbm, wu_hbm, wd_hbm,
                   o_ref, lg_s, lu_s, ld_s, wg_s, wu_s, wd_s, sem, *, l):
    i = pl.program_id(0)

    def fetch(e):
        return (pltpu.make_async_copy(wg_hbm.at[l, e], lg_s, sem.at[0]),
                pltpu.make_async_copy(wu_hbm.at[l, e], lu_s, sem.at[1]),
                pltpu.make_async_copy(wd_hbm.at[l, e], ld_s, sem.at[2]))

    @pl.when(i == 0)
    def _():
        for cp in fetch(te_ref[0]):
            cp.start()

    @pl.when(first_ref[i] == 1)
    def _():
        for cp in fetch(te_ref[i]):
            cp.wait()
        wg_s[...] = lg_s[...].astype(BF16)
        wu_s[...] = lu_s[...].astype(BF16)
        wd_s[...] = ld_s[...].astype(BF16)

        @pl.when(nx_ref[i] >= 0)
        def _():
            for cp in fetch(nx_ref[i]):
                cp.start()

    @pl.when(valid_ref[i] == 1)
    def _():
        x = _unpack_bf16_pairs(x_ref[...])
        gate = jnp.dot(x, wg_s[...], preferred_element_type=F32)
        up = jnp.dot(x, wu_s[...], preferred_element_type=F32)
        mid = (_silu(gate) * up).astype(BF16)
        o_ref[...] = jnp.dot(mid, wd_s[...], preferred_element_type=F32)

    @pl.when(valid_ref[i] == 0)
    def _():
        o_ref[...] = jnp.zeros_like(o_ref)


def _experts(tile_meta, x_sorted, w_gate, w_up, w_down, l, n_tiles):
    d = D_MODEL
    tm = EXPERT_TILE
    hbm = pl.BlockSpec(memory_space=pl.ANY)
    return pl.pallas_call(
        functools.partial(_expert_kernel, l=l),
        grid_spec=pltpu.PrefetchScalarGridSpec(
            num_scalar_prefetch=5,
            grid=(n_tiles,),
            in_specs=[pl.BlockSpec((tm, d // 2), lambda i, te, nx, xi, fi, va: (xi[i], 0)),
                      hbm, hbm, hbm],
            out_specs=pl.BlockSpec((tm, d), lambda i, te, nx, xi, fi, va: (i, 0)),
            scratch_shapes=[pltpu.VMEM((d, D_EXPERT), F32), pltpu.VMEM((d, D_EXPERT), F32),
                            pltpu.VMEM((D_EXPERT, d), F32),
                            pltpu.VMEM((d, D_EXPERT), BF16), pltpu.VMEM((d, D_EXPERT), BF16),
                            pltpu.VMEM((D_EXPERT, d), BF16),
                            pltpu.SemaphoreType.DMA((3,))]),
        out_shape=jax.ShapeDtypeStruct((n_tiles * tm, d), F32),
        compiler_params=_params(("arbitrary",)),
        name="experts",
    )(*tile_meta, x_sorted, w_gate, w_up, w_down)


def _combine_kernel(pos_ref, h_ref, meta_ref, g_ref, ys_ref, o_ref, buf_s, sem, *, n_ctx, bm, t):
    i = pl.program_id(0)
    n = pl.num_programs(0)

    def issue(blk, slot):
        base = blk * bm

        def body(g, carry):
            r0 = g * DMA_GROUP
            ps = [[pos_ref[k * t + base + r0 + u] for k in range(2)] for u in range(DMA_GROUP)]
            for u in range(DMA_GROUP):
                for k in range(2):
                    _row_copy(ys_ref, ps[u][k], buf_s.at[slot, k], r0 + u,
                              sem.at[slot]).start(priority=k)
            return carry

        lax.fori_loop(0, bm // DMA_GROUP, body, 0)

    @pl.when(i == 0)
    def _():
        issue(0, 0)

    @pl.when(i + 1 < n)
    def _():
        issue(i + 1, (i + 1) % 2)

    slot = i % 2
    for k in range(2):
        pltpu.make_async_copy(ys_ref.at[pl.ds(0, bm), :], buf_s.at[slot, k], sem.at[slot]).wait()
    meta = meta_ref[...]
    moe = meta[:, 2:3] * buf_s[slot, 0] + meta[:, 3:4] * buf_s[slot, 1]
    row = i * bm + lax.broadcasted_iota(jnp.int32, (bm, 1), 0)
    gate = jnp.where(row < n_ctx, g_ref[1:2, :], g_ref[0:1, :])
    o_ref[...] = h_ref[...] + gate * moe


def _combine(pos, h, meta, mod, y_sorted, l, n_ctx):
    t, d = h.shape
    bm = TOK_BLOCK
    return pl.pallas_call(
        functools.partial(_combine_kernel, n_ctx=n_ctx, bm=bm, t=t),
        grid_spec=pltpu.PrefetchScalarGridSpec(
            num_scalar_prefetch=1,
            grid=(t // bm,),
            in_specs=[pl.BlockSpec((bm, d), lambda i, pos: (i, 0)),
                      pl.BlockSpec((bm, LANES), lambda i, pos: (i, 0)),
                      pl.BlockSpec((None, 8, d), lambda i, pos: (l, 0, 5)),
                      pl.BlockSpec(memory_space=pl.ANY)],
            out_specs=pl.BlockSpec((bm, d), lambda i, pos: (i, 0)),
            scratch_shapes=[pltpu.VMEM((2, 2, bm, d), F32), pltpu.SemaphoreType.DMA((2,))]),
        out_shape=jax.ShapeDtypeStruct((t, d), F32),
        compiler_params=_params(("arbitrary",)),
        name="combine",
    )(pos, h, meta, mod, y_sorted)


def _tile_plan(meta, counts, n_tiles):
    tm = EXPERT_TILE
    cnt = counts[0, :N_EXPERTS].astype(jnp.int32)
    tiles_e = (cnt + tm - 1) // tm
    tile_end = jnp.cumsum(tiles_e)
    offset = (tile_end - tiles_e) * tm
    meta_t = meta[:, :8].T
    expert = meta_t[0:2].astype(jnp.int32)
    rank = meta_t[4:6].astype(jnp.int32)
    ids = jnp.arange(N_EXPERTS, dtype=jnp.int32)[:, None, None]
    base = jnp.sum(jnp.where(expert[None] == ids, offset[:, None, None], 0), axis=0)
    pos = (base + rank).reshape(-1)
    n_used = tile_end[-1]
    tid = jnp.arange(n_tiles, dtype=jnp.int32)
    valid = tid < n_used
    tsrc = jnp.minimum(tid, n_used - 1)
    te = jnp.minimum(jnp.searchsorted(tile_end, tsrc, side="right"), N_EXPERTS - 1).astype(jnp.int32)
    first = (valid & ((tid == 0) | (te != jnp.roll(te, 1)))).astype(jnp.int32)
    after = tile_end[te]
    nxt = jnp.where(after < n_used, te[jnp.minimum(after, n_tiles - 1)], -1).astype(jnp.int32)
    return pos, (te, nxt, tsrc.astype(jnp.int32), first, valid.astype(jnp.int32))


def _final_kernel(h_ref, w_ref, o_ref):
    x = h_ref[...]
    ms = jnp.mean(x * x, axis=-1, keepdims=True)
    o_ref[...] = x * lax.rsqrt(ms + EPS) * w_ref[...]


def _final_norm(h, w, n_ctx):
    t, d = h.shape
    bm = TOK_BLOCK
    skip = n_ctx // bm
    return pl.pallas_call(
        _final_kernel,
        grid=((t - n_ctx) // bm,),
        in_specs=[pl.BlockSpec((bm, d), lambda i: (i + skip, 0)),
                  pl.BlockSpec((1, d), lambda i: (0, 0))],
        out_specs=pl.BlockSpec((bm, d), lambda i: (i, 0)),
        out_shape=jax.ShapeDtypeStruct((t - n_ctx, d), F32),
        compiler_params=_params(("arbitrary",)),
        name="final_norm",
    )(h, w)


def _pad_lanes(v):
    return jnp.pad(v, [(0, 0)] * (v.ndim - 1) + [(0, LANES - v.shape[-1])])


def _rope_tables(n_ctx, n_lat):
    rows = n_lat // GRID_W
    row = jnp.broadcast_to(jnp.arange(rows, dtype=F32)[:, None], (rows, GRID_W)).reshape(n_lat)
    col = jnp.broadcast_to(jnp.arange(GRID_W, dtype=F32)[None, :], (rows, GRID_W)).reshape(n_lat)
    inv_freq = ROPE_BASE ** (-jnp.arange(ROPE_FREQS, dtype=F32) / ROPE_FREQS)
    ang_r = row[:, None] * inv_freq[None, :]
    ang_c = col[:, None] * inv_freq[None, :]
    cos_l = jnp.concatenate([jnp.cos(ang_r)] * 2 + [jnp.cos(ang_c)] * 2, axis=1)
    sin_l = jnp.concatenate([-jnp.sin(ang_r), jnp.sin(ang_r), -jnp.sin(ang_c), jnp.sin(ang_c)], axis=1)
    cos_t = jnp.concatenate([jnp.ones((n_ctx, 2 * LANES), F32), cos_l], axis=0)
    sin_t = jnp.concatenate([jnp.zeros((n_ctx, 2 * LANES), F32), sin_l], axis=0)
    return cos_t, sin_t


def kernel(x, c, ctx, c_ctx, w_mod, b_mod, norm1_w, w_in, conv_w, conv_b, ssm_A_log, ssm_dt_bias,
           ssm_D, ssm_norm_w, ret_decay_logit, ret_norm_w, w_ssd_proj, w_ret_proj, w_out, norm2_w,
           w_route_group, b_route_group, w_route_expert, b_route_expert, w_exp_gate, w_exp_up,
           w_exp_down, final_norm_w):
    batch, n_lat, d = x.shape
    n_ctx = ctx.shape[1]
    depth = w_mod.shape[0]
    assert batch == 1 and d == D_MODEL and w_in.shape[-1] == MAIN_W + DT_W
    t = n_ctx + n_lat
    assert t % OUT_ROW_BLOCK == 0 and n_ctx % TOK_BLOCK == 0 and n_lat % TOK_BLOCK == 0

    w_nat, w_dt = _wprep(jnp.swapaxes(w_in, 1, 2))
    w_ssd_b = w_ssd_proj.astype(BF16)
    w_ret_b = w_ret_proj.astype(BF16)
    w_out_b = w_out.astype(BF16)
    w_route = _pad_lanes(jnp.concatenate([w_route_group, w_route_expert], axis=-1))
    b_route = _pad_lanes(jnp.concatenate([b_route_group, b_route_expert], axis=-1))[:, None, :]
    d_skip = jnp.repeat(ssm_D, SSM_HEADDIM, axis=-1)[:, None, :]
    dt_bias = _pad_lanes(ssm_dt_bias)
    a_log = _pad_lanes(ssm_A_log)
    ret_dl = _pad_lanes(ret_decay_logit)
    head_of_col = jnp.arange(D_MODEL, dtype=jnp.int32) // SSM_HEADDIM
    e_mat = ((jnp.arange(2 * LANES, dtype=jnp.int32) % LANES)[:, None]
             == head_of_col[None, :]).astype(BF16)
    cos_t, sin_t = _rope_tables(n_ctx, n_lat)

    cvec = jnp.concatenate([c, c_ctx[None, :], jnp.zeros((6, d), F32)], axis=0)
    mod = _modulation(cvec, w_mod, b_mod)

    n_tiles = (2 * t) // EXPERT_TILE + N_EXPERTS
    h = jnp.concatenate([ctx[0], x[0]], axis=0)
    x_sorted = jnp.zeros((n_tiles * EXPERT_TILE, d // 2), jnp.uint32)
    for l in range(depth):
        p_main, dt_raw = _inproj(h, mod, norm1_w[:, None, :], w_nat, w_dt, cos_t, sin_t, l, n_ctx)
        xbc_act = _conv(p_main, conv_w, conv_b[:, None, :], l, n_ctx)
        y_ssd = _ssd_scan(xbc_act, dt_raw, dt_bias[l], a_log[l], e_mat, n_ctx)
        y_ret = _ret_scan(p_main, ret_dl[l], n_ctx)
        merged = _merge(y_ssd, xbc_act, p_main, y_ret, d_skip, ssm_norm_w[:, None, :],
                        ret_norm_w[:, None, :], w_ssd_b, w_ret_b, l)
        h, xn, meta, counts = _outproj_route(merged, w_out_b, h, mod, norm2_w[:, None, :], w_route,
                                             b_route, l, n_ctx)
        pos, tile_meta = _tile_plan(meta, counts, n_tiles)
        x_sorted = _dispatch(pos, xn, x_sorted)
        y_sorted = _experts(tile_meta, x_sorted, w_exp_gate, w_exp_up, w_exp_down, l, n_tiles)
        h = _combine(pos, h, meta, mod, y_sorted, l, n_ctx)

    return _final_norm(h, final_norm_w[None, :], n_ctx)[None]
```

```python
import functools

import jax
import jax.numpy as jnp
from jax import lax
from jax.experimental import pallas as pl
from jax.experimental.pallas import tpu as pltpu

F32 = jnp.float32
BF16 = jnp.bfloat16
HIGHEST = lax.Precision.HIGHEST

D_MODEL = 2048
GRID_W = 64
CHUNK = 128
EPS = 1e-6

SSM_HEADDIM = 64
SSM_HEADS = 32
SSM_GROUPS = 4
SSM_HPG = 8
SSM_STATE = 128
SSM_CONV = 5
GROUP_W = SSM_HPG * SSM_HEADDIM
BC_W = SSM_GROUPS * SSM_STATE
XBC_D = D_MODEL + 2 * BC_W

RET_HEADS = 8
RET_DK = 256
ROPE_FREQS = 64
ROPE_BASE = 10000.0

N_GROUPS_E = 4
EXPERTS_PER_GROUP = 8
N_EXPERTS = 32
D_EXPERT = 512

LANES = 128
MAIN_W = 7 * D_MODEL + XBC_D
XBC_COL0 = 7 * D_MODEL
DT_W = SSM_HEADS
DT_COL0 = D_MODEL + XBC_D

INPROJ_ROW_BLOCKS = (1056, 768)
COL_BLOCK = 1024
OUT_ROW_BLOCK = 384
MERGE_ROW_BLOCK = 256
TOK_BLOCK = 256
DMA_GROUP = 8
EXPERT_TILE = 256
NEG_BIG = -1e30
VMEM_LIMIT = 56 * 1024 * 1024


def _params(sem, vmem=VMEM_LIMIT):
    return pltpu.CompilerParams(dimension_semantics=sem, vmem_limit_bytes=vmem)


def _sigmoid(v):
    return 0.5 * jnp.tanh(0.5 * v) + 0.5


def _silu(v):
    return v * _sigmoid(v)


def _softplus(v):
    return jnp.maximum(v, 0.0) + jnp.log1p(jnp.exp(-jnp.abs(v)))


def _split_bf16(v):
    hi = v.astype(BF16)
    return hi, (v - hi.astype(F32)).astype(BF16)


def _dot3(x, w):
    n = w.shape[1]
    xh, xl = _split_bf16(x)
    wh, wl = _split_bf16(w)
    both = jnp.dot(xh, jnp.concatenate([wh, wl], axis=1), preferred_element_type=F32)
    return both[:, :n] + both[:, n:] + jnp.dot(xl, wh, preferred_element_type=F32)


def _modulated_norm(x, nw, sh_ref, sc_ref, row0, n_ctx):
    ms = jnp.mean(x * x, axis=-1, keepdims=True)
    y = x * lax.rsqrt(ms + EPS) * nw
    row = row0 + lax.broadcasted_iota(jnp.int32, (x.shape[0], 1), 0)
    is_ctx = row < n_ctx
    sc = jnp.where(is_ctx, sc_ref[1:2, :], sc_ref[0:1, :])
    sh = jnp.where(is_ctx, sh_ref[1:2, :], sh_ref[0:1, :])
    return y * (1.0 + sc) + sh


def _mod_kernel(a_ref, w_ref, b_ref, o_ref):
    xh, xl = _split_bf16(_silu(a_ref[...]))
    wh, wl = _split_bf16(w_ref[...])
    o_ref[...] = (jnp.dot(xh, wh, preferred_element_type=F32)
                  + jnp.dot(xl, wh, preferred_element_type=F32)
                  + jnp.dot(xh, wl, preferred_element_type=F32)) + b_ref[...]


def _modulation(cvec, w_mod, b_mod):
    depth, d, n = w_mod.shape
    bn = 1024
    return pl.pallas_call(
        _mod_kernel,
        grid=(depth, n // bn),
        in_specs=[pl.BlockSpec((8, d), lambda l, j: (0, 0)),
                  pl.BlockSpec((None, d, bn), lambda l, j: (l, 0, j)),
                  pl.BlockSpec((None, 1, bn), lambda l, j: (l, 0, j))],
        out_specs=pl.BlockSpec((None, 8, bn), lambda l, j: (l, 0, j)),
        out_shape=jax.ShapeDtypeStruct((depth, 8, n), F32),
        compiler_params=_params(("arbitrary", "arbitrary")),
        name="modulation",
    )(cvec, w_mod, b_mod.reshape(depth, 1, n))


def _dot3_nt(x, wt):
    nt = (((1,), (1,)), ((), ()))
    n = wt.shape[0]
    xh, xl = _split_bf16(x)
    wh, wl = _split_bf16(wt)
    both = lax.dot_general(xh, jnp.concatenate([wh, wl], axis=0), nt, preferred_element_type=F32)
    return both[:, :n] + both[:, n:] + lax.dot_general(xl, wh, nt, preferred_element_type=F32)


def _wprep_kernel(w_hbm, o_ref, dt_ref, buf_s, dtb_s, sem, dsem, *, bn, n_aligned, nj):
    l = pl.program_id(0)
    n = pl.program_id(1)
    step = l * nj + n
    total = pl.num_programs(0) * nj

    def fetch(ll, nn, slot):
        row0 = pl.multiple_of(nn * bn + jnp.where(nn >= n_aligned, DT_W, 0), 8)
        return pltpu.make_async_copy(w_hbm.at[ll, pl.ds(row0, bn), :], buf_s.at[slot], sem.at[slot])

    @pl.when(step == 0)
    def _():
        fetch(l, n, 0).start()

    @pl.when(step + 1 < total)
    def _():
        wrap = n + 1 == nj
        fetch(jnp.where(wrap, l + 1, l), jnp.where(wrap, 0, n + 1), (step + 1) % 2).start()

    fetch(l, n, step % 2).wait()
    o_ref[...] = buf_s[step % 2].astype(BF16)

    @pl.when(n == 0)
    def _():
        cp = pltpu.make_async_copy(w_hbm.at[l, pl.ds(DT_COL0, DT_W), :], dtb_s, dsem)
        cp.start()
        cp.wait()
        dt_ref[0:DT_W, :] = dtb_s[...]
        dt_ref[DT_W:, :] = jnp.zeros((LANES - DT_W, dt_ref.shape[1]), F32)


def _wprep(w_in_t):
    depth, _, d = w_in_t.shape
    bn = COL_BLOCK
    nj = MAIN_W // bn
    return pl.pallas_call(
        functools.partial(_wprep_kernel, bn=bn, n_aligned=DT_COL0 // bn, nj=nj),
        grid=(depth, nj),
        in_specs=[pl.BlockSpec(memory_space=pl.ANY)],
        out_specs=[pl.BlockSpec((None, bn, d), lambda l, n: (l, n, 0)),
                   pl.BlockSpec((None, LANES, d), lambda l, n: (l, 0, 0))],
        out_shape=[jax.ShapeDtypeStruct((depth, MAIN_W, d), BF16),
                   jax.ShapeDtypeStruct((depth, LANES, d), F32)],
        scratch_shapes=[pltpu.VMEM((2, bn, d), F32), pltpu.VMEM((DT_W, d), F32),
                        pltpu.SemaphoreType.DMA((2,)), pltpu.SemaphoreType.DMA(())],
        compiler_params=_params(("arbitrary", "arbitrary")),
        name="wprep",
    )(w_in_t)


def _inproj_kernel(h_ref, sh_ref, sc_ref, nw_ref, w_ref, wdt_ref, cos_ref, sin_ref,
                   o_ref, dt_ref, xn_s, *, n_ctx, bm, bn):
    i = pl.program_id(0)
    j = pl.program_id(1)

    @pl.when(j == 0)
    def _():
        xn = _modulated_norm(h_ref[...], nw_ref[...], sh_ref, sc_ref, i * bm, n_ctx)
        xn_s[...] = xn.astype(BF16)
        dt_ref[...] = _dot3_nt(xn, wdt_ref[...])

    acc = lax.dot_general(xn_s[...], w_ref[...], (((1,), (1,)), ((), ())),
                          preferred_element_type=F32)
    q_lo = DT_COL0 // bn
    k_lo = (DT_COL0 + D_MODEL) // bn
    k_hi = (DT_COL0 + 2 * D_MODEL) // bn
    is_rope = (j >= q_lo) & (j < k_hi)

    @pl.when(is_rope)
    def _():
        scale = jnp.where(j >= k_lo, RET_DK ** -0.5, 1.0).astype(F32)
        for gi in range(bn // LANES):
            u = acc[:, gi * LANES:(gi + 1) * LANES]
            half = gi % 2
            c = cos_ref[:, half * LANES:(half + 1) * LANES]
            s = sin_ref[:, half * LANES:(half + 1) * LANES]
            r = (u * c + pltpu.roll(u, LANES // 2, 1) * s) * scale
            o_ref[:, gi * LANES:(gi + 1) * LANES] = r.astype(BF16)

    @pl.when(jnp.logical_not(is_rope))
    def _():
        o_ref[...] = acc.astype(BF16)


def _inproj(h, mod, norm_w, w_nat, w_dt, cos_t, sin_t, l, n_ctx):
    t, d = h.shape
    bm = next(b for b in INPROJ_ROW_BLOCKS if t % b == 0)
    bn = COL_BLOCK
    nj = MAIN_W // bn
    z_blocks = D_MODEL // bn
    xbc_blocks = XBC_D // bn

    def out_block(j):
        return jnp.where(j < z_blocks, j,
                         jnp.where(j < z_blocks + xbc_blocks, j + (nj - z_blocks - xbc_blocks),
                                   j - xbc_blocks))

    once = pl.Buffered(1)
    kern = functools.partial(_inproj_kernel, n_ctx=n_ctx, bm=bm, bn=bn)
    return pl.pallas_call(
        kern,
        grid=(t // bm, nj),
        in_specs=[pl.BlockSpec((bm, d), lambda i, j: (i, 0)),
                  pl.BlockSpec((None, 8, d), lambda i, j: (l, 0, 0)),
                  pl.BlockSpec((None, 8, d), lambda i, j: (l, 0, 1)),
                  pl.BlockSpec((None, 1, d), lambda i, j: (l, 0, 0)),
                  pl.BlockSpec((None, bn, d), lambda i, j: (l, j, 0)),
                  pl.BlockSpec((None, LANES, d), lambda i, j: (l, 0, 0)),
                  pl.BlockSpec((bm, 2 * LANES), lambda i, j: (i, 0), pipeline_mode=once),
                  pl.BlockSpec((bm, 2 * LANES), lambda i, j: (i, 0), pipeline_mode=once)],
        out_specs=[pl.BlockSpec((bm, bn), lambda i, j: (i, out_block(j))),
                   pl.BlockSpec((bm, LANES), lambda i, j: (i, 0))],
        out_shape=[jax.ShapeDtypeStruct((t, MAIN_W), BF16),
                   jax.ShapeDtypeStruct((t, LANES), F32)],
        scratch_shapes=[pltpu.VMEM((bm, d), BF16)],
        compiler_params=_params(("arbitrary", "arbitrary")),
        name="inproj",
    )(h, mod, mod, norm_w, w_nat, w_dt, cos_t, sin_t)


def _conv_kernel(x_ref, w_ref, b_ref, o_ref, pad_s, *, n_ctx, n_lat, rows):
    halo = 8
    zeros = jnp.zeros((halo, LANES), F32)
    c0 = halo
    x0 = 2 * halo + n_ctx
    pad_s[0:halo, :] = zeros
    pad_s[c0:c0 + n_ctx, :] = x_ref[0:n_ctx, :].astype(F32)
    pad_s[c0 + n_ctx:x0, :] = zeros
    pad_s[x0:x0 + n_lat, :] = x_ref[n_ctx:n_ctx + n_lat, :].astype(F32)
    pad_s[x0 + n_lat:x0 + n_lat + halo, :] = zeros
    w = w_ref[...]
    b = b_ref[...]

    def segment(src0, dst0, n):
        def body(s, carry):
            r0 = pl.multiple_of(s * rows, rows)
            acc = jnp.broadcast_to(b, (rows, LANES))
            for kk in range(SSM_CONV):
                acc = acc + w[kk:kk + 1, :] * pad_s[pl.ds(src0 + r0 + kk - SSM_CONV // 2, rows), :]
            o_ref[pl.ds(dst0 + r0, rows), :] = _silu(acc).astype(BF16)
            return carry
        lax.fori_loop(0, n // rows, body, 0)

    segment(c0, 0, n_ctx)
    segment(x0, n_ctx, n_lat)


def _conv(p_main, conv_w, conv_b, l, n_ctx):
    t = p_main.shape[0]
    n_lat = t - n_ctx
    col0 = XBC_COL0 // LANES
    kern = functools.partial(_conv_kernel, n_ctx=n_ctx, n_lat=n_lat, rows=256)
    return pl.pallas_call(
        kern,
        grid=(XBC_D // LANES,),
        in_specs=[pl.BlockSpec((t, LANES), lambda c: (0, col0 + c)),
                  pl.BlockSpec((None, SSM_CONV, LANES), lambda c: (l, 0, c)),
                  pl.BlockSpec((None, 1, LANES), lambda c: (l, 0, c))],
        out_specs=pl.BlockSpec((t, LANES), lambda c: (0, c)),
        out_shape=jax.ShapeDtypeStruct((t, XBC_D), BF16),
        scratch_shapes=[pltpu.VMEM((t + 24, LANES), F32)],
        compiler_params=_params(("arbitrary",)),
        name="conv_silu",
    )(p_main, conv_w, conv_b)


def _chunk_index(s, nc_ctx, nc, rev):
    if not rev:
        return s
    return jnp.where(s < nc_ctx, nc_ctx - 1 - s, nc - 1 - (s - nc_ctx))


def _split_dot(v, e2):
    hi, lo = _split_bf16(v)
    return jnp.dot(jnp.concatenate([hi, lo], axis=1), e2, preferred_element_type=F32)


def _ssd_kernel(xs_f, b_f, c_f, dt_f, xs_b, b_b, c_b, dt_b, bias_ref, alog_ref, e_ref,
                of_ref, ob_ref, h_s):
    @pl.when(pl.program_id(0) == 0)
    def _():
        h_s[...] = jnp.zeros_like(h_s)

    e = e_ref[...]
    _ssd_chunk(xs_f, b_f, c_f, dt_f, bias_ref[0:1, :], alog_ref[0:1, :], e, of_ref, h_s.at[0], False)
    _ssd_chunk(xs_b, b_b, c_b, dt_b, bias_ref[1:2, :], alog_ref[1:2, :], e, ob_ref, h_s.at[1], True)


def _ssd_chunk(xs_ref, b_ref, c_ref, dt_ref, bias, alog, e, o_ref, h_s, rev):
    q = CHUNK
    ii = lax.broadcasted_iota(jnp.int32, (q, q), 0)
    jj = lax.broadcasted_iota(jnp.int32, (q, q), 1)
    seen = (jj >= ii) if rev else (jj <= ii)
    tri = jnp.where(seen, 1.0, 0.0).astype(F32)

    dt = _softplus(dt_ref[...] + bias)
    a = dt * (-jnp.exp(alog))
    cum = jnp.dot(tri, a, precision=HIGHEST, preferred_element_type=F32)
    cum_t = cum.T
    dt_t = dt.T
    total = cum[0:1, :] if rev else cum[q - 1:q, :]
    dfs = jnp.exp(cum)
    w_end = dt * jnp.exp(total - cum)
    cdec = jnp.broadcast_to(jnp.exp(total), (8, LANES))
    expanded = _split_dot(jnp.concatenate([w_end, dfs, cdec], axis=0), e)
    w_end_x = expanded[0:q]
    dfs_x = expanded[q:2 * q]
    cdec_x = expanded[2 * q:2 * q + 1]

    xs = xs_ref[...]
    xw = (xs.astype(F32) * w_end_x).astype(BF16)
    first_head = jj < SSM_HEADDIM

    for g in range(SSM_GROUPS):
        cg = c_ref[:, g * SSM_STATE:(g + 1) * SSM_STATE]
        bg = b_ref[:, g * SSM_STATE:(g + 1) * SSM_STATE]
        gs = slice(g * GROUP_W, (g + 1) * GROUP_W)
        cb = lax.dot_general(cg, bg, (((1,), (1,)), ((), ())), preferred_element_type=F32)
        h_prev = h_s[:, gs]
        y_off = jnp.dot(cg, h_prev.astype(BF16), preferred_element_type=F32) * dfs_x[:, gs]
        new = lax.dot_general(bg, xw[:, gs], (((0,), (0,)), ((), ())), preferred_element_type=F32)
        pieces = []
        for hh in range(0, SSM_HPG, 2):
            ms = []
            for hd in (g * SSM_HPG + hh, g * SSM_HPG + hh + 1):
                seg = cum[:, hd:hd + 1] - cum_t[hd:hd + 1, :]
                m = jnp.exp(jnp.where(seen, seg, NEG_BIG)) * cb * dt_t[hd:hd + 1, :]
                ms.append(m.astype(BF16))
            c0 = (g * SSM_HPG + hh) * SSM_HEADDIM
            xp = xs[:, c0:c0 + LANES]
            zero = jnp.zeros_like(xp)
            rhs = jnp.concatenate([jnp.where(first_head, xp, zero), jnp.where(first_head, zero, xp)],
                                  axis=0)
            pieces.append(jnp.dot(jnp.concatenate(ms, axis=1), rhs, preferred_element_type=F32))
        y = jnp.concatenate(pieces, axis=1) + y_off
        o_ref[:, gs] = y.astype(o_ref.dtype)
        h_s[:, gs] = h_prev * cdec_x[:, gs] + new


def _scan_specs(n_ctx, t, col_blocks, width):
    nc, nc_ctx = t // CHUNK, n_ctx // CHUNK
    specs = []
    for rev in (False, True):
        cidx = functools.partial(_chunk_index, nc_ctx=nc_ctx, nc=nc, rev=rev)
        specs.append([pl.BlockSpec((CHUNK, w), lambda s, c=c, cidx=cidx: (cidx(s), c))
                      for c, w in zip(col_blocks, width)])
    return specs


def _ssd_scan(xbc_act, dt_raw, dt_bias, a_log, e_mat, n_ctx):
    t = xbc_act.shape[0]
    bc0 = D_MODEL // BC_W
    fwd, bwd = _scan_specs(n_ctx, t, (0, bc0, bc0 + 1, 0), (D_MODEL, BC_W, BC_W, LANES))
    out_f, out_b = _scan_specs(n_ctx, t, (0,), (D_MODEL,))
    const = [pl.BlockSpec((2, LANES), lambda s: (0, 0)),
             pl.BlockSpec((2, LANES), lambda s: (0, 0)),
             pl.BlockSpec((2 * LANES, D_MODEL), lambda s: (0, 0))]
    seqs = (xbc_act, xbc_act, xbc_act, dt_raw)
    return pl.pallas_call(
        _ssd_kernel,
        grid=(t // CHUNK,),
        in_specs=fwd + bwd + const,
        out_specs=out_f + out_b,
        out_shape=[jax.ShapeDtypeStruct((t, D_MODEL), BF16)] * 2,
        scratch_shapes=[pltpu.VMEM((2, SSM_STATE, D_MODEL), F32)],
        compiler_params=_params(("arbitrary",)),
        name="ssd_scan",
    )(*seqs, *seqs, dt_bias, a_log, e_mat)


def _ret_kernel(q_f, k_f, v_f, q_b, k_b, v_b, dl_ref, of_ref, ob_ref, r_s, dm_s, kd_s, qd_s, cd_s):
    qn = CHUNK

    @pl.when(pl.program_id(0) == 0)
    def _():
        lg_both = -_softplus(-dl_ref[...])
        cd_s[...] = jnp.exp(float(qn) * lg_both)
        r_s[...] = jnp.zeros_like(r_s)
        ii = lax.broadcasted_iota(jnp.int32, (qn, qn), 0)
        jj = lax.broadcasted_iota(jnp.int32, (qn, qn), 1)
        for di, rev in enumerate((False, True)):
            diff = (jj - ii) if rev else (ii - jj)
            dpos = jnp.maximum(diff, 0).astype(F32)
            k_exp = (ii if rev else (qn - 1 - ii)).astype(F32)
            q_exp = ((qn - ii) if rev else (ii + 1)).astype(F32)
            for hd in range(RET_HEADS):
                lg = lg_both[di:di + 1, hd:hd + 1]
                dm_s[di, hd] = jnp.where(diff >= 0, jnp.exp(dpos * lg), 0.0)
                kd_s[di, hd] = jnp.exp(k_exp * lg)
                qd_s[di, hd] = jnp.exp(q_exp * lg)

    _ret_chunk(q_f, k_f, v_f, cd_s[0:1, :], of_ref, r_s.at[0], dm_s.at[0], kd_s.at[0], qd_s.at[0])
    _ret_chunk(q_b, k_b, v_b, cd_s[1:2, :], ob_ref, r_s.at[1], dm_s.at[1], kd_s.at[1], qd_s.at[1])


def _ret_chunk(q_ref, k_ref, v_ref, chunk_decay, o_ref, r_s, dm_s, kd_s, qd_s):
    for hd in range(RET_HEADS):
        hs = slice(hd * RET_DK, (hd + 1) * RET_DK)
        qh = q_ref[:, hs]
        kh = k_ref[:, hs]
        vh = v_ref[:, hs]
        sc = lax.dot_general(qh, kh, (((1,), (1,)), ((), ())), preferred_element_type=F32)
        y = jnp.dot((sc * dm_s[hd]).astype(BF16), vh, preferred_element_type=F32)
        r_prev = r_s[hd]
        qd = qd_s[hd]
        y_int = jnp.dot(qh, r_prev.astype(BF16), preferred_element_type=F32)
        y = y + y_int * jnp.concatenate([qd, qd], axis=1)
        kd = kd_s[hd]
        kdec = (kh.astype(F32) * jnp.concatenate([kd, kd], axis=1)).astype(BF16)
        kv = lax.dot_general(kdec, vh, (((0,), (0,)), ((), ())), preferred_element_type=F32)
        r_s[hd] = r_prev * chunk_decay[:, hd:hd + 1] + kv
        o_ref[:, hs] = y.astype(o_ref.dtype)


def _ret_scan(p_main, decay_logit, n_ctx):
    t = p_main.shape[0]
    fwd, bwd = _scan_specs(n_ctx, t, (1, 2, 3), (D_MODEL,) * 3)
    out_f, out_b = _scan_specs(n_ctx, t, (0,), (D_MODEL,))
    seqs = (p_main, p_main, p_main)
    decay_tables = pltpu.VMEM((2, RET_HEADS, CHUNK, CHUNK), F32)
    return pl.pallas_call(
        _ret_kernel,
        grid=(t // CHUNK,),
        in_specs=fwd + bwd + [pl.BlockSpec((2, LANES), lambda s: (0, 0))],
        out_specs=out_f + out_b,
        out_shape=[jax.ShapeDtypeStruct((t, D_MODEL), BF16)] * 2,
        scratch_shapes=[pltpu.VMEM((2, RET_HEADS, RET_DK, RET_DK), F32),
                        decay_tables, decay_tables, decay_tables, pltpu.VMEM((2, LANES), F32)],
        compiler_params=_params(("arbitrary",)),
        name="ret_scan",
    )(*seqs, *seqs, decay_logit)


def _merge_kernel(ysf_ref, ysb_ref, xs_ref, z_ref, yrf_ref, yrb_ref, g_ref, gs_ref, gr_ref,
                  dsk_ref, snw_ref, rnw_ref, ws_ref, wr_ref, o_ref):
    y = (ysf_ref[...].astype(F32) + ysb_ref[...].astype(F32)
         + dsk_ref[...] * xs_ref[...].astype(F32))
    y = y * _silu(z_ref[...].astype(F32))
    ms = jnp.mean(y * y, axis=-1, keepdims=True)
    a = (y * lax.rsqrt(ms + EPS) * snw_ref[...]).astype(BF16)
    merged = _sigmoid(gs_ref[...].astype(F32)) * jnp.dot(a, ws_ref[...],
                                                         preferred_element_type=F32)
    gate = _silu(g_ref[...].astype(F32)) * rnw_ref[...]
    parts = []
    for hd in range(RET_HEADS):
        hs = slice(hd * RET_DK, (hd + 1) * RET_DK)
        r = yrf_ref[:, hs].astype(F32) + yrb_ref[:, hs].astype(F32)
        ms = jnp.mean(r * r, axis=-1, keepdims=True)
        parts.append((r * lax.rsqrt(ms + EPS) * gate[:, hs]).astype(BF16))
    b = jnp.concatenate(parts, axis=1)
    merged = merged + _sigmoid(gr_ref[...].astype(F32)) * jnp.dot(
        b, wr_ref[...], preferred_element_type=F32)
    o_ref[...] = merged.astype(BF16)


def _merge(y_ssd, xbc_act, p_main, y_ret, d_skip, ssm_norm_w, ret_norm_w, w_ssd, w_ret, l):
    t, d = xbc_act.shape[0], D_MODEL
    bm = MERGE_ROW_BLOCK
    once = pl.Buffered(1)
    return pl.pallas_call(
        _merge_kernel,
        grid=(t // bm,),
        in_specs=[pl.BlockSpec((bm, d), lambda i: (i, 0)),
                  pl.BlockSpec((bm, d), lambda i: (i, 0)),
                  pl.BlockSpec((bm, d), lambda i: (i, 0)),
                  pl.BlockSpec((bm, d), lambda i: (i, 0)),
                  pl.BlockSpec((bm, d), lambda i: (i, 0)),
                  pl.BlockSpec((bm, d), lambda i: (i, 0)),
                  pl.BlockSpec((bm, d), lambda i: (i, 4)),
                  pl.BlockSpec((bm, d), lambda i: (i, 5)),
                  pl.BlockSpec((bm, d), lambda i: (i, 6)),
                  pl.BlockSpec((None, 1, d), lambda i: (l, 0, 0)),
                  pl.BlockSpec((None, 1, d), lambda i: (l, 0, 0)),
                  pl.BlockSpec((None, 1, d), lambda i: (l, 0, 0)),
                  pl.BlockSpec((None, d, d), lambda i: (l, 0, 0), pipeline_mode=once),
                  pl.BlockSpec((None, d, d), lambda i: (l, 0, 0), pipeline_mode=once)],
        out_specs=pl.BlockSpec((bm, d), lambda i: (i, 0)),
        out_shape=jax.ShapeDtypeStruct((t, d), BF16),
        compiler_params=_params(("arbitrary",)),
        name="merge",
    )(*y_ssd, xbc_act, p_main, *y_ret, p_main, p_main, p_main, d_skip, ssm_norm_w, ret_norm_w,
      w_ssd, w_ret)


def _pack_bf16_pairs(v):
    n = v.shape[1] // 2
    bits = lax.bitcast_convert_type(v.astype(BF16).astype(F32), jnp.uint32)
    return bits[:, :n] | (bits[:, n:] >> 16)


def _unpack_bf16_pairs(p):
    hi = lax.bitcast_convert_type(p & jnp.uint32(0xFFFF0000), F32)
    lo = lax.bitcast_convert_type(p << 16, F32)
    return jnp.concatenate([hi, lo], axis=1).astype(BF16)


def _route_rows(x, i, sh_ref, sc_ref, nw_ref, wr_ref, br_ref, xn_ref, meta_ref, cnt_ref,
                carry_s, *, n_ctx, bm):
    @pl.when(i == 0)
    def _():
        carry_s[...] = jnp.zeros_like(carry_s)

    xn = _modulated_norm(x, nw_ref[...], sh_ref, sc_ref, i * bm, n_ctx)
    xn_ref[...] = _pack_bf16_pairs(xn)
    logits = _dot3(xn, wr_ref[...]) + br_ref[...]
    lane = lax.broadcasted_iota(jnp.int32, (bm, LANES), 1)
    far = jnp.int32(4 * LANES)

    glog = jnp.where(lane < N_GROUPS_E, logits, NEG_BIG)
    gmax = jnp.max(glog, axis=1, keepdims=True)
    gidx = jnp.min(jnp.where(glog == gmax, lane, far), axis=1, keepdims=True)
    p_top = 1.0 / jnp.sum(jnp.exp(glog - gmax), axis=1, keepdims=True)

    in_group = ((lane >= N_GROUPS_E) & (lane < N_GROUPS_E + N_EXPERTS)
                & (((lane - N_GROUPS_E) >> 3) == gidx))
    elog = jnp.where(in_group, logits, NEG_BIG)
    m1 = jnp.max(elog, axis=1, keepdims=True)
    i1 = jnp.min(jnp.where(elog == m1, lane, far), axis=1, keepdims=True)
    elog2 = jnp.where(lane == i1, NEG_BIG, elog)
    m2 = jnp.max(elog2, axis=1, keepdims=True)
    i2 = jnp.min(jnp.where(elog2 == m2, lane, far), axis=1, keepdims=True)
    e2 = jnp.exp(m2 - m1)
    w1 = p_top / (1.0 + e2)
    w2 = p_top * e2 / (1.0 + e2)
    ex1 = i1 - N_GROUPS_E
    ex2 = i2 - N_GROUPS_E

    onehot = jnp.where((lane == ex1) | (lane == ex2), 1.0, 0.0).astype(F32)
    ri = lax.broadcasted_iota(jnp.int32, (bm, bm), 0)
    ci = lax.broadcasted_iota(jnp.int32, (bm, bm), 1)
    earlier = jnp.where(ci < ri, 1.0, 0.0).astype(BF16)
    before = jnp.dot(earlier, onehot.astype(BF16), preferred_element_type=F32) + carry_s[0:1, :]
    r1 = jnp.sum(jnp.where(lane == ex1, before, 0.0), axis=1, keepdims=True)
    r2 = jnp.sum(jnp.where(lane == ex2, before, 0.0), axis=1, keepdims=True)
    carry = carry_s[...] + jnp.sum(onehot, axis=0, keepdims=True)
    carry_s[...] = carry
    cnt_ref[...] = carry

    meta = jnp.where(lane == 0, ex1.astype(F32), 0.0)
    meta = jnp.where(lane == 1, ex2.astype(F32), meta)
    meta = jnp.where(lane == 2, w1, meta)
    meta = jnp.where(lane == 3, w2, meta)
    meta = jnp.where(lane == 4, r1, meta)
    meta = jnp.where(lane == 5, r2, meta)
    meta_ref[...] = meta


def _outproj_kernel(m_ref, w_ref, h_ref, g_ref, sh_ref, sc_ref, nw_ref, wr_ref, br_ref,
                    o_ref, xn_ref, meta_ref, cnt_ref, carry_s, *, n_ctx, bm):
    i = pl.program_id(0)
    row = i * bm + lax.broadcasted_iota(jnp.int32, (bm, 1), 0)
    gate = jnp.where(row < n_ctx, g_ref[1:2, :], g_ref[0:1, :])
    h_new = h_ref[...] + gate * jnp.dot(m_ref[...], w_ref[...], preferred_element_type=F32)
    o_ref[...] = h_new
    _route_rows(h_new, i, sh_ref, sc_ref, nw_ref, wr_ref, br_ref, xn_ref, meta_ref, cnt_ref,
                carry_s, n_ctx=n_ctx, bm=bm)


def _outproj_route(merged, w_out, h, mod, norm_w, w_route, b_route, l, n_ctx):
    t, d = h.shape
    bm = OUT_ROW_BLOCK
    return pl.pallas_call(
        functools.partial(_outproj_kernel, n_ctx=n_ctx, bm=bm),
        grid=(t // bm,),
        in_specs=[pl.BlockSpec((bm, d), lambda i: (i, 0)),
                  pl.BlockSpec((None, d, d), lambda i: (l, 0, 0), pipeline_mode=pl.Buffered(1)),
                  pl.BlockSpec((bm, d), lambda i: (i, 0)),
                  pl.BlockSpec((None, 8, d), lambda i: (l, 0, 2)),
                  pl.BlockSpec((None, 8, d), lambda i: (l, 0, 3)),
                  pl.BlockSpec((None, 8, d), lambda i: (l, 0, 4)),
                  pl.BlockSpec((None, 1, d), lambda i: (l, 0, 0)),
                  pl.BlockSpec((None, d, LANES), lambda i: (l, 0, 0)),
                  pl.BlockSpec((None, 1, LANES), lambda i: (l, 0, 0))],
        out_specs=[pl.BlockSpec((bm, d), lambda i: (i, 0)),
                   pl.BlockSpec((bm, d // 2), lambda i: (i, 0)),
                   pl.BlockSpec((bm, LANES), lambda i: (i, 0)),
                   pl.BlockSpec((8, LANES), lambda i: (0, 0))],
        out_shape=[jax.ShapeDtypeStruct((t, d), F32),
                   jax.ShapeDtypeStruct((t, d // 2), jnp.uint32),
                   jax.ShapeDtypeStruct((t, LANES), F32),
                   jax.ShapeDtypeStruct((8, LANES), F32)],
        scratch_shapes=[pltpu.VMEM((8, LANES), F32)],
        compiler_params=_params(("arbitrary",)),
        name="outproj_route",
    )(merged, w_out, h, mod, mod, mod, norm_w, w_route, b_route)


def _row_copy(src, src_row, dst, dst_row, sem):
    return pltpu.make_async_copy(src.at[pl.ds(src_row, 1), :], dst.at[pl.ds(dst_row, 1), :], sem)


def _dispatch_kernel(pos_ref, xn_ref, init_ref, o_ref, sem, *, bm, t):
    del init_ref
    base = pl.program_id(0) * bm

    def issue(g, carry):
        r0 = g * DMA_GROUP
        ps = [[pos_ref[k * t + base + r0 + u] for k in range(2)] for u in range(DMA_GROUP)]
        for u in range(DMA_GROUP):
            for k in range(2):
                _row_copy(xn_ref, r0 + u, o_ref, ps[u][k], sem).start(priority=k)
        return carry

    lax.fori_loop(0, bm // DMA_GROUP, issue, 0)
    for k in range(2):
        pltpu.make_async_copy(xn_ref, o_ref.at[pl.ds(0, bm), :], sem).wait()


def _dispatch(pos, xn, init):
    t, d = xn.shape
    n_rows = init.shape[0]
    bm = TOK_BLOCK
    return pl.pallas_call(
        functools.partial(_dispatch_kernel, bm=bm, t=t),
        grid_spec=pltpu.PrefetchScalarGridSpec(
            num_scalar_prefetch=1,
            grid=(t // bm,),
            in_specs=[pl.BlockSpec((bm, d), lambda i, pos: (i, 0)),
                      pl.BlockSpec(memory_space=pl.ANY)],
            out_specs=pl.BlockSpec(memory_space=pl.ANY),
            scratch_shapes=[pltpu.SemaphoreType.DMA(())]),
        out_shape=jax.ShapeDtypeStruct((n_rows, d), xn.dtype),
        input_output_aliases={2: 0},
        compiler_params=_params(("arbitrary",)),
        name="dispatch",
    )(pos, xn, init)


def _expert_kernel(te_ref, nx_ref, xi_ref, first_ref, valid_ref, x_ref, wg_hbm, wu_hbm, wd_hbm,
                   o_ref, lg_s, lu_s, ld_s, wg_s, wu_s, wd_s, sem, *, l):
    i = pl.program_id(0)

    weights = ((wg_hbm, lg_s, wg_s), (wu_hbm, lu_s, wu_s), (wd_hbm, ld_s, wd_s))

    def fetch(k, e):
        return pltpu.make_async_copy(weights[k][0].at[l, e], weights[k][1], sem.at[k])

    @pl.when(i == 0)
    def _():
        for k in range(3):
            fetch(k, te_ref[0]).start()

    @pl.when(first_ref[i] == 1)
    def _():
        has_next = nx_ref[i] >= 0
        nxt = jnp.maximum(nx_ref[i], 0)
        for k, (_, land, work) in enumerate(weights):
            fetch(k, te_ref[i]).wait()
            work[...] = land[...].astype(BF16)

            @pl.when(has_next)
            def _():
                fetch(k, nxt).start()

    @pl.when(valid_ref[i] == 1)
    def _():
        x = _unpack_bf16_pairs(x_ref[...])
        gate = jnp.dot(x, wg_s[...], preferred_element_type=F32)
        up = jnp.dot(x, wu_s[...], preferred_element_type=F32)
        mid = (_silu(gate) * up).astype(BF16)
        o_ref[...] = jnp.dot(mid, wd_s[...], preferred_element_type=F32)

    @pl.when(valid_ref[i] == 0)
    def _():
        o_ref[...] = jnp.zeros_like(o_ref)


def _experts(tile_meta, x_sorted, w_gate, w_up, w_down, l, n_tiles):
    d = D_MODEL
    tm = EXPERT_TILE
    hbm = pl.BlockSpec(memory_space=pl.ANY)
    return pl.pallas_call(
        functools.partial(_expert_kernel, l=l),
        grid_spec=pltpu.PrefetchScalarGridSpec(
            num_scalar_prefetch=5,
            grid=(n_tiles,),
            in_specs=[pl.BlockSpec((tm, d // 2), lambda i, te, nx, xi, fi, va: (xi[i], 0)),
                      hbm, hbm, hbm],
            out_specs=pl.BlockSpec((tm, d), lambda i, te, nx, xi, fi, va: (i, 0)),
            scratch_shapes=[pltpu.VMEM((d, D_EXPERT), F32), pltpu.VMEM((d, D_EXPERT), F32),
                            pltpu.VMEM((D_EXPERT, d), F32),
                            pltpu.VMEM((d, D_EXPERT), BF16), pltpu.VMEM((d, D_EXPERT), BF16),
                            pltpu.VMEM((D_EXPERT, d), BF16),
                            pltpu.SemaphoreType.DMA((3,))]),
        out_shape=jax.ShapeDtypeStruct((n_tiles * tm, d), F32),
        compiler_params=_params(("arbitrary",)),
        name="experts",
    )(*tile_meta, x_sorted, w_gate, w_up, w_down)


def _combine_kernel(pos_ref, h_ref, meta_ref, g_ref, ys_ref, o_ref, buf_s, sem, *, n_ctx, bm, t):
    i = pl.program_id(0)
    n = pl.num_programs(0)

    def issue(blk, slot):
        base = blk * bm

        def body(g, carry):
            r0 = g * DMA_GROUP
            ps = [[pos_ref[k * t + base + r0 + u] for k in range(2)] for u in range(DMA_GROUP)]
            for u in range(DMA_GROUP):
                for k in range(2):
                    _row_copy(ys_ref, ps[u][k], buf_s.at[slot, k], r0 + u,
                              sem.at[slot]).start(priority=k)
            return carry

        lax.fori_loop(0, bm // DMA_GROUP, body, 0)

    @pl.when(i == 0)
    def _():
        issue(0, 0)

    @pl.when(i + 1 < n)
    def _():
        issue(i + 1, (i + 1) % 2)

    slot = i % 2
    for k in range(2):
        pltpu.make_async_copy(ys_ref.at[pl.ds(0, bm), :], buf_s.at[slot, k], sem.at[slot]).wait()
    meta = meta_ref[...]
    moe = meta[:, 2:3] * buf_s[slot, 0] + meta[:, 3:4] * buf_s[slot, 1]
    row = i * bm + lax.broadcasted_iota(jnp.int32, (bm, 1), 0)
    gate = jnp.where(row < n_ctx, g_ref[1:2, :], g_ref[0:1, :])
    o_ref[...] = h_ref[...] + gate * moe


def _combine(pos, h, meta, mod, y_sorted, l, n_ctx):
    t, d = h.shape
    bm = TOK_BLOCK
    return pl.pallas_call(
        functools.partial(_combine_kernel, n_ctx=n_ctx, bm=bm, t=t),
        grid_spec=pltpu.PrefetchScalarGridSpec(
            num_scalar_prefetch=1,
            grid=(t // bm,),
            in_specs=[pl.BlockSpec((bm, d), lambda i, pos: (i, 0)),
                      pl.BlockSpec((bm, LANES), lambda i, pos: (i, 0)),
                      pl.BlockSpec((None, 8, d), lambda i, pos: (l, 0, 5)),
                      pl.BlockSpec(memory_space=pl.ANY)],
            out_specs=pl.BlockSpec((bm, d), lambda i, pos: (i, 0)),
            scratch_shapes=[pltpu.VMEM((2, 2, bm, d), F32), pltpu.SemaphoreType.DMA((2,))]),
        out_shape=jax.ShapeDtypeStruct((t, d), F32),
        compiler_params=_params(("arbitrary",)),
        name="combine",
    )(pos, h, meta, mod, y_sorted)


def _tile_plan(meta, counts, n_tiles):
    tm = EXPERT_TILE
    cnt = counts[0, :N_EXPERTS].astype(jnp.int32)
    tiles_e = (cnt + tm - 1) // tm
    tile_end = jnp.cumsum(tiles_e)
    offset = (tile_end - tiles_e) * tm
    meta_t = meta[:, :8].T
    expert = meta_t[0:2].astype(jnp.int32)
    rank = meta_t[4:6].astype(jnp.int32)
    ids = jnp.arange(N_EXPERTS, dtype=jnp.int32)[:, None, None]
    base = jnp.sum(jnp.where(expert[None] == ids, offset[:, None, None], 0), axis=0)
    pos = (base + rank).reshape(-1)
    n_used = tile_end[-1]
    tid = jnp.arange(n_tiles, dtype=jnp.int32)
    valid = tid < n_used
    tsrc = jnp.minimum(tid, n_used - 1)
    def expert_of_tile(v):
        return jnp.minimum(jnp.sum((tile_end[None, :] <= v[:, None]).astype(jnp.int32), axis=1),
                           N_EXPERTS - 1)

    te = expert_of_tile(tsrc)
    first = (valid & ((tid == 0) | (te != jnp.roll(te, 1)))).astype(jnp.int32)
    after = jnp.sum(jnp.where(te[:, None] == jnp.arange(N_EXPERTS)[None, :], tile_end[None, :], 0),
                    axis=1)
    nxt = jnp.where(after < n_used, expert_of_tile(after), -1).astype(jnp.int32)
    return pos, (te, nxt, tsrc.astype(jnp.int32), first, valid.astype(jnp.int32))


def _final_kernel(h_ref, w_ref, o_ref):
    x = h_ref[...]
    ms = jnp.mean(x * x, axis=-1, keepdims=True)
    o_ref[...] = x * lax.rsqrt(ms + EPS) * w_ref[...]


def _final_norm(h, w, n_ctx):
    t, d = h.shape
    bm = TOK_BLOCK
    skip = n_ctx // bm
    return pl.pallas_call(
        _final_kernel,
        grid=((t - n_ctx) // bm,),
        in_specs=[pl.BlockSpec((bm, d), lambda i: (i + skip, 0)),
                  pl.BlockSpec((1, d), lambda i: (0, 0))],
        out_specs=pl.BlockSpec((bm, d), lambda i: (i, 0)),
        out_shape=jax.ShapeDtypeStruct((t - n_ctx, d), F32),
        compiler_params=_params(("arbitrary",)),
        name="final_norm",
    )(h, w)


def _pad_lanes(v):
    return jnp.pad(v, [(0, 0)] * (v.ndim - 1) + [(0, LANES - v.shape[-1])])


def _rope_tables(n_ctx, n_lat):
    rows = n_lat // GRID_W
    row = jnp.broadcast_to(jnp.arange(rows, dtype=F32)[:, None], (rows, GRID_W)).reshape(n_lat)
    col = jnp.broadcast_to(jnp.arange(GRID_W, dtype=F32)[None, :], (rows, GRID_W)).reshape(n_lat)
    inv_freq = ROPE_BASE ** (-jnp.arange(ROPE_FREQS, dtype=F32) / ROPE_FREQS)
    ang_r = row[:, None] * inv_freq[None, :]
    ang_c = col[:, None] * inv_freq[None, :]
    cos_l = jnp.concatenate([jnp.cos(ang_r)] * 2 + [jnp.cos(ang_c)] * 2, axis=1)
    sin_l = jnp.concatenate([-jnp.sin(ang_r), jnp.sin(ang_r), -jnp.sin(ang_c), jnp.sin(ang_c)], axis=1)
    cos_t = jnp.concatenate([jnp.ones((n_ctx, 2 * LANES), F32), cos_l], axis=0)
    sin_t = jnp.concatenate([jnp.zeros((n_ctx, 2 * LANES), F32), sin_l], axis=0)
    return cos_t, sin_t


def kernel(x, c, ctx, c_ctx, w_mod, b_mod, norm1_w, w_in, conv_w, conv_b, ssm_A_log, ssm_dt_bias,
           ssm_D, ssm_norm_w, ret_decay_logit, ret_norm_w, w_ssd_proj, w_ret_proj, w_out, norm2_w,
           w_route_group, b_route_group, w_route_expert, b_route_expert, w_exp_gate, w_exp_up,
           w_exp_down, final_norm_w):
    batch, n_lat, d = x.shape
    n_ctx = ctx.shape[1]
    depth = w_mod.shape[0]
    assert batch == 1 and d == D_MODEL and w_in.shape[-1] == MAIN_W + DT_W
    t = n_ctx + n_lat
    assert t % OUT_ROW_BLOCK == 0 and n_ctx % TOK_BLOCK == 0 and n_lat % TOK_BLOCK == 0

    w_nat, w_dt = _wprep(jnp.swapaxes(w_in, 1, 2))
    w_ssd_b = w_ssd_proj.astype(BF16)
    w_ret_b = w_ret_proj.astype(BF16)
    w_out_b = w_out.astype(BF16)
    w_route = _pad_lanes(jnp.concatenate([w_route_group, w_route_expert], axis=-1))
    b_route = _pad_lanes(jnp.concatenate([b_route_group, b_route_expert], axis=-1))[:, None, :]
    d_skip = jnp.repeat(ssm_D, SSM_HEADDIM, axis=-1)[:, None, :]
    dt_bias = _pad_lanes(ssm_dt_bias)
    a_log = _pad_lanes(ssm_A_log)
    ret_dl = _pad_lanes(ret_decay_logit)
    head_of_col = jnp.arange(D_MODEL, dtype=jnp.int32) // SSM_HEADDIM
    e_mat = ((jnp.arange(2 * LANES, dtype=jnp.int32) % LANES)[:, None]
             == head_of_col[None, :]).astype(BF16)
    cos_t, sin_t = _rope_tables(n_ctx, n_lat)

    cvec = jnp.concatenate([c, c_ctx[None, :], jnp.zeros((6, d), F32)], axis=0)
    mod = _modulation(cvec, w_mod, b_mod)

    n_tiles = (2 * t) // EXPERT_TILE + N_EXPERTS
    h = jnp.concatenate([ctx[0], x[0]], axis=0)
    x_sorted = jnp.zeros((n_tiles * EXPERT_TILE, d // 2), jnp.uint32)
    for l in range(depth):
        p_main, dt_raw = _inproj(h, mod, norm1_w[:, None, :], w_nat, w_dt, cos_t, sin_t, l, n_ctx)
        xbc_act = _conv(p_main, conv_w, conv_b[:, None, :], l, n_ctx)
        y_ssd = _ssd_scan(xbc_act, dt_raw, dt_bias[l], a_log[l], e_mat, n_ctx)
        y_ret = _ret_scan(p_main, ret_dl[l], n_ctx)
        merged = _merge(y_ssd, xbc_act, p_main, y_ret, d_skip, ssm_norm_w[:, None, :],
                        ret_norm_w[:, None, :], w_ssd_b, w_ret_b, l)
        h, xn, meta, counts = _outproj_route(merged, w_out_b, h, mod, norm2_w[:, None, :], w_route,
                                             b_route, l, n_ctx)
        pos, tile_meta = _tile_plan(meta, counts, n_tiles)
        x_sorted = _dispatch(pos, xn, x_sorted)
        y_sorted = _experts(tile_meta, x_sorted, w_exp_gate, w_exp_up, w_exp_down, l, n_tiles)
        h = _combine(pos, h, meta, mod, y_sorted, l, n_ctx)

    return _final_norm(h, final_norm_w[None, :], n_ctx)[None]
```

```python
import functools

import jax
import jax.numpy as jnp
from jax import lax
from jax.experimental import pallas as pl
from jax.experimental.pallas import tpu as pltpu

F32 = jnp.float32
BF16 = jnp.bfloat16
HIGHEST = lax.Precision.HIGHEST

D_MODEL = 2048
GRID_W = 64
CHUNK = 128
EPS = 1e-6

SSM_HEADDIM = 64
SSM_HEADS = 32
SSM_GROUPS = 4
SSM_HPG = 8
SSM_STATE = 128
SSM_CONV = 5
GROUP_W = SSM_HPG * SSM_HEADDIM
BC_W = SSM_GROUPS * SSM_STATE
XBC_D = D_MODEL + 2 * BC_W

RET_HEADS = 8
RET_DK = 256
ROPE_FREQS = 64
ROPE_BASE = 10000.0

N_GROUPS_E = 4
EXPERTS_PER_GROUP = 8
N_EXPERTS = 32
D_EXPERT = 512

LANES = 128
MAIN_W = 7 * D_MODEL + XBC_D
XBC_COL0 = 7 * D_MODEL
DT_W = SSM_HEADS
DT_COL0 = D_MODEL + XBC_D

INPROJ_ROW_BLOCKS = (1056, 768)
COL_BLOCK = 1024
OUT_ROW_BLOCK = 384
MERGE_ROW_BLOCK = 256
TOK_BLOCK = 256
DMA_GROUP = 8
EXPERT_TILE = 256
NEG_BIG = -1e30
VMEM_LIMIT = 56 * 1024 * 1024


def _params(sem, vmem=VMEM_LIMIT):
    return pltpu.CompilerParams(dimension_semantics=sem, vmem_limit_bytes=vmem)


def _sigmoid(v):
    return 0.5 * jnp.tanh(0.5 * v) + 0.5


def _silu(v):
    return v * _sigmoid(v)


def _softplus(v):
    return jnp.maximum(v, 0.0) + jnp.log1p(jnp.exp(-jnp.abs(v)))


def _split_bf16(v):
    hi = v.astype(BF16)
    return hi, (v - hi.astype(F32)).astype(BF16)


def _dot3(x, w):
    n = w.shape[1]
    xh, xl = _split_bf16(x)
    wh, wl = _split_bf16(w)
    both = jnp.dot(xh, jnp.concatenate([wh, wl], axis=1), preferred_element_type=F32)
    return both[:, :n] + both[:, n:] + jnp.dot(xl, wh, preferred_element_type=F32)


def _modulated_norm(x, nw, sh_ref, sc_ref, row0, n_ctx):
    ms = jnp.mean(x * x, axis=-1, keepdims=True)
    y = x * lax.rsqrt(ms + EPS) * nw
    row = row0 + lax.broadcasted_iota(jnp.int32, (x.shape[0], 1), 0)
    is_ctx = row < n_ctx
    sc = jnp.where(is_ctx, sc_ref[1:2, :], sc_ref[0:1, :])
    sh = jnp.where(is_ctx, sh_ref[1:2, :], sh_ref[0:1, :])
    return y * (1.0 + sc) + sh


def _mod_kernel(a_ref, w_ref, b_ref, o_ref):
    xh, xl = _split_bf16(_silu(a_ref[...]))
    wh, wl = _split_bf16(w_ref[...])
    o_ref[...] = (jnp.dot(xh, wh, preferred_element_type=F32)
                  + jnp.dot(xl, wh, preferred_element_type=F32)
                  + jnp.dot(xh, wl, preferred_element_type=F32)) + b_ref[...]


def _modulation(cvec, w_mod, b_mod):
    depth, d, n = w_mod.shape
    bn = 1024
    return pl.pallas_call(
        _mod_kernel,
        grid=(depth, n // bn),
        in_specs=[pl.BlockSpec((8, d), lambda l, j: (0, 0)),
                  pl.BlockSpec((None, d, bn), lambda l, j: (l, 0, j)),
                  pl.BlockSpec((None, 1, bn), lambda l, j: (l, 0, j))],
        out_specs=pl.BlockSpec((None, 8, bn), lambda l, j: (l, 0, j)),
        out_shape=jax.ShapeDtypeStruct((depth, 8, n), F32),
        compiler_params=_params(("arbitrary", "arbitrary")),
        name="modulation",
    )(cvec, w_mod, b_mod.reshape(depth, 1, n))


def _dot3_nt(x, wt):
    nt = (((1,), (1,)), ((), ()))
    n = wt.shape[0]
    xh, xl = _split_bf16(x)
    wh, wl = _split_bf16(wt)
    both = lax.dot_general(xh, jnp.concatenate([wh, wl], axis=0), nt, preferred_element_type=F32)
    return both[:, :n] + both[:, n:] + lax.dot_general(xl, wh, nt, preferred_element_type=F32)


def _wprep_kernel(w_hbm, o_ref, dt_ref, buf_s, dtb_s, sem, dsem, *, bn, n_aligned, nj):
    l = pl.program_id(0)
    n = pl.program_id(1)
    step = l * nj + n
    total = pl.num_programs(0) * nj

    def fetch(ll, nn, slot):
        row0 = pl.multiple_of(nn * bn + jnp.where(nn >= n_aligned, DT_W, 0), 8)
        return pltpu.make_async_copy(w_hbm.at[ll, pl.ds(row0, bn), :], buf_s.at[slot], sem.at[slot])

    @pl.when(step == 0)
    def _():
        fetch(l, n, 0).start()

    @pl.when(step + 1 < total)
    def _():
        wrap = n + 1 == nj
        fetch(jnp.where(wrap, l + 1, l), jnp.where(wrap, 0, n + 1), (step + 1) % 2).start()

    fetch(l, n, step % 2).wait()
    o_ref[...] = buf_s[step % 2].astype(BF16)

    @pl.when(n == 0)
    def _():
        cp = pltpu.make_async_copy(w_hbm.at[l, pl.ds(DT_COL0, DT_W), :], dtb_s, dsem)
        cp.start()
        cp.wait()
        dt_ref[0:DT_W, :] = dtb_s[...]
        dt_ref[DT_W:, :] = jnp.zeros((LANES - DT_W, dt_ref.shape[1]), F32)


def _wprep(w_in_t):
    depth, _, d = w_in_t.shape
    bn = COL_BLOCK
    nj = MAIN_W // bn
    return pl.pallas_call(
        functools.partial(_wprep_kernel, bn=bn, n_aligned=DT_COL0 // bn, nj=nj),
        grid=(depth, nj),
        in_specs=[pl.BlockSpec(memory_space=pl.ANY)],
        out_specs=[pl.BlockSpec((None, bn, d), lambda l, n: (l, n, 0)),
                   pl.BlockSpec((None, LANES, d), lambda l, n: (l, 0, 0))],
        out_shape=[jax.ShapeDtypeStruct((depth, MAIN_W, d), BF16),
                   jax.ShapeDtypeStruct((depth, LANES, d), F32)],
        scratch_shapes=[pltpu.VMEM((2, bn, d), F32), pltpu.VMEM((DT_W, d), F32),
                        pltpu.SemaphoreType.DMA((2,)), pltpu.SemaphoreType.DMA(())],
        compiler_params=_params(("arbitrary", "arbitrary")),
        name="wprep",
    )(w_in_t)


def _inproj_kernel(h_ref, sh_ref, sc_ref, nw_ref, w_ref, wdt_ref, cos_ref, sin_ref,
                   o_ref, dt_ref, xn_s, *, n_ctx, bm, bn):
    i = pl.program_id(0)
    j = pl.program_id(1)

    @pl.when(j == 0)
    def _():
        xn = _modulated_norm(h_ref[...], nw_ref[...], sh_ref, sc_ref, i * bm, n_ctx)
        xn_s[...] = xn.astype(BF16)
        dt_ref[...] = _dot3_nt(xn, wdt_ref[...])

    acc = lax.dot_general(xn_s[...], w_ref[...], (((1,), (1,)), ((), ())),
                          preferred_element_type=F32)
    q_lo = DT_COL0 // bn
    k_lo = (DT_COL0 + D_MODEL) // bn
    k_hi = (DT_COL0 + 2 * D_MODEL) // bn
    is_rope = (j >= q_lo) & (j < k_hi)

    @pl.when(is_rope)
    def _():
        scale = jnp.where(j >= k_lo, RET_DK ** -0.5, 1.0).astype(F32)
        for gi in range(bn // LANES):
            u = acc[:, gi * LANES:(gi + 1) * LANES]
            half = gi % 2
            c = cos_ref[:, half * LANES:(half + 1) * LANES]
            s = sin_ref[:, half * LANES:(half + 1) * LANES]
            r = (u * c + pltpu.roll(u, LANES // 2, 1) * s) * scale
            o_ref[:, gi * LANES:(gi + 1) * LANES] = r.astype(BF16)

    @pl.when(jnp.logical_not(is_rope))
    def _():
        o_ref[...] = acc.astype(BF16)


def _inproj(h, mod, norm_w, w_nat, w_dt, cos_t, sin_t, l, n_ctx):
    t, d = h.shape
    bm = next(b for b in INPROJ_ROW_BLOCKS if t % b == 0)
    bn = COL_BLOCK
    nj = MAIN_W // bn
    z_blocks = D_MODEL // bn
    xbc_blocks = XBC_D // bn

    def out_block(j):
        return jnp.where(j < z_blocks, j,
                         jnp.where(j < z_blocks + xbc_blocks, j + (nj - z_blocks - xbc_blocks),
                                   j - xbc_blocks))

    once = pl.Buffered(1)
    kern = functools.partial(_inproj_kernel, n_ctx=n_ctx, bm=bm, bn=bn)
    return pl.pallas_call(
        kern,
        grid=(t // bm, nj),
        in_specs=[pl.BlockSpec((bm, d), lambda i, j: (i, 0)),
                  pl.BlockSpec((None, 8, d), lambda i, j: (l, 0, 0)),
                  pl.BlockSpec((None, 8, d), lambda i, j: (l, 0, 1)),
                  pl.BlockSpec((None, 1, d), lambda i, j: (l, 0, 0)),
                  pl.BlockSpec((None, bn, d), lambda i, j: (l, j, 0)),
                  pl.BlockSpec((None, LANES, d), lambda i, j: (l, 0, 0)),
                  pl.BlockSpec((bm, 2 * LANES), lambda i, j: (i, 0), pipeline_mode=once),
                  pl.BlockSpec((bm, 2 * LANES), lambda i, j: (i, 0), pipeline_mode=once)],
        out_specs=[pl.BlockSpec((bm, bn), lambda i, j: (i, out_block(j))),
                   pl.BlockSpec((bm, LANES), lambda i, j: (i, 0))],
        out_shape=[jax.ShapeDtypeStruct((t, MAIN_W), BF16),
                   jax.ShapeDtypeStruct((t, LANES), F32)],
        scratch_shapes=[pltpu.VMEM((bm, d), BF16)],
        compiler_params=_params(("arbitrary", "arbitrary")),
        name="inproj",
    )(h, mod, mod, norm_w, w_nat, w_dt, cos_t, sin_t)


def _conv_kernel(x_ref, w_ref, b_ref, o_ref, pad_s, *, n_ctx, n_lat, rows):
    halo = 8
    zeros = jnp.zeros((halo, LANES), F32)
    c0 = halo
    x0 = 2 * halo + n_ctx
    pad_s[0:halo, :] = zeros
    pad_s[c0:c0 + n_ctx, :] = x_ref[0:n_ctx, :].astype(F32)
    pad_s[c0 + n_ctx:x0, :] = zeros
    pad_s[x0:x0 + n_lat, :] = x_ref[n_ctx:n_ctx + n_lat, :].astype(F32)
    pad_s[x0 + n_lat:x0 + n_lat + halo, :] = zeros
    w = w_ref[...]
    b = b_ref[...]

    def segment(src0, dst0, n):
        def body(s, carry):
            r0 = pl.multiple_of(s * rows, rows)
            acc = jnp.broadcast_to(b, (rows, LANES))
            for kk in range(SSM_CONV):
                acc = acc + w[kk:kk + 1, :] * pad_s[pl.ds(src0 + r0 + kk - SSM_CONV // 2, rows), :]
            o_ref[pl.ds(dst0 + r0, rows), :] = _silu(acc).astype(BF16)
            return carry
        lax.fori_loop(0, n // rows, body, 0)

    segment(c0, 0, n_ctx)
    segment(x0, n_ctx, n_lat)


def _conv(p_main, conv_w, conv_b, l, n_ctx):
    t = p_main.shape[0]
    n_lat = t - n_ctx
    col0 = XBC_COL0 // LANES
    kern = functools.partial(_conv_kernel, n_ctx=n_ctx, n_lat=n_lat, rows=256)
    return pl.pallas_call(
        kern,
        grid=(XBC_D // LANES,),
        in_specs=[pl.BlockSpec((t, LANES), lambda c: (0, col0 + c)),
                  pl.BlockSpec((None, SSM_CONV, LANES), lambda c: (l, 0, c)),
                  pl.BlockSpec((None, 1, LANES), lambda c: (l, 0, c))],
        out_specs=pl.BlockSpec((t, LANES), lambda c: (0, c)),
        out_shape=jax.ShapeDtypeStruct((t, XBC_D), BF16),
        scratch_shapes=[pltpu.VMEM((t + 24, LANES), F32)],
        compiler_params=_params(("arbitrary",)),
        name="conv_silu",
    )(p_main, conv_w, conv_b)


def _chunk_index(s, nc_ctx, nc, rev):
    if not rev:
        return s
    return jnp.where(s < nc_ctx, nc_ctx - 1 - s, nc - 1 - (s - nc_ctx))


def _split_dot(v, e2):
    hi, lo = _split_bf16(v)
    return jnp.dot(jnp.concatenate([hi, lo], axis=1), e2, preferred_element_type=F32)


def _ssd_kernel(xs_f, b_f, c_f, dt_f, xs_b, b_b, c_b, dt_b, bias_ref, alog_ref, e_ref,
                of_ref, ob_ref, h_s):
    @pl.when(pl.program_id(0) == 0)
    def _():
        h_s[...] = jnp.zeros_like(h_s)

    _ssd_chunk(xs_f, b_f, c_f, dt_f, bias_ref[0:1, :], alog_ref[0:1, :], e_ref, of_ref, h_s.at[0], False)
    _ssd_chunk(xs_b, b_b, c_b, dt_b, bias_ref[1:2, :], alog_ref[1:2, :], e_ref, ob_ref, h_s.at[1], True)


def _ssd_chunk(xs_ref, b_ref, c_ref, dt_ref, bias, alog, e, o_ref, h_s, rev):
    q = CHUNK
    ii = lax.broadcasted_iota(jnp.int32, (q, q), 0)
    jj = lax.broadcasted_iota(jnp.int32, (q, q), 1)
    seen = (jj >= ii) if rev else (jj <= ii)
    tri = jnp.where(seen, 1.0, 0.0).astype(F32)

    dt = _softplus(dt_ref[...] + bias)
    a = dt * (-jnp.exp(alog))
    cum = jnp.dot(tri, a, precision=HIGHEST, preferred_element_type=F32)
    cum_t = cum.T
    dt_t = dt.T
    total = cum[0:1, :] if rev else cum[q - 1:q, :]
    dfs = jnp.exp(cum)
    w_end = dt * jnp.exp(total - cum)
    cdec = jnp.broadcast_to(jnp.exp(total), (8, LANES))
    hi, lo = _split_bf16(jnp.concatenate([w_end, dfs, cdec], axis=0))
    v_hl = jnp.concatenate([hi, lo], axis=1)
    xs = xs_ref[...]
    first_head = jj < SSM_HEADDIM

    for g in range(SSM_GROUPS):
        cg = c_ref[:, g * SSM_STATE:(g + 1) * SSM_STATE]
        bg = b_ref[:, g * SSM_STATE:(g + 1) * SSM_STATE]
        gs = slice(g * GROUP_W, (g + 1) * GROUP_W)
        cb = lax.dot_general(cg, bg, (((1,), (1,)), ((), ())), preferred_element_type=F32)
        ex = jnp.dot(v_hl, e[:, gs], preferred_element_type=F32)
        xw_g = (xs[:, gs].astype(F32) * ex[0:q]).astype(BF16)
        h_prev = h_s[:, gs]
        y_off = jnp.dot(cg, h_prev.astype(BF16), preferred_element_type=F32) * ex[q:2 * q]
        new = lax.dot_general(bg, xw_g, (((0,), (0,)), ((), ())), preferred_element_type=F32)
        pieces = []
        for hh in range(0, SSM_HPG, 2):
            ms = []
            for hd in (g * SSM_HPG + hh, g * SSM_HPG + hh + 1):
                seg = cum[:, hd:hd + 1] - cum_t[hd:hd + 1, :]
                m = jnp.exp(jnp.where(seen, seg, NEG_BIG)) * cb * dt_t[hd:hd + 1, :]
                ms.append(m.astype(BF16))
            c0 = (g * SSM_HPG + hh) * SSM_HEADDIM
            xp = xs[:, c0:c0 + LANES]
            zero = jnp.zeros_like(xp)
            rhs = jnp.concatenate([jnp.where(first_head, xp, zero), jnp.where(first_head, zero, xp)],
                                  axis=0)
            pieces.append(jnp.dot(jnp.concatenate(ms, axis=1), rhs, preferred_element_type=F32))
        y = jnp.concatenate(pieces, axis=1) + y_off
        o_ref[:, gs] = y.astype(o_ref.dtype)
        h_s[:, gs] = h_prev * ex[2 * q:2 * q + 1] + new


def _scan_specs(n_ctx, t, col_blocks, width):
    nc, nc_ctx = t // CHUNK, n_ctx // CHUNK
    specs = []
    for rev in (False, True):
        cidx = functools.partial(_chunk_index, nc_ctx=nc_ctx, nc=nc, rev=rev)
        specs.append([pl.BlockSpec((CHUNK, w), lambda s, c=c, cidx=cidx: (cidx(s), c))
                      for c, w in zip(col_blocks, width)])
    return specs


def _ssd_scan(xbc_act, dt_raw, dt_bias, a_log, e_mat, n_ctx):
    t = xbc_act.shape[0]
    bc0 = D_MODEL // BC_W
    fwd, bwd = _scan_specs(n_ctx, t, (0, bc0, bc0 + 1, 0), (D_MODEL, BC_W, BC_W, LANES))
    out_f, out_b = _scan_specs(n_ctx, t, (0,), (D_MODEL,))
    const = [pl.BlockSpec((2, LANES), lambda s: (0, 0)),
             pl.BlockSpec((2, LANES), lambda s: (0, 0)),
             pl.BlockSpec((2 * LANES, D_MODEL), lambda s: (0, 0))]
    seqs = (xbc_act, xbc_act, xbc_act, dt_raw)
    return pl.pallas_call(
        _ssd_kernel,
        grid=(t // CHUNK,),
        in_specs=fwd + bwd + const,
        out_specs=out_f + out_b,
        out_shape=[jax.ShapeDtypeStruct((t, D_MODEL), BF16)] * 2,
        scratch_shapes=[pltpu.VMEM((2, SSM_STATE, D_MODEL), F32)],
        compiler_params=_params(("arbitrary",)),
        name="ssd_scan",
    )(*seqs, *seqs, dt_bias, a_log, e_mat)


def _ret_kernel(q_f, k_f, v_f, q_b, k_b, v_b, dl_ref, of_ref, ob_ref, r_s, dm_s, kd_s, qd_s, cd_s):
    qn = CHUNK

    @pl.when(pl.program_id(0) == 0)
    def _():
        lg_both = -_softplus(-dl_ref[...])
        cd_s[...] = jnp.exp(float(qn) * lg_both)
        r_s[...] = jnp.zeros_like(r_s)
        ii = lax.broadcasted_iota(jnp.int32, (qn, qn), 0)
        jj = lax.broadcasted_iota(jnp.int32, (qn, qn), 1)
        for di, rev in enumerate((False, True)):
            diff = (jj - ii) if rev else (ii - jj)
            dpos = jnp.maximum(diff, 0).astype(F32)
            k_exp = (ii if rev else (qn - 1 - ii)).astype(F32)
            q_exp = ((qn - ii) if rev else (ii + 1)).astype(F32)
            for hd in range(RET_HEADS):
                lg = lg_both[di:di + 1, hd:hd + 1]
                dm_s[di, hd] = jnp.where(diff >= 0, jnp.exp(dpos * lg), 0.0)
                kd_s[di, hd] = jnp.exp(k_exp * lg)
                qd_s[di, hd] = jnp.exp(q_exp * lg)

    _ret_chunk(q_f, k_f, v_f, cd_s[0:1, :], of_ref, r_s.at[0], dm_s.at[0], kd_s.at[0], qd_s.at[0])
    _ret_chunk(q_b, k_b, v_b, cd_s[1:2, :], ob_ref, r_s.at[1], dm_s.at[1], kd_s.at[1], qd_s.at[1])


def _ret_chunk(q_ref, k_ref, v_ref, chunk_decay, o_ref, r_s, dm_s, kd_s, qd_s):
    for hd in range(RET_HEADS):
        hs = slice(hd * RET_DK, (hd + 1) * RET_DK)
        qh = q_ref[:, hs]
        kh = k_ref[:, hs]
        vh = v_ref[:, hs]
        sc = lax.dot_general(qh, kh, (((1,), (1,)), ((), ())), preferred_element_type=F32)
        y = jnp.dot((sc * dm_s[hd]).astype(BF16), vh, preferred_element_type=F32)
        r_prev = r_s[hd]
        qd = qd_s[hd]
        y_int = jnp.dot(qh, r_prev.astype(BF16), preferred_element_type=F32)
        y = y + y_int * jnp.concatenate([qd, qd], axis=1)
        kd = kd_s[hd]
        kdec = (kh.astype(F32) * jnp.concatenate([kd, kd], axis=1)).astype(BF16)
        kv = lax.dot_general(kdec, vh, (((0,), (0,)), ((), ())), preferred_element_type=F32)
        r_s[hd] = r_prev * chunk_decay[:, hd:hd + 1] + kv
        o_ref[:, hs] = y.astype(o_ref.dtype)


def _ret_scan(p_main, decay_logit, n_ctx):
    t = p_main.shape[0]
    fwd, bwd = _scan_specs(n_ctx, t, (1, 2, 3), (D_MODEL,) * 3)
    out_f, out_b = _scan_specs(n_ctx, t, (0,), (D_MODEL,))
    seqs = (p_main, p_main, p_main)
    decay_tables = pltpu.VMEM((2, RET_HEADS, CHUNK, CHUNK), F32)
    return pl.pallas_call(
        _ret_kernel,
        grid=(t // CHUNK,),
        in_specs=fwd + bwd + [pl.BlockSpec((2, LANES), lambda s: (0, 0))],
        out_specs=out_f + out_b,
        out_shape=[jax.ShapeDtypeStruct((t, D_MODEL), BF16)] * 2,
        scratch_shapes=[pltpu.VMEM((2, RET_HEADS, RET_DK, RET_DK), F32),
                        decay_tables, decay_tables, decay_tables, pltpu.VMEM((2, LANES), F32)],
        compiler_params=_params(("arbitrary",)),
        name="ret_scan",
    )(*seqs, *seqs, decay_logit)


def _merge_kernel(ysf_ref, ysb_ref, xs_ref, z_ref, yrf_ref, yrb_ref, g_ref, gs_ref, gr_ref,
                  dsk_ref, snw_ref, rnw_ref, ws_ref, wr_ref, o_ref):
    y = (ysf_ref[...].astype(F32) + ysb_ref[...].astype(F32)
         + dsk_ref[...] * xs_ref[...].astype(F32))
    y = y * _silu(z_ref[...].astype(F32))
    ms = jnp.mean(y * y, axis=-1, keepdims=True)
    a = (y * lax.rsqrt(ms + EPS) * snw_ref[...]).astype(BF16)
    merged = _sigmoid(gs_ref[...].astype(F32)) * jnp.dot(a, ws_ref[...],
                                                         preferred_element_type=F32)
    gate = _silu(g_ref[...].astype(F32)) * rnw_ref[...]
    parts = []
    for hd in range(RET_HEADS):
        hs = slice(hd * RET_DK, (hd + 1) * RET_DK)
        r = yrf_ref[:, hs].astype(F32) + yrb_ref[:, hs].astype(F32)
        ms = jnp.mean(r * r, axis=-1, keepdims=True)
        parts.append((r * lax.rsqrt(ms + EPS) * gate[:, hs]).astype(BF16))
    b = jnp.concatenate(parts, axis=1)
    merged = merged + _sigmoid(gr_ref[...].astype(F32)) * jnp.dot(
        b, wr_ref[...], preferred_element_type=F32)
    o_ref[...] = merged.astype(BF16)


def _merge(y_ssd, xbc_act, p_main, y_ret, d_skip, ssm_norm_w, ret_norm_w, w_ssd, w_ret, l):
    t, d = xbc_act.shape[0], D_MODEL
    bm = MERGE_ROW_BLOCK
    once = pl.Buffered(1)
    return pl.pallas_call(
        _merge_kernel,
        grid=(t // bm,),
        in_specs=[pl.BlockSpec((bm, d), lambda i: (i, 0)),
                  pl.BlockSpec((bm, d), lambda i: (i, 0)),
                  pl.BlockSpec((bm, d), lambda i: (i, 0)),
                  pl.BlockSpec((bm, d), lambda i: (i, 0)),
                  pl.BlockSpec((bm, d), lambda i: (i, 0)),
                  pl.BlockSpec((bm, d), lambda i: (i, 0)),
                  pl.BlockSpec((bm, d), lambda i: (i, 4)),
                  pl.BlockSpec((bm, d), lambda i: (i, 5)),
                  pl.BlockSpec((bm, d), lambda i: (i, 6)),
                  pl.BlockSpec((None, 1, d), lambda i: (l, 0, 0)),
                  pl.BlockSpec((None, 1, d), lambda i: (l, 0, 0)),
                  pl.BlockSpec((None, 1, d), lambda i: (l, 0, 0)),
                  pl.BlockSpec((None, d, d), lambda i: (l, 0, 0), pipeline_mode=once),
                  pl.BlockSpec((None, d, d), lambda i: (l, 0, 0), pipeline_mode=once)],
        out_specs=pl.BlockSpec((bm, d), lambda i: (i, 0)),
        out_shape=jax.ShapeDtypeStruct((t, d), BF16),
        compiler_params=_params(("arbitrary",)),
        name="merge",
    )(*y_ssd, xbc_act, p_main, *y_ret, p_main, p_main, p_main, d_skip, ssm_norm_w, ret_norm_w,
      w_ssd, w_ret)


def _pack_bf16_pairs(v):
    n = v.shape[1] // 2
    bits = lax.bitcast_convert_type(v.astype(BF16).astype(F32), jnp.uint32)
    return bits[:, :n] | (bits[:, n:] >> 16)


def _unpack_bf16_pairs(p):
    hi = lax.bitcast_convert_type(p & jnp.uint32(0xFFFF0000), F32)
    lo = lax.bitcast_convert_type(p << 16, F32)
    return jnp.concatenate([hi, lo], axis=1).astype(BF16)


def _route_rows(x, i, sh_ref, sc_ref, nw_ref, wr_ref, br_ref, xn_ref, meta_ref, cnt_ref,
                carry_s, *, n_ctx, bm):
    @pl.when(i == 0)
    def _():
        carry_s[...] = jnp.zeros_like(carry_s)

    xn = _modulated_norm(x, nw_ref[...], sh_ref, sc_ref, i * bm, n_ctx)
    xn_ref[...] = _pack_bf16_pairs(xn)
    logits = _dot3(xn, wr_ref[...]) + br_ref[...]
    lane = lax.broadcasted_iota(jnp.int32, (bm, LANES), 1)
    far = jnp.int32(4 * LANES)

    glog = jnp.where(lane < N_GROUPS_E, logits, NEG_BIG)
    gmax = jnp.max(glog, axis=1, keepdims=True)
    gidx = jnp.min(jnp.where(glog == gmax, lane, far), axis=1, keepdims=True)
    p_top = 1.0 / jnp.sum(jnp.exp(glog - gmax), axis=1, keepdims=True)

    in_group = ((lane >= N_GROUPS_E) & (lane < N_GROUPS_E + N_EXPERTS)
                & (((lane - N_GROUPS_E) >> 3) == gidx))
    elog = jnp.where(in_group, logits, NEG_BIG)
    m1 = jnp.max(elog, axis=1, keepdims=True)
    i1 = jnp.min(jnp.where(elog == m1, lane, far), axis=1, keepdims=True)
    elog2 = jnp.where(lane == i1, NEG_BIG, elog)
    m2 = jnp.max(elog2, axis=1, keepdims=True)
    i2 = jnp.min(jnp.where(elog2 == m2, lane, far), axis=1, keepdims=True)
    e2 = jnp.exp(m2 - m1)
    w1 = p_top / (1.0 + e2)
    w2 = p_top * e2 / (1.0 + e2)
    ex1 = i1 - N_GROUPS_E
    ex2 = i2 - N_GROUPS_E

    onehot = jnp.where((lane == ex1) | (lane == ex2), 1.0, 0.0).astype(F32)
    ri = lax.broadcasted_iota(jnp.int32, (bm, bm), 0)
    ci = lax.broadcasted_iota(jnp.int32, (bm, bm), 1)
    earlier = jnp.where(ci < ri, 1.0, 0.0).astype(BF16)
    before = jnp.dot(earlier, onehot.astype(BF16), preferred_element_type=F32) + carry_s[0:1, :]
    r1 = jnp.sum(jnp.where(lane == ex1, before, 0.0), axis=1, keepdims=True)
    r2 = jnp.sum(jnp.where(lane == ex2, before, 0.0), axis=1, keepdims=True)
    carry = carry_s[...] + jnp.sum(onehot, axis=0, keepdims=True)
    carry_s[...] = carry
    cnt_ref[...] = carry

    meta = jnp.where(lane == 0, ex1.astype(F32), 0.0)
    meta = jnp.where(lane == 1, ex2.astype(F32), meta)
    meta = jnp.where(lane == 2, w1, meta)
    meta = jnp.where(lane == 3, w2, meta)
    meta = jnp.where(lane == 4, r1, meta)
    meta = jnp.where(lane == 5, r2, meta)
    meta_ref[...] = meta


def _outproj_kernel(m_ref, w_ref, h_ref, g_ref, sh_ref, sc_ref, nw_ref, wr_ref, br_ref,
                    o_ref, xn_ref, meta_ref, cnt_ref, carry_s, *, n_ctx, bm):
    i = pl.program_id(0)
    row = i * bm + lax.broadcasted_iota(jnp.int32, (bm, 1), 0)
    gate = jnp.where(row < n_ctx, g_ref[1:2, :], g_ref[0:1, :])
    h_new = h_ref[...] + gate * jnp.dot(m_ref[...], w_ref[...], preferred_element_type=F32)
    o_ref[...] = h_new
    _route_rows(h_new, i, sh_ref, sc_ref, nw_ref, wr_ref, br_ref, xn_ref, meta_ref, cnt_ref,
                carry_s, n_ctx=n_ctx, bm=bm)


def _outproj_route(merged, w_out, h, mod, norm_w, w_route, b_route, l, n_ctx):
    t, d = h.shape
    bm = OUT_ROW_BLOCK
    return pl.pallas_call(
        functools.partial(_outproj_kernel, n_ctx=n_ctx, bm=bm),
        grid=(t // bm,),
        in_specs=[pl.BlockSpec((bm, d), lambda i: (i, 0)),
                  pl.BlockSpec((None, d, d), lambda i: (l, 0, 0), pipeline_mode=pl.Buffered(1)),
                  pl.BlockSpec((bm, d), lambda i: (i, 0)),
                  pl.BlockSpec((None, 8, d), lambda i: (l, 0, 2)),
                  pl.BlockSpec((None, 8, d), lambda i: (l, 0, 3)),
                  pl.BlockSpec((None, 8, d), lambda i: (l, 0, 4)),
                  pl.BlockSpec((None, 1, d), lambda i: (l, 0, 0)),
                  pl.BlockSpec((None, d, LANES), lambda i: (l, 0, 0)),
                  pl.BlockSpec((None, 1, LANES), lambda i: (l, 0, 0))],
        out_specs=[pl.BlockSpec((bm, d), lambda i: (i, 0)),
                   pl.BlockSpec((bm, d // 2), lambda i: (i, 0)),
                   pl.BlockSpec((bm, LANES), lambda i: (i, 0)),
                   pl.BlockSpec((8, LANES), lambda i: (0, 0))],
        out_shape=[jax.ShapeDtypeStruct((t, d), F32),
                   jax.ShapeDtypeStruct((t, d // 2), jnp.uint32),
                   jax.ShapeDtypeStruct((t, LANES), F32),
                   jax.ShapeDtypeStruct((8, LANES), F32)],
        scratch_shapes=[pltpu.VMEM((8, LANES), F32)],
        compiler_params=_params(("arbitrary",)),
        name="outproj_route",
    )(merged, w_out, h, mod, mod, mod, norm_w, w_route, b_route)


def _row_copy(src, src_row, dst, dst_row, sem):
    return pltpu.make_async_copy(src.at[pl.ds(src_row, 1), :], dst.at[pl.ds(dst_row, 1), :], sem)


def _dispatch_kernel(pos_ref, xn_ref, init_ref, o_ref, sem, *, bm, t):
    del init_ref
    base = pl.program_id(0) * bm

    def issue(g, carry):
        r0 = g * DMA_GROUP
        ps = [[pos_ref[k * t + base + r0 + u] for k in range(2)] for u in range(DMA_GROUP)]
        for u in range(DMA_GROUP):
            for k in range(2):
                _row_copy(xn_ref, r0 + u, o_ref, ps[u][k], sem).start(priority=k)
        return carry

    lax.fori_loop(0, bm // DMA_GROUP, issue, 0)
    for k in range(2):
        pltpu.make_async_copy(xn_ref, o_ref.at[pl.ds(0, bm), :], sem).wait()


def _dispatch(pos, xn, init):
    t, d = xn.shape
    n_rows = init.shape[0]
    bm = TOK_BLOCK
    return pl.pallas_call(
        functools.partial(_dispatch_kernel, bm=bm, t=t),
        grid_spec=pltpu.PrefetchScalarGridSpec(
            num_scalar_prefetch=1,
            grid=(t // bm,),
            in_specs=[pl.BlockSpec((bm, d), lambda i, pos: (i, 0)),
                      pl.BlockSpec(memory_space=pl.ANY)],
            out_specs=pl.BlockSpec(memory_space=pl.ANY),
            scratch_shapes=[pltpu.SemaphoreType.DMA(())]),
        out_shape=jax.ShapeDtypeStruct((n_rows, d), xn.dtype),
        input_output_aliases={2: 0},
        compiler_params=_params(("arbitrary",)),
        name="dispatch",
    )(pos, xn, init)


def _expert_kernel(te_ref, nx_ref, xi_ref, first_ref, valid_ref, x_ref, wg_hbm, wu_hbm, wd_hbm,
                   o_ref, lg_s, lu_s, ld_s, wg_s, wu_s, wd_s, sem, *, l):
    i = pl.program_id(0)

    weights = ((wg_hbm, lg_s, wg_s), (wu_hbm, lu_s, wu_s), (wd_hbm, ld_s, wd_s))

    def fetch(k, e):
        return pltpu.make_async_copy(weights[k][0].at[l, e], weights[k][1], sem.at[k])

    @pl.when(i == 0)
    def _():
        for k in range(3):
            fetch(k, te_ref[0]).start()

    @pl.when(first_ref[i] == 1)
    def _():
        has_next = nx_ref[i] >= 0
        nxt = jnp.maximum(nx_ref[i], 0)
        for k, (_, land, work) in enumerate(weights):
            fetch(k, te_ref[i]).wait()
            work[...] = land[...].astype(BF16)

            @pl.when(has_next)
            def _():
                fetch(k, nxt).start()

    @pl.when(valid_ref[i] == 1)
    def _():
        x = _unpack_bf16_pairs(x_ref[...])
        gate = jnp.dot(x, wg_s[...], preferred_element_type=F32)
        up = jnp.dot(x, wu_s[...], preferred_element_type=F32)
        mid = (_silu(gate) * up).astype(BF16)
        o_ref[...] = jnp.dot(mid, wd_s[...], preferred_element_type=F32)

    @pl.when(valid_ref[i] == 0)
    def _():
        o_ref[...] = jnp.zeros_like(o_ref)


def _experts(tile_meta, x_sorted, w_gate, w_up, w_down, l, n_tiles):
    d = D_MODEL
    tm = EXPERT_TILE
    hbm = pl.BlockSpec(memory_space=pl.ANY)
    return pl.pallas_call(
        functools.partial(_expert_kernel, l=l),
        grid_spec=pltpu.PrefetchScalarGridSpec(
            num_scalar_prefetch=5,
            grid=(n_tiles,),
            in_specs=[pl.BlockSpec((tm, d // 2), lambda i, te, nx, xi, fi, va: (xi[i], 0)),
                      hbm, hbm, hbm],
            out_specs=pl.BlockSpec((tm, d), lambda i, te, nx, xi, fi, va: (i, 0)),
            scratch_shapes=[pltpu.VMEM((d, D_EXPERT), F32), pltpu.VMEM((d, D_EXPERT), F32),
                            pltpu.VMEM((D_EXPERT, d), F32),
                            pltpu.VMEM((d, D_EXPERT), BF16), pltpu.VMEM((d, D_EXPERT), BF16),
                            pltpu.VMEM((D_EXPERT, d), BF16),
                            pltpu.SemaphoreType.DMA((3,))]),
        out_shape=jax.ShapeDtypeStruct((n_tiles * tm, d), F32),
        compiler_params=_params(("arbitrary",)),
        name="experts",
    )(*tile_meta, x_sorted, w_gate, w_up, w_down)


def _combine_kernel(pos_ref, h_ref, meta_ref, g_ref, ys_ref, o_ref, buf_s, sem, *, n_ctx, bm, t):
    i = pl.program_id(0)
    n = pl.num_programs(0)

    def issue(blk, slot):
        base = blk * bm

        def body(g, carry):
            r0 = g * DMA_GROUP
            ps = [[pos_ref[k * t + base + r0 + u] for k in range(2)] for u in range(DMA_GROUP)]
            for u in range(DMA_GROUP):
                for k in range(2):
                    _row_copy(ys_ref, ps[u][k], buf_s.at[slot, k], r0 + u,
                              sem.at[slot]).start(priority=k)
            return carry

        lax.fori_loop(0, bm // DMA_GROUP, body, 0)

    @pl.when(i == 0)
    def _():
        issue(0, 0)

    @pl.when(i + 1 < n)
    def _():
        issue(i + 1, (i + 1) % 2)

    slot = i % 2
    for k in range(2):
        pltpu.make_async_copy(ys_ref.at[pl.ds(0, bm), :], buf_s.at[slot, k], sem.at[slot]).wait()
    meta = meta_ref[...]
    moe = meta[:, 2:3] * buf_s[slot, 0] + meta[:, 3:4] * buf_s[slot, 1]
    row = i * bm + lax.broadcasted_iota(jnp.int32, (bm, 1), 0)
    gate = jnp.where(row < n_ctx, g_ref[1:2, :], g_ref[0:1, :])
    o_ref[...] = h_ref[...] + gate * moe


def _combine(pos, h, meta, mod, y_sorted, l, n_ctx):
    t, d = h.shape
    bm = TOK_BLOCK
    return pl.pallas_call(
        functools.partial(_combine_kernel, n_ctx=n_ctx, bm=bm, t=t),
        grid_spec=pltpu.PrefetchScalarGridSpec(
            num_scalar_prefetch=1,
            grid=(t // bm,),
            in_specs=[pl.BlockSpec((bm, d), lambda i, pos: (i, 0)),
                      pl.BlockSpec((bm, LANES), lambda i, pos: (i, 0)),
                      pl.BlockSpec((None, 8, d), lambda i, pos: (l, 0, 5)),
                      pl.BlockSpec(memory_space=pl.ANY)],
            out_specs=pl.BlockSpec((bm, d), lambda i, pos: (i, 0)),
            scratch_shapes=[pltpu.VMEM((2, 2, bm, d), F32), pltpu.SemaphoreType.DMA((2,))]),
        out_shape=jax.ShapeDtypeStruct((t, d), F32),
        compiler_params=_params(("arbitrary",)),
        name="combine",
    )(pos, h, meta, mod, y_sorted)


def _tile_plan(meta, counts, n_tiles):
    tm = EXPERT_TILE
    cnt = counts[0, :N_EXPERTS].astype(jnp.int32)
    tiles_e = (cnt + tm - 1) // tm
    tile_end = jnp.cumsum(tiles_e)
    offset = (tile_end - tiles_e) * tm
    meta_t = meta[:, :8].T
    expert = meta_t[0:2].astype(jnp.int32)
    rank = meta_t[4:6].astype(jnp.int32)
    ids = jnp.arange(N_EXPERTS, dtype=jnp.int32)[:, None, None]
    base = jnp.sum(jnp.where(expert[None] == ids, offset[:, None, None], 0), axis=0)
    pos = (base + rank).reshape(-1)
    n_used = tile_end[-1]
    tid = jnp.arange(n_tiles, dtype=jnp.int32)
    valid = tid < n_used
    tsrc = jnp.minimum(tid, n_used - 1)
    def expert_of_tile(v):
        return jnp.minimum(jnp.sum((tile_end[None, :] <= v[:, None]).astype(jnp.int32), axis=1),
                           N_EXPERTS - 1)

    te = expert_of_tile(tsrc)
    first = (valid & ((tid == 0) | (te != jnp.roll(te, 1)))).astype(jnp.int32)
    after = jnp.sum(jnp.where(te[:, None] == jnp.arange(N_EXPERTS)[None, :], tile_end[None, :], 0),
                    axis=1)
    nxt = jnp.where(after < n_used, expert_of_tile(after), -1).astype(jnp.int32)
    return pos, (te, nxt, tsrc.astype(jnp.int32), first, valid.astype(jnp.int32))


def _final_kernel(h_ref, w_ref, o_ref):
    x = h_ref[...]
    ms = jnp.mean(x * x, axis=-1, keepdims=True)
    o_ref[...] = x * lax.rsqrt(ms + EPS) * w_ref[...]


def _final_norm(h, w, n_ctx):
    t, d = h.shape
    bm = TOK_BLOCK
    skip = n_ctx // bm
    return pl.pallas_call(
        _final_kernel,
        grid=((t - n_ctx) // bm,),
        in_specs=[pl.BlockSpec((bm, d), lambda i: (i + skip, 0)),
                  pl.BlockSpec((1, d), lambda i: (0, 0))],
        out_specs=pl.BlockSpec((bm, d), lambda i: (i, 0)),
        out_shape=jax.ShapeDtypeStruct((t - n_ctx, d), F32),
        compiler_params=_params(("arbitrary",)),
        name="final_norm",
    )(h, w)


def _pad_lanes(v):
    return jnp.pad(v, [(0, 0)] * (v.ndim - 1) + [(0, LANES - v.shape[-1])])


def _rope_tables(n_ctx, n_lat):
    rows = n_lat // GRID_W
    row = jnp.broadcast_to(jnp.arange(rows, dtype=F32)[:, None], (rows, GRID_W)).reshape(n_lat)
    col = jnp.broadcast_to(jnp.arange(GRID_W, dtype=F32)[None, :], (rows, GRID_W)).reshape(n_lat)
    inv_freq = ROPE_BASE ** (-jnp.arange(ROPE_FREQS, dtype=F32) / ROPE_FREQS)
    ang_r = row[:, None] * inv_freq[None, :]
    ang_c = col[:, None] * inv_freq[None, :]
    cos_l = jnp.concatenate([jnp.cos(ang_r)] * 2 + [jnp.cos(ang_c)] * 2, axis=1)
    sin_l = jnp.concatenate([-jnp.sin(ang_r), jnp.sin(ang_r), -jnp.sin(ang_c), jnp.sin(ang_c)], axis=1)
    cos_t = jnp.concatenate([jnp.ones((n_ctx, 2 * LANES), F32), cos_l], axis=0)
    sin_t = jnp.concatenate([jnp.zeros((n_ctx, 2 * LANES), F32), sin_l], axis=0)
    return cos_t, sin_t


def kernel(x, c, ctx, c_ctx, w_mod, b_mod, norm1_w, w_in, conv_w, conv_b, ssm_A_log, ssm_dt_bias,
           ssm_D, ssm_norm_w, ret_decay_logit, ret_norm_w, w_ssd_proj, w_ret_proj, w_out, norm2_w,
           w_route_group, b_route_group, w_route_expert, b_route_expert, w_exp_gate, w_exp_up,
           w_exp_down, final_norm_w):
    batch, n_lat, d = x.shape
    n_ctx = ctx.shape[1]
    depth = w_mod.shape[0]
    assert batch == 1 and d == D_MODEL and w_in.shape[-1] == MAIN_W + DT_W
    t = n_ctx + n_lat
    assert t % OUT_ROW_BLOCK == 0 and n_ctx % TOK_BLOCK == 0 and n_lat % TOK_BLOCK == 0

    w_nat, w_dt = _wprep(jnp.swapaxes(w_in, 1, 2))
    w_ssd_b = w_ssd_proj.astype(BF16)
    w_ret_b = w_ret_proj.astype(BF16)
    w_out_b = w_out.astype(BF16)
    w_route = _pad_lanes(jnp.concatenate([w_route_group, w_route_expert], axis=-1))
    b_route = _pad_lanes(jnp.concatenate([b_route_group, b_route_expert], axis=-1))[:, None, :]
    d_skip = jnp.repeat(ssm_D, SSM_HEADDIM, axis=-1)[:, None, :]
    dt_bias = _pad_lanes(ssm_dt_bias)
    a_log = _pad_lanes(ssm_A_log)
    ret_dl = _pad_lanes(ret_decay_logit)
    head_of_col = jnp.arange(D_MODEL, dtype=jnp.int32) // SSM_HEADDIM
    e_mat = ((jnp.arange(2 * LANES, dtype=jnp.int32) % LANES)[:, None]
             == head_of_col[None, :]).astype(BF16)
    cos_t, sin_t = _rope_tables(n_ctx, n_lat)

    cvec = jnp.concatenate([c, c_ctx[None, :], jnp.zeros((6, d), F32)], axis=0)
    mod = _modulation(cvec, w_mod, b_mod)

    n_tiles = (2 * t) // EXPERT_TILE + N_EXPERTS
    h = jnp.concatenate([ctx[0], x[0]], axis=0)
    x_sorted = jnp.zeros((n_tiles * EXPERT_TILE, d // 2), jnp.uint32)
    for l in range(depth):
        p_main, dt_raw = _inproj(h, mod, norm1_w[:, None, :], w_nat, w_dt, cos_t, sin_t, l, n_ctx)
        xbc_act = _conv(p_main, conv_w, conv_b[:, None, :], l, n_ctx)
        y_ssd = _ssd_scan(xbc_act, dt_raw, dt_bias[l], a_log[l], e_mat, n_ctx)
        y_ret = _ret_scan(p_main, ret_dl[l], n_ctx)
        merged = _merge(y_ssd, xbc_act, p_main, y_ret, d_skip, ssm_norm_w[:, None, :],
                        ret_norm_w[:, None, :], w_ssd_b, w_ret_b, l)
        h, xn, meta, counts = _outproj_route(merged, w_out_b, h, mod, norm2_w[:, None, :], w_route,
                                             b_route, l, n_ctx)
        pos, tile_meta = _tile_plan(meta, counts, n_tiles)
        x_sorted = _dispatch(pos, xn, x_sorted)
        y_sorted = _experts(tile_meta, x_sorted, w_exp_gate, w_exp_up, w_exp_down, l, n_tiles)
        h = _combine(pos, h, meta, mod, y_sorted, l, n_ctx)

    return _final_norm(h, final_norm_w[None, :], n_ctx)[None]
```
